```python
import math
import jax, jax.numpy as jnp
from jax import lax
import numpy as np

D_MODEL = 1024
BATCH = 8
SEQ = 4096
DEPTH = 4

GRID_W = 64
CTX_LEN = 256
D_FF = 4 * D_MODEL
CONV_K = 4
CONV_LEFT = 2
GDN_HEADS = 4
GDN_DK = 128
GDN_DV = 128
GDN_W = GDN_HEADS * GDN_DK
GDN_CHUNK = 64
LRU_W = D_MODEL - GDN_HEADS * GDN_DV
LRU_BLOCKS = 8
LRU_BW = LRU_W // LRU_BLOCKS
LRU_C = 8.0
EV_IN = 4 * GDN_W + 4 * GDN_HEADS + 2 * LRU_W
DIFF_HEADS = 8
DIFF_D = D_MODEL // DIFF_HEADS // 2
DIFF_DV = 2 * DIFF_D
Q_BLOCK = 128
ROPE_THETA = 10000.0
N_EVEN = (DEPTH + 1) // 2
N_ODD = DEPTH // 2
DEEPNORM_ALPHA = (2 * DEPTH) ** 0.25
DEEPNORM_BETA = (8 * DEPTH) ** -0.25
NORM_EPS = 1e-6

kernel_name = 'hybrid_gdn_rglru_diffattn_prefix_trunk'

F32 = jnp.float32


def _layernorm(x, g, b):
    xf = x.astype(F32)
    mu = jnp.mean(xf, -1, keepdims=True)
    var = jnp.mean(jnp.square(xf - mu), -1, keepdims=True)
    return ((xf - mu) * lax.rsqrt(var + NORM_EPS) * g + b).astype(x.dtype)


def _rmsnorm(x, g):
    xf = x.astype(F32)
    return xf * lax.rsqrt(jnp.mean(xf * xf, -1, keepdims=True) + NORM_EPS) * g.astype(F32)


def _l2norm(x):
    return x * lax.rsqrt(jnp.sum(x * x, -1, keepdims=True) + NORM_EPS)


def _modulate(h, shift, scale):
    return h * (1 + scale) + shift


def _dwconv(x, w):
    k = w.shape[0]
    return lax.conv_general_dilated(x, w[:, None, :], window_strides=(1,),
                                    padding=[(CONV_LEFT, k - 1 - CONV_LEFT)],
                                    dimension_numbers=('NWC', 'WIO', 'NWC'),
                                    feature_group_count=x.shape[-1])


def _axial_rope_tables(row, col):
    half = DIFF_D // 2
    inv = ROPE_THETA ** (-(jnp.arange(0, half, 2, dtype=F32) / half))
    ang_r = row.astype(F32)[:, None] * inv
    ang_c = col.astype(F32)[:, None] * inv
    ang = jnp.concatenate([ang_r, ang_r, ang_c, ang_c], axis=-1)
    return jnp.cos(ang), jnp.sin(ang)


def _rope(t, cos, sin):
    t1, t2, t3, t4 = jnp.split(t.astype(F32), 4, axis=-1)
    rot = jnp.concatenate([-t2, t1, -t4, t3], axis=-1)
    return (t.astype(F32) * cos + rot * sin).astype(t.dtype)


def _gdn_chunk(q, k, v, beta, g, s0):
    bsz, nh, L, _ = q.shape
    n = L // GDN_CHUNK
    chunks = lambda t: t.reshape(bsz, nh, n, GDN_CHUNK, *t.shape[3:])
    q, k, v, beta, g = chunks(q), chunks(k), chunks(v), chunks(beta), chunks(g)
    gam = jnp.cumsum(g, axis=-1)
    idx = jnp.arange(GDN_CHUNK)
    incl = idx[:, None] >= idx[None, :]
    strict = idx[:, None] > idx[None, :]
    decay = jnp.exp(jnp.where(incl, gam[..., :, None] - gam[..., None, :], -jnp.inf))
    kb = k * beta[..., None]
    a_strict = jnp.where(strict, jnp.einsum('bhnid,bhnjd->bhnij', kb, k) * decay, 0.0)
    eye = jnp.eye(GDN_CHUNK, dtype=F32)
    t_mat = lax.linalg.triangular_solve(eye + a_strict, jnp.broadcast_to(eye, a_strict.shape),
                                        left_side=True, lower=True, unit_diagonal=True)
    u = t_mat @ (v * beta[..., None])
    w = t_mat @ (kb * jnp.exp(gam)[..., None])
    qk = jnp.einsum('bhnid,bhnjd->bhnij', q, k) * decay
    q_dec = q * jnp.exp(gam)[..., None]
    k_dec = k * jnp.exp(gam[..., -1:] - gam)[..., None]
    c_dec = jnp.exp(gam[..., -1])

    def step(s, xs):
        u_c, w_c, qk_c, qd_c, kd_c, cd_c = xs
        v_new = u_c - w_c @ s
        o = qd_c @ s + qk_c @ v_new
        s = s * cd_c[..., None, None] + jnp.einsum('bhcd,bhce->bhde', kd_c, v_new)
        return s, o

    xs = tuple(jnp.moveaxis(t, 2, 0) for t in (u, w, qk, q_dec, k_dec, c_dec))
    s_fin, o = lax.scan(step, s0, xs)
    o = jnp.moveaxis(o, 0, 2).reshape(bsz, nh, L, -1)
    return o, s_fin


def _gdn_prep(qkv, a, b, conv_w, a_log, dt_bias):
    bsz, L, _ = qkv.shape
    qkv = jax.nn.silu(_dwconv(qkv, conv_w)).astype(F32)
    q, k, v = jnp.split(qkv, 3, axis=-1)
    heads = lambda t: t.reshape(bsz, L, GDN_HEADS, -1).transpose(0, 2, 1, 3)
    q = _l2norm(heads(q)) * GDN_DK ** -0.5
    k = _l2norm(heads(k))
    v = heads(v)
    a = a.astype(F32).reshape(bsz, L, 2, GDN_HEADS)
    g = -jnp.exp(a_log.astype(F32)) * jax.nn.softplus(a + dt_bias.astype(F32))
    beta = jax.nn.sigmoid(b.astype(F32).reshape(bsz, L, 2, GDN_HEADS))
    return q, k, v, g.transpose(2, 0, 3, 1), beta.transpose(2, 0, 3, 1)


def _gdn_bidir(ctx_in, lat_in):
    qc, kc, vc, gc, bc = ctx_in
    ql, kl, vl, gl, bl = lat_in
    s0 = jnp.zeros((qc.shape[0], GDN_HEADS, GDN_DK, GDN_DV), F32)
    flip = lambda t: jnp.flip(t, axis=2)
    oc_f, s_f = _gdn_chunk(qc, kc, vc, bc[0], gc[0], s0)
    ol_f, _ = _gdn_chunk(ql, kl, vl, bl[0], gl[0], s_f)
    oc_b, s_b = _gdn_chunk(flip(qc), flip(kc), flip(vc), flip(bc[1]), flip(gc[1]), s0)
    ol_b, _ = _gdn_chunk(flip(ql), flip(kl), flip(vl), flip(bl[1]), flip(gl[1]), s_b)
    return oc_f + flip(oc_b), ol_f + flip(ol_b)


def _gdn_out(o, z, norm_g):
    bsz, _, L, _ = o.shape
    o = _rmsnorm(o.transpose(0, 2, 1, 3), norm_g)
    zz = jax.nn.silu(z.astype(F32)).reshape(bsz, L, GDN_HEADS, GDN_DV)
    return (o * zz).reshape(bsz, L, GDN_W)


def _lru_coeffs(xr, gate_w, gate_b, lam):
    bsz, L, _ = xr.shape
    xb = xr.reshape(bsz, L, LRU_BLOCKS, LRU_BW)
    gates = jnp.einsum('blnc,gncd->gblnd', xb, gate_w.astype(F32)).reshape(2, bsz, L, LRU_W)
    gates = gates + gate_b.astype(F32)[:, None, None, :]
    r = jax.nn.sigmoid(gates[0])
    i = jax.nn.sigmoid(gates[1])
    log_a = -LRU_C * r * jax.nn.softplus(-lam.astype(F32))
    a = jnp.exp(log_a)
    u = jnp.sqrt(-jnp.expm1(2.0 * log_a)) * (i * xr)
    return a, u


def _lin_scan(a, u, h0):
    u = u.at[:, 0].add(a[:, 0] * h0)
    comb = lambda l, r: (l[0] * r[0], r[0] * l[1] + r[1])
    _, h = lax.associative_scan(comb, (a, u), axis=1)
    return h


def _lru_bidir(xr_c, xr_l, gate_w, gate_b, lam):
    h0 = jnp.zeros((xr_c.shape[0], LRU_W), F32)
    flip = lambda t: jnp.flip(t, axis=1)
    a, u = _lru_coeffs(xr_c, gate_w[0], gate_b[0], lam[0])
    hc_f = _lin_scan(a, u, h0)
    a, u = _lru_coeffs(xr_l, gate_w[0], gate_b[0], lam[0])
    hl_f = _lin_scan(a, u, hc_f[:, -1])
    a, u = _lru_coeffs(flip(xr_c), gate_w[1], gate_b[1], lam[1])
    hc_b = _lin_scan(a, u, h0)
    a, u = _lru_coeffs(flip(xr_l), gate_w[1], gate_b[1], lam[1])
    hl_b = _lin_scan(a, u, hc_b[:, -1])
    return hc_f + flip(hc_b), hl_f + flip(hl_b)


def _even_mixer(u_ctx, u_lat, w_in, w_out, qkv_conv, a_log, dt_bias, gdn_norm,
                lru_conv_w, lru_conv_b, lru_gate_w, lru_gate_b, lru_lambda, need_ctx):
    cuts = [3 * GDN_W, 4 * GDN_W, 4 * GDN_W + 2 * GDN_HEADS, 4 * GDN_W + 4 * GDN_HEADS,
            4 * GDN_W + 4 * GDN_HEADS + LRU_W]

    def local(u):
        qkv, z, a, b, xr, gate = jnp.split(u @ w_in, cuts, axis=-1)
        gdn = _gdn_prep(qkv, a, b, qkv_conv, a_log, dt_bias)
        xr = (_dwconv(xr, lru_conv_w) + lru_conv_b).astype(F32)
        return gdn, z, xr, gate

    gdn_c, z_c, xr_c, gate_c = local(u_ctx)
    gdn_l, z_l, xr_l, gate_l = local(u_lat)
    o_c, o_l = _gdn_bidir(gdn_c, gdn_l)
    h_c, h_l = _lru_bidir(xr_c, xr_l, lru_gate_w, lru_gate_b, lru_lambda)

    def merge(o, z, h, gate, dtype):
        y = jnp.concatenate([_gdn_out(o, z, gdn_norm), h * jax.nn.gelu(gate.astype(F32))], axis=-1)
        return (y.astype(w_out.dtype) @ w_out).astype(dtype)

    y_lat = merge(o_l, z_l, h_l, gate_l, u_lat.dtype)
    y_ctx = merge(o_c, z_c, h_c, gate_c, u_ctx.dtype) if need_ctx else None
    return y_ctx, y_lat


def _diff_attend(q, k, v, lam):
    s = jnp.einsum('bqhmd,bkhmd->bhmqk', q, k).astype(F32) * DIFF_D ** -0.5
    p = jax.nn.softmax(s, axis=-1)
    p = p[:, :, 0] - lam * p[:, :, 1]
    return jnp.einsum('bhqk,bkhv->bqhv', p.astype(v.dtype), v)


def _diff_mixer(u_ctx, u_lat, w_qkv, w_out, lam_vec, subln, cos, sin, layer_idx, need_ctx):
    lam_init = 0.8 - 0.6 * math.exp(-0.3 * layer_idx)
    lv = lam_vec.astype(F32)
    lam = jnp.exp(jnp.sum(lv[0] * lv[1])) - jnp.exp(jnp.sum(lv[2] * lv[3])) + lam_init

    def heads(u):
        bsz, L, _ = u.shape
        q, k, v = jnp.split(u @ w_qkv, 3, axis=-1)
        return (q.reshape(bsz, L, DIFF_HEADS, 2, DIFF_D), k.reshape(bsz, L, DIFF_HEADS, 2, DIFF_D),
                v.reshape(bsz, L, DIFF_HEADS, DIFF_DV))

    qc, kc, vc = heads(u_ctx)
    ql, kl, vl = heads(u_lat)
    cb, sb = cos[None, :, None, None, :], sin[None, :, None, None, :]
    ql, kl = _rope(ql, cb, sb), _rope(kl, cb, sb)
    k_all = jnp.concatenate([kc, kl], axis=1)
    v_all = jnp.concatenate([vc, vl], axis=1)
    bsz, L = ql.shape[:2]
    nb = L // Q_BLOCK
    qb = jnp.moveaxis(ql.reshape(bsz, nb, Q_BLOCK, DIFF_HEADS, 2, DIFF_D), 1, 0)
    o_l = lax.map(lambda qblk: _diff_attend(qblk, k_all, v_all, lam), qb)
    o_l = jnp.moveaxis(o_l, 0, 1).reshape(bsz, L, DIFF_HEADS, DIFF_DV)

    def out(o, dtype):
        y = _rmsnorm(o, subln) * (1.0 - lam_init)
        y = y.reshape(o.shape[0], o.shape[1], D_MODEL).astype(w_out.dtype)
        return (y @ w_out).astype(dtype)

    y_lat = out(o_l, u_lat.dtype)
    y_ctx = out(_diff_attend(qc, kc, vc, lam), u_ctx.dtype) if need_ctx else None
    return y_ctx, y_lat


def _mlp(u, w1, w2):
    return jnp.square(jax.nn.relu(u @ w1)) @ w2


def setup_inputs(seed: int = 0) -> dict:
    key = jax.random.key(seed)
    ks = jax.random.split(key, 24)
    nrm = lambda k, shape, s: jax.random.normal(k, shape, F32) * s
    x = nrm(ks[0], (BATCH, SEQ, D_MODEL), 1.0)
    c = nrm(ks[1], (BATCH, D_MODEL), 1.0)
    ctx = nrm(ks[2], (BATCH, CTX_LEN, D_MODEL), 1.0)
    c_ctx = nrm(ks[3], (D_MODEL,), 1.0)
    ada_w = nrm(ks[4], (DEPTH, D_MODEL, 6 * D_MODEL), 0.5 * D_MODEL ** -0.5)
    ada_b = nrm(ks[5], (DEPTH, 6 * D_MODEL), 0.02)
    ln_g = 1.0 + nrm(ks[6], (DEPTH, 2, D_MODEL), 0.02)
    ln_b = nrm(ks[7], (DEPTH, 2, D_MODEL), 0.02)
    mlp_w1 = nrm(ks[8], (DEPTH, D_MODEL, D_FF), D_MODEL ** -0.5)
    mlp_w2 = nrm(ks[9], (DEPTH, D_FF, D_MODEL), DEEPNORM_BETA * D_FF ** -0.5)
    mix_w_out = nrm(ks[10], (DEPTH, D_MODEL, D_MODEL), DEEPNORM_BETA * D_MODEL ** -0.5)
    ev_w_in = nrm(ks[11], (N_EVEN, D_MODEL, EV_IN), D_MODEL ** -0.5)
    ev_qkv_conv = nrm(ks[12], (N_EVEN, CONV_K, 3 * GDN_W), CONV_K ** -0.5)
    ev_a_log = jnp.log(jax.random.uniform(ks[13], (N_EVEN, 2, GDN_HEADS), F32, 1.0, 16.0))
    dt = jnp.exp(jax.random.uniform(ks[14], (N_EVEN, 2, GDN_HEADS), F32, math.log(1e-3), math.log(1e-1)))
    ev_dt_bias = dt + jnp.log(-jnp.expm1(-dt))
    ev_gdn_norm = 1.0 + nrm(ks[15], (N_EVEN, GDN_DV), 0.02)
    ev_lru_conv_w = nrm(ks[16], (N_EVEN, CONV_K, LRU_W), CONV_K ** -0.5)
    ev_lru_conv_b = nrm(ks[17], (N_EVEN, LRU_W), 0.02)
    ev_lru_gate_w = nrm(ks[18], (N_EVEN, 2, 2, LRU_BLOCKS, LRU_BW, LRU_BW), LRU_BW ** -0.5)
    ev_lru_gate_b = nrm(ks[19], (N_EVEN, 2, 2, LRU_W), 0.02)
    a_c = jax.random.uniform(ks[20], (N_EVEN, 2, LRU_W), F32, 0.9, 0.999)
    sig = a_c ** (1.0 / LRU_C)
    ev_lru_lambda = jnp.log(sig) - jnp.log1p(-sig)
    od_w_qkv = nrm(ks[21], (N_ODD, D_MODEL, 3 * D_MODEL), D_MODEL ** -0.5)
    od_lambda = nrm(ks[22], (N_ODD, 4, DIFF_D), 0.1)
    od_subln = 1.0 + nrm(ks[23], (N_ODD, DIFF_DV), 0.02)
    return {'x': x, 'c': c, 'ctx': ctx, 'c_ctx': c_ctx, 'ada_w': ada_w, 'ada_b': ada_b,
            'ln_g': ln_g, 'ln_b': ln_b, 'mlp_w1': mlp_w1, 'mlp_w2': mlp_w2, 'mix_w_out': mix_w_out,
            'ev_w_in': ev_w_in, 'ev_qkv_conv': ev_qkv_conv, 'ev_a_log': ev_a_log, 'ev_dt_bias': ev_dt_bias,
            'ev_gdn_norm': ev_gdn_norm, 'ev_lru_conv_w': ev_lru_conv_w, 'ev_lru_conv_b': ev_lru_conv_b,
            'ev_lru_gate_w': ev_lru_gate_w, 'ev_lru_gate_b': ev_lru_gate_b, 'ev_lru_lambda': ev_lru_lambda,
            'od_w_qkv': od_w_qkv, 'od_lambda': od_lambda, 'od_subln': od_subln}


def reference(x, c, ctx, c_ctx, ada_w, ada_b, ln_g, ln_b, mlp_w1, mlp_w2, mix_w_out,
              ev_w_in, ev_qkv_conv, ev_a_log, ev_dt_bias, ev_gdn_norm, ev_lru_conv_w, ev_lru_conv_b,
              ev_lru_gate_w, ev_lru_gate_b, ev_lru_lambda, od_w_qkv, od_lambda, od_subln):
    n_lat = x.shape[1]
    rows = n_lat // GRID_W
    row = jnp.repeat(jnp.arange(rows), GRID_W)
    col = jnp.tile(jnp.arange(GRID_W), rows)
    cos, sin = _axial_rope_tables(row, col)
    silu_c = jax.nn.silu(c)
    silu_cc = jax.nn.silu(c_ctx)
    h_lat, h_ctx = x, ctx
    for i in range(DEPTH):
        last = i == DEPTH - 1
        m_l = [m[:, None, :] for m in jnp.split(silu_c @ ada_w[i] + ada_b[i], 6, axis=-1)]
        m_c = jnp.split(silu_cc @ ada_w[i] + ada_b[i], 6, axis=-1)
        u_l = _modulate(h_lat, m_l[0], m_l[1])
        u_c = _modulate(h_ctx, m_c[0], m_c[1])
        j = i // 2
        if i % 2 == 0:
            y_c, y_l = _even_mixer(u_c, u_l, ev_w_in[j], mix_w_out[i], ev_qkv_conv[j], ev_a_log[j],
                                   ev_dt_bias[j], ev_gdn_norm[j], ev_lru_conv_w[j], ev_lru_conv_b[j],
                                   ev_lru_gate_w[j], ev_lru_gate_b[j], ev_lru_lambda[j], not last)
        else:
            y_c, y_l = _diff_mixer(u_c, u_l, od_w_qkv[j], mix_w_out[i], od_lambda[j], od_subln[j],
                                   cos, sin, i, not last)
        h_lat = _layernorm(DEEPNORM_ALPHA * h_lat + m_l[2] * y_l, ln_g[i, 0], ln_b[i, 0])
        f_l = _mlp(_modulate(h_lat, m_l[3], m_l[4]), mlp_w1[i], mlp_w2[i])
        h_lat = _layernorm(DEEPNORM_ALPHA * h_lat + m_l[5] * f_l, ln_g[i, 1], ln_b[i, 1])
        if not last:
            h_ctx = _layernorm(DEEPNORM_ALPHA * h_ctx + m_c[2] * y_c, ln_g[i, 0], ln_b[i, 0])
            f_c = _mlp(_modulate(h_ctx, m_c[3], m_c[4]), mlp_w1[i], mlp_w2[i])
            h_ctx = _layernorm(DEEPNORM_ALPHA * h_ctx + m_c[5] * f_c, ln_g[i, 1], ln_b[i, 1])
    return h_lat
```

```python
import functools
import math

import jax
import jax.numpy as jnp
from jax import lax
from jax.experimental import pallas as pl
from jax.experimental.pallas import tpu as pltpu

F32 = jnp.float32
BF16 = jnp.bfloat16

D_MODEL = 1024
D_FF = 4 * D_MODEL
CONV_K = 4
CONV_LEFT = 2
GDN_HEADS = 4
GDN_DK = 128
GDN_W = GDN_HEADS * GDN_DK
CHUNK = 64
LRU_W = D_MODEL - GDN_W
LRU_BLOCKS = 8
LRU_BW = LRU_W // LRU_BLOCKS
LRU_C = 8.0
DIFF_HEADS = 8
DIFF_D = 64
DIFF_DV = 128
GRID_W = 64
ROPE_THETA = 10000.0
NORM_EPS = 1e-6
LANES = 128
SUBLANES = 8
MOD_ROWS = 16
VMEM_LIMIT = 56 * 1024 * 1024


def _cparams(sem):
    return pltpu.CompilerParams(dimension_semantics=sem, vmem_limit_bytes=VMEM_LIMIT)


def _sigmoid(x):
    return 1.0 / (1.0 + jnp.exp(-x))


def _silu(x):
    return x * _sigmoid(x)


def _softplus(x):
    return jnp.maximum(x, 0.0) + jnp.log1p(jnp.exp(-jnp.abs(x)))


def _gelu_tanh(x):
    return 0.5 * x * (1.0 + jnp.tanh(math.sqrt(2.0 / math.pi) * (x + 0.044715 * (x * x * x))))


def _modulate(h, mod_ref, shift_row, scale_row):
    return h * (1.0 + mod_ref[scale_row:scale_row + 1, :]) + mod_ref[shift_row:shift_row + 1, :]


def _layernorm(x, g, b):
    mu = jnp.mean(x, axis=-1, keepdims=True)
    xc = x - mu
    var = jnp.mean(xc * xc, axis=-1, keepdims=True)
    return xc * lax.rsqrt(var + NORM_EPS) * g + b


def _dot(a, b):
    return jnp.dot(a, b, preferred_element_type=F32)


def _dot_nt(a, b):
    return lax.dot_general(a, b, (((1,), (1,)), ((), ())), preferred_element_type=F32)


def _dot_tn(a, b):
    return lax.dot_general(a, b, (((0,), (0,)), ((), ())), preferred_element_type=F32)


def _ada_kernel(c_ref, w_ref, b_ref, o_ref):
    s = _silu(c_ref[...])
    o_ref[...] = jnp.dot(s, w_ref[...], preferred_element_type=F32,
                         precision=lax.Precision.HIGHEST) + b_ref[...]


def _ada_call(cc, ada_w, ada_b):
    depth, d, n = ada_w.shape
    tn = D_MODEL
    return pl.pallas_call(
        _ada_kernel,
        grid=(depth, n // tn),
        in_specs=[pl.BlockSpec((MOD_ROWS, d), lambda i, j: (0, 0)),
                  pl.BlockSpec((None, d, tn), lambda i, j: (i, 0, j)),
                  pl.BlockSpec((None, 1, tn), lambda i, j: (i, 0, j))],
        out_specs=pl.BlockSpec((None, MOD_ROWS, tn), lambda i, j: (i, 0, j)),
        out_shape=jax.ShapeDtypeStruct((depth, MOD_ROWS, n), F32),
        compiler_params=_cparams(("parallel", "parallel")),
    )(cc, ada_w, ada_b.reshape(depth, 1, n))


def _mod_spec(layer, rows_per_mod):
    if rows_per_mod is None:
        return pl.BlockSpec((None, None, 6, D_MODEL), lambda t: (layer, 0, 0, 0))
    return pl.BlockSpec((None, None, 6, D_MODEL), lambda t: (layer, 1 + t // rows_per_mod, 0, 0))


def _const_spec(shape):
    nd = len(shape)
    return pl.BlockSpec(shape, lambda t: (0,) * nd)


def _token_tile(n_rows, per_batch):
    tm = min(512, n_rows if per_batch is None else per_batch)
    assert n_rows % tm == 0 and (per_batch is None or per_batch % tm == 0)
    return tm


def _inproj_even_kernel(h_ref, mod_ref, w_ref, wab_ref, qkv_ref, z_ref, xr_ref, gate_ref, ab_ref):
    u = _modulate(h_ref[...], mod_ref, 0, 1).astype(BF16)
    qkv_ref[...] = _dot(u, w_ref[:, 0:3 * GDN_W])
    z_ref[...] = _dot(u, w_ref[:, 3 * GDN_W:4 * GDN_W])
    xr_ref[...] = _dot(u, w_ref[:, 4 * GDN_W:4 * GDN_W + LRU_W])
    gate_ref[...] = _dot(u, w_ref[:, 4 * GDN_W + LRU_W:])
    ab_ref[...] = _dot(u, wab_ref[...])[:, 0:4 * GDN_HEADS]


def _inproj_even_call(h, mod, layer, per_batch, w_main, w_ab):
    n = h.shape[0]
    tm = _token_tile(n, per_batch)
    rpm = None if per_batch is None else per_batch // tm
    tok = lambda w: pl.BlockSpec((tm, w), lambda t: (t, 0))
    return pl.pallas_call(
        _inproj_even_kernel,
        grid=(n // tm,),
        in_specs=[tok(D_MODEL), _mod_spec(layer, rpm), _const_spec(w_main.shape), _const_spec(w_ab.shape)],
        out_specs=[tok(3 * GDN_W), tok(GDN_W), tok(LRU_W), tok(LRU_W), tok(4 * GDN_HEADS)],
        out_shape=[jax.ShapeDtypeStruct((n, 3 * GDN_W), F32), jax.ShapeDtypeStruct((n, GDN_W), F32),
                   jax.ShapeDtypeStruct((n, LRU_W), F32), jax.ShapeDtypeStruct((n, LRU_W), F32),
                   jax.ShapeDtypeStruct((n, 4 * GDN_HEADS), F32)],
        compiler_params=_cparams(("parallel",)),
    )(h, mod, w_main, w_ab)


def _dwconv(x_ref, w_ref, pad_ref, n):
    zeros = jnp.zeros((SUBLANES, LANES), F32)
    pad_ref[0:SUBLANES, :] = zeros
    pad_ref[SUBLANES + n:2 * SUBLANES + n, :] = zeros
    pad_ref[SUBLANES:SUBLANES + n, :] = x_ref[...]
    y = None
    for j in range(CONV_K):
        off = SUBLANES + j - CONV_LEFT
        term = pad_ref[off:off + n, :] * w_ref[j:j + 1, :]
        y = term if y is None else y + term
    return y


def _hp_parts(x):
    hi = x.astype(BF16).astype(F32)
    return hi, x - hi


def _mm_hp(x, y):
    xh, xl = _hp_parts(x)
    yh, yl = _hp_parts(y)
    x2 = xh + pltpu.roll(xl, CHUNK, 1)
    lhs = jnp.concatenate([x2, x2], axis=1).astype(BF16)
    rhs = jnp.concatenate([yh, yh, yl, yl], axis=0).astype(BF16)
    return _dot(lhs, rhs)


def _unit_tri_inverse(a, eye):
    p = eye - a
    q = _mm_hp(a, a)
    sq = 2
    while sq * 2 < CHUNK:
        r = _mm_hp(jnp.concatenate([p, q], axis=0), q)
        p = p + r[0:CHUNK]
        q = r[CHUNK:2 * CHUNK]
        sq *= 2
    return p + _mm_hp(p, q)


def _gdn_kernel(ql_ref, kl_ref, vl_ref, qc_ref, kc_ref, vc_ref, wq_ref, wk_ref, wv_ref,
                abl_ref, abc_ref, gp_ref, ol_ref, oc_ref,
                pad_s, q_s, k_s, v_s, ab_s, u_s, w_s, qg_s, kd_s, qk_s, gl_s, st_s, *, n_lat, n_ctx):
    n_tot = n_ctx + n_lat

    def prep(x_ref, w_ref, n, dst, base, kind):
        y = _silu(_dwconv(x_ref, w_ref, pad_s, n))
        if kind != "v":
            y = y * lax.rsqrt(jnp.sum(y * y, axis=-1, keepdims=True) + NORM_EPS)
        if kind == "q":
            y = y * (GDN_DK ** -0.5)
        dst[base:base + n, :] = y

    prep(qc_ref, wq_ref, n_ctx, q_s, 0, "q")
    prep(kc_ref, wk_ref, n_ctx, k_s, 0, "k")
    prep(vc_ref, wv_ref, n_ctx, v_s, 0, "v")
    prep(ql_ref, wq_ref, n_lat, q_s, n_ctx, "q")
    prep(kl_ref, wk_ref, n_lat, k_s, n_ctx, "k")
    prep(vl_ref, wv_ref, n_lat, v_s, n_ctx, "v")
    ab_s[0:n_ctx, :] = abc_ref[...]
    ab_s[n_ctx:n_tot, :] = abl_ref[...]

    row = lax.broadcasted_iota(jnp.int32, (CHUNK, LANES), 0)
    col = lax.broadcasted_iota(jnp.int32, (CHUNK, LANES), 1)
    left = col < CHUNK
    eye = jnp.where(row == col, 1.0, 0.0).astype(F32)
    incl = [jnp.logical_and(row >= col, left), jnp.logical_and(row <= col, left)]
    strict = [jnp.logical_and(row > col, left), jnp.logical_and(row < col, left)]
    ones_left = jnp.where(left, 1.0, 0.0).astype(F32)
    gp = gp_ref[...]

    def chunk_prep(c, carry):
        r0 = pl.multiple_of(c * CHUNK, CHUNK)
        rows = pl.ds(r0, CHUNK)
        q_c = q_s[rows, :]
        k_c = k_s[rows, :]
        v_c = v_s[rows, :]
        ab_c = ab_s[rows, :]
        kb = k_c.astype(BF16)
        k_pad = jnp.concatenate([kb, jnp.zeros_like(kb)], axis=0)
        qk_kk = _dot_nt(jnp.concatenate([q_c.astype(BF16), kb], axis=0), k_pad)
        qk_raw = qk_kk[0:CHUNK]
        kk = qk_kk[CHUNK:2 * CHUNK]
        for d in range(2):
            a_col = ab_c[:, d:d + 1]
            b_col = ab_c[:, 2 + d:3 + d]
            g_col = -jnp.exp(gp[:, d:d + 1]) * _softplus(a_col + gp[:, 2 + d:3 + d])
            beta = _sigmoid(b_col)
            g_b = jnp.broadcast_to(g_col, (CHUNK, LANES))
            m_t = incl[1 - d]
            gh, gl = _hp_parts(g_b)
            cum = jnp.where(incl[d], 1.0, 0.0).astype(F32)
            lhs1 = cum + pltpu.roll(ones_left, CHUNK, 1)
            lhs = jnp.concatenate([lhs1, lhs1], axis=1).astype(BF16)
            rhs = jnp.concatenate([gh, jnp.where(m_t, -gh, 0.0), gl, jnp.where(m_t, -gl, 0.0)],
                                  axis=0).astype(BF16)
            diff = _dot(lhs, rhs)
            gam = jnp.where(left, pltpu.roll(diff, CHUNK, 1), diff)
            tot = jnp.sum(g_b, axis=0, keepdims=True)
            decay = jnp.exp(jnp.where(incl[d], diff, -1e30))
            a_mat = jnp.where(strict[d], beta * kk * decay, 0.0)
            t_mat = _unit_tri_inverse(a_mat, eye)
            egam = jnp.exp(gam)
            rhs_uw = jnp.concatenate([v_c * beta, k_c * (beta * egam)], axis=1).astype(BF16)
            uw = _dot(t_mat[:, 0:CHUNK].astype(BF16), rhs_uw)
            u_s[d, rows, :] = uw[:, 0:LANES]
            w_s[d, rows, :] = uw[:, LANES:2 * LANES].astype(BF16)
            qg_s[d, rows, :] = (q_c * egam).astype(BF16)
            kd_s[d, rows, :] = (k_c * jnp.exp(tot - gam)).astype(BF16)
            qk_s[d, rows, :] = (qk_raw * decay).astype(BF16)
            gl_s[d, c] = jnp.broadcast_to(jnp.exp(tot), (SUBLANES, LANES))
        return carry

    lax.fori_loop(0, n_tot // CHUNK, chunk_prep, 0)

    st_s[...] = jnp.zeros_like(st_s)
    ol_ref[...] = jnp.zeros_like(ol_ref)
    oc_ref[...] = jnp.zeros_like(oc_ref)

    def seq_segment(chunk0, n_chunks, out_ref):
        def step(s, carry):
            for d in range(2):
                lc = s if d == 0 else n_chunks - 1 - s
                c = chunk0 + lc
                rows = pl.ds(pl.multiple_of(c * CHUNK, CHUNK), CHUNK)
                orow = pl.ds(pl.multiple_of(lc * CHUNK, CHUNK), CHUNK)
                st = st_s[d]
                stb = st.astype(BF16)
                r = _dot(jnp.concatenate([w_s[d, rows, :], qg_s[d, rows, :]], axis=0), stb)
                v_new = u_s[d, rows, :] - r[0:CHUNK]
                vb = v_new.astype(BF16)
                o = r[CHUNK:2 * CHUNK] + _dot(qk_s[d, rows, :][:, 0:CHUNK], vb)
                out_ref[orow, :] = out_ref[orow, :] + o
                st_s[d] = st * gl_s[d, c][0:1, :] + _dot_tn(kd_s[d, rows, :], vb)
            return carry
        lax.fori_loop(0, n_chunks, step, 0)

    seq_segment(0, n_ctx // CHUNK, oc_ref)
    seq_segment(n_ctx // CHUNK, n_lat // CHUNK, ol_ref)


def _gdn_call(qkv_l, qkv_c, ab_l, ab_c, conv_w, gparams, bsz, n_lat, n_ctx):
    n_tot = n_lat + n_ctx
    nh = GDN_HEADS
    lat = lambda off: pl.BlockSpec((n_lat, LANES), lambda b, h: (b, off + h))
    ctx = lambda off: pl.BlockSpec((n_ctx, LANES), lambda b, h: (b, off + h))
    cw = lambda off: pl.BlockSpec((CONV_K, LANES), lambda b, h: (0, off + h))
    kern = functools.partial(_gdn_kernel, n_lat=n_lat, n_ctx=n_ctx)
    return pl.pallas_call(
        kern,
        grid=(bsz, nh),
        in_specs=[lat(0), lat(nh), lat(2 * nh), ctx(0), ctx(nh), ctx(2 * nh), cw(0), cw(nh), cw(2 * nh),
                  pl.BlockSpec((None, n_lat, 4), lambda b, h: (h, b, 0)),
                  pl.BlockSpec((None, n_ctx, 4), lambda b, h: (h, b, 0)),
                  pl.BlockSpec((None, 1, 4), lambda b, h: (h, 0, 0))],
        out_specs=[pl.BlockSpec((n_lat, LANES), lambda b, h: (b, h)),
                   pl.BlockSpec((n_ctx, LANES), lambda b, h: (b, h))],
        out_shape=[jax.ShapeDtypeStruct((bsz * n_lat, GDN_W), F32),
                   jax.ShapeDtypeStruct((bsz * n_ctx, GDN_W), F32)],
        scratch_shapes=[
            pltpu.VMEM((n_lat + 2 * SUBLANES, LANES), F32),
            pltpu.VMEM((n_tot, LANES), F32),
            pltpu.VMEM((n_tot, LANES), F32),
            pltpu.VMEM((n_tot, LANES), F32),
            pltpu.VMEM((n_tot, 4), F32),
            pltpu.VMEM((2, n_tot, LANES), F32),
            pltpu.VMEM((2, n_tot, LANES), BF16),
            pltpu.VMEM((2, n_tot, LANES), BF16),
            pltpu.VMEM((2, n_tot, LANES), BF16),
            pltpu.VMEM((2, n_tot, LANES), BF16),
            pltpu.VMEM((2, n_tot // CHUNK, SUBLANES, LANES), F32),
            pltpu.VMEM((2, GDN_DK, LANES), F32),
        ],
        compiler_params=_cparams(("parallel", "arbitrary")),
    )(qkv_l, qkv_l, qkv_l, qkv_c, qkv_c, qkv_c, conv_w, conv_w, conv_w, ab_l, ab_c, gparams)


def _lru_kernel(xl_ref, xc_ref, cw_ref, cb_ref, gw_ref, gb_ref, lam_ref, hl_ref, hc_ref,
                pad_s, x_s, a_s, u_s, *, n_lat, n_ctx):
    n_tot = n_ctx + n_lat
    x_s[0:n_ctx, :] = _dwconv(xc_ref, cw_ref, pad_s, n_ctx) + cb_ref[...]
    x_s[n_ctx:n_tot, :] = _dwconv(xl_ref, cw_ref, pad_s, n_lat) + cb_ref[...]

    sp = [_softplus(-lam_ref[:, d * LANES:(d + 1) * LANES]) for d in range(2)]
    tile = min(256, n_ctx)

    def coeffs(t, carry):
        rows = pl.ds(pl.multiple_of(t * tile, tile), tile)
        x = x_s[rows, :]
        gates = _dot(x.astype(BF16), gw_ref[...]) + gb_ref[...]
        for d in range(2):
            r = _sigmoid(gates[:, (2 * d) * LANES:(2 * d + 1) * LANES])
            i = _sigmoid(gates[:, (2 * d + 1) * LANES:(2 * d + 2) * LANES])
            log_a = -LRU_C * r * sp[d]
            a_s[d, rows, :] = jnp.exp(log_a)
            th = jnp.tanh(log_a)
            u_s[d, rows, :] = jnp.sqrt(-2.0 * th / (1.0 - th)) * (i * x)
        return carry

    lax.fori_loop(0, n_tot // tile, coeffs, 0)

    sub = lax.broadcasted_iota(jnp.int32, (SUBLANES, LANES), 0)

    def group_scan(d, g0):
        rows = pl.ds(pl.multiple_of(g0, SUBLANES), SUBLANES)
        a = a_s[d, rows, :]
        u = u_s[d, rows, :]
        sh = 1
        while sh < SUBLANES:
            if d == 0:
                a_sh, u_sh, ok = pltpu.roll(a, sh, 0), pltpu.roll(u, sh, 0), sub >= sh
            else:
                a_sh, u_sh = pltpu.roll(a, SUBLANES - sh, 0), pltpu.roll(u, SUBLANES - sh, 0)
                ok = sub < SUBLANES - sh
            u = u + a * jnp.where(ok, u_sh, 0.0)
            a = a * jnp.where(ok, a_sh, 1.0)
            sh *= 2
        return rows, u, a

    def fwd(base, n, h0):
        def body(g, h):
            rows, u, a = group_scan(0, base + g * SUBLANES)
            hh = u + a * h
            u_s[0, rows, :] = hh
            return jnp.broadcast_to(hh[SUBLANES - 1:SUBLANES, :], (SUBLANES, LANES))
        return lax.fori_loop(0, n // SUBLANES, body, h0, unroll=4)

    def bwd(base, n, h0, out_ref):
        def body(g, h):
            lg = n // SUBLANES - 1 - g
            rows, u, a = group_scan(1, base + lg * SUBLANES)
            hh = u + a * h
            orow = pl.ds(pl.multiple_of(lg * SUBLANES, SUBLANES), SUBLANES)
            out_ref[orow, :] = u_s[0, rows, :] + hh
            return jnp.broadcast_to(hh[0:1, :], (SUBLANES, LANES))
        return lax.fori_loop(0, n // SUBLANES, body, h0, unroll=4)

    zero = jnp.zeros((SUBLANES, LANES), F32)
    h = fwd(0, n_ctx, zero)
    fwd(n_ctx, n_lat, h)
    h = bwd(0, n_ctx, zero, hc_ref)
    bwd(n_ctx, n_lat, h, hl_ref)


def _lru_call(xr_l, xr_c, conv_w, conv_b, gate_w, gate_b, lam, bsz, n_lat, n_ctx):
    n_tot = n_lat + n_ctx
    nt = LRU_W // LANES
    kern = functools.partial(_lru_kernel, n_lat=n_lat, n_ctx=n_ctx)
    return pl.pallas_call(
        kern,
        grid=(bsz, nt),
        in_specs=[pl.BlockSpec((n_lat, LANES), lambda b, j: (b, j)),
                  pl.BlockSpec((n_ctx, LANES), lambda b, j: (b, j)),
                  pl.BlockSpec((CONV_K, LANES), lambda b, j: (0, j)),
                  pl.BlockSpec((1, LANES), lambda b, j: (0, j)),
                  pl.BlockSpec((None, LANES, 4 * LANES), lambda b, j: (j, 0, 0)),
                  pl.BlockSpec((None, 1, 4 * LANES), lambda b, j: (j, 0, 0)),
                  pl.BlockSpec((None, 1, 2 * LANES), lambda b, j: (j, 0, 0))],
        out_specs=[pl.BlockSpec((n_lat, LANES), lambda b, j: (b, j)),
                   pl.BlockSpec((n_ctx, LANES), lambda b, j: (b, j))],
        out_shape=[jax.ShapeDtypeStruct((bsz * n_lat, LRU_W), F32),
                   jax.ShapeDtypeStruct((bsz * n_ctx, LRU_W), F32)],
        scratch_shapes=[pltpu.VMEM((n_lat + 2 * SUBLANES, LANES), F32),
                        pltpu.VMEM((n_tot, LANES), F32),
                        pltpu.VMEM((2, n_tot, LANES), F32),
                        pltpu.VMEM((2, n_tot, LANES), F32)],
        compiler_params=_cparams(("parallel", "arbitrary")),
    )(xr_l, xr_c, conv_w, conv_b, gate_w, gate_b, lam)


def _residual_ln(h, y, mod_ref, gate_row, lng_ref, lnb_ref, alpha):
    return _layernorm(alpha * h + mod_ref[gate_row:gate_row + 1, :] * y, lng_ref[...], lnb_ref[...])


def _out_even_kernel(o_ref, z_ref, hr_ref, gate_ref, h_ref, mod_ref, gn_ref, w_ref, lng_ref, lnb_ref,
                     out_ref, *, alpha):
    parts = []
    for hd in range(GDN_HEADS):
        sl = slice(hd * LANES, (hd + 1) * LANES)
        o = o_ref[:, sl]
        on = o * lax.rsqrt(jnp.mean(o * o, axis=-1, keepdims=True) + NORM_EPS) * gn_ref[...]
        parts.append(on * _silu(z_ref[:, sl]))
    y_gdn = jnp.concatenate(parts, axis=1).astype(BF16)
    y_lru = (hr_ref[...] * _gelu_tanh(gate_ref[...])).astype(BF16)
    y = _dot(y_gdn, w_ref[0:GDN_W, :]) + _dot(y_lru, w_ref[GDN_W:, :])
    out_ref[...] = _residual_ln(h_ref[...], y, mod_ref, 2, lng_ref, lnb_ref, alpha)


def _out_odd_kernel(y_ref, h_ref, mod_ref, w_ref, lng_ref, lnb_ref, out_ref, *, alpha):
    y = _dot(y_ref[...], w_ref[...])
    out_ref[...] = _residual_ln(h_ref[...], y, mod_ref, 2, lng_ref, lnb_ref, alpha)


def _out_even_call(o, z, hr, gate, h, mod, layer, per_batch, gn, w_out, ln_g, ln_b, alpha):
    n = h.shape[0]
    tm = _token_tile(n, per_batch)
    rpm = None if per_batch is None else per_batch // tm
    tok = lambda w: pl.BlockSpec((tm, w), lambda t: (t, 0))
    return pl.pallas_call(
        functools.partial(_out_even_kernel, alpha=alpha),
        grid=(n // tm,),
        in_specs=[tok(GDN_W), tok(GDN_W), tok(LRU_W), tok(LRU_W), tok(D_MODEL), _mod_spec(layer, rpm),
                  _const_spec(gn.shape), _const_spec(w_out.shape), _const_spec(ln_g.shape),
                  _const_spec(ln_b.shape)],
        out_specs=tok(D_MODEL),
        out_shape=jax.ShapeDtypeStruct((n, D_MODEL), F32),
        compiler_params=_cparams(("parallel",)),
    )(o, z, hr, gate, h, mod, gn, w_out, ln_g, ln_b)


def _out_odd_call(y, h, mod, layer, per_batch, w_out, ln_g, ln_b, alpha):
    n = h.shape[0]
    tm = _token_tile(n, per_batch)
    rpm = None if per_batch is None else per_batch // tm
    tok = lambda w: pl.BlockSpec((tm, w), lambda t: (t, 0))
    return pl.pallas_call(
        functools.partial(_out_odd_kernel, alpha=alpha),
        grid=(n // tm,),
        in_specs=[tok(D_MODEL), tok(D_MODEL), _mod_spec(layer, rpm), _const_spec(w_out.shape),
                  _const_spec(ln_g.shape), _const_spec(ln_b.shape)],
        out_specs=tok(D_MODEL),
        out_shape=jax.ShapeDtypeStruct((n, D_MODEL), F32),
        compiler_params=_cparams(("parallel",)),
    )(y, h, mod, w_out, ln_g, ln_b)


FF_TILE = 1024


def _mlp_kernel(h_ref, mod_ref, w1_ref, w2_ref, lng_ref, lnb_ref, out_ref, *, alpha):
    h = h_ref[...]
    u = _modulate(h, mod_ref, 3, 4).astype(BF16)
    acc = None
    for j in range(D_FF // FF_TILE):
        sl = slice(j * FF_TILE, (j + 1) * FF_TILE)
        a = jnp.maximum(_dot(u, w1_ref[:, sl]), 0.0)
        part = _dot((a * a).astype(BF16), w2_ref[sl, :])
        acc = part if acc is None else acc + part
    out_ref[...] = _residual_ln(h, acc, mod_ref, 5, lng_ref, lnb_ref, alpha)


def _mlp_call(h, mod, layer, per_batch, w1, w2, ln_g, ln_b, alpha):
    n = h.shape[0]
    tm = _token_tile(n, per_batch)
    rpm = None if per_batch is None else per_batch // tm
    tok = pl.BlockSpec((tm, D_MODEL), lambda t: (t, 0))
    return pl.pallas_call(
        functools.partial(_mlp_kernel, alpha=alpha),
        grid=(n // tm,),
        in_specs=[tok, _mod_spec(layer, rpm), _const_spec(w1.shape), _const_spec(w2.shape),
                  _const_spec(ln_g.shape), _const_spec(ln_b.shape)],
        out_specs=tok,
        out_shape=jax.ShapeDtypeStruct((n, D_MODEL), F32),
        compiler_params=_cparams(("parallel",)),
    )(h, mod, w1, w2, ln_g, ln_b)


def _qkv_rope_kernel(h_ref, mod_ref, wq_ref, wqr_ref, wk_ref, wkr_ref, wvt_ref, cos_ref, sin_ref,
                     q_ref, k_ref, vt_ref):
    u = _modulate(h_ref[...], mod_ref, 0, 1).astype(BF16)
    cos = jnp.concatenate([cos_ref[...]] * DIFF_HEADS, axis=1)
    sin = jnp.concatenate([sin_ref[...]] * DIFF_HEADS, axis=1)
    q_ref[...] = (_dot(u, wq_ref[...]) * cos + _dot(u, wqr_ref[...]) * sin).astype(BF16)
    k_ref[...] = (_dot(u, wk_ref[...]) * cos + _dot(u, wkr_ref[...]) * sin).astype(BF16)
    vt_ref[...] = _dot_nt(wvt_ref[...], u).astype(BF16)


def _qkv_plain_kernel(h_ref, mod_ref, wq_ref, wk_ref, wvt_ref, q_ref, k_ref, vt_ref):
    u = _modulate(h_ref[...], mod_ref, 0, 1).astype(BF16)
    q_ref[...] = _dot(u, wq_ref[...]).astype(BF16)
    k_ref[...] = _dot(u, wk_ref[...]).astype(BF16)
    vt_ref[...] = _dot_nt(wvt_ref[...], u).astype(BF16)


def _qkv_call(h, mod, layer, per_batch, weights, rope):
    n = h.shape[0]
    tm = _token_tile(n, per_batch)
    tok = pl.BlockSpec((tm, D_MODEL), lambda t: (t, 0))
    out_specs = [tok, tok, pl.BlockSpec((D_MODEL, tm), lambda t: (0, t))]
    out_shape = [jax.ShapeDtypeStruct((n, D_MODEL), BF16), jax.ShapeDtypeStruct((n, D_MODEL), BF16),
                 jax.ShapeDtypeStruct((D_MODEL, n), BF16)]
    wspec = _const_spec((D_MODEL, D_MODEL))
    if rope is not None:
        cos2, sin2 = rope
        rpm = per_batch // tm
        tab = pl.BlockSpec((tm, LANES), lambda t: (t % rpm, 0))
        wq, wqr, wk, wkr, wvt = weights
        return pl.pallas_call(
            _qkv_rope_kernel, grid=(n // tm,),
            in_specs=[tok, _mod_spec(layer, rpm), wspec, wspec, wspec, wspec, wspec, tab, tab],
            out_specs=out_specs, out_shape=out_shape,
            compiler_params=_cparams(("parallel",)),
        )(h, mod, wq, wqr, wk, wkr, wvt, cos2, sin2)
    wq, wk, wvt = weights
    return pl.pallas_call(
        _qkv_plain_kernel, grid=(n // tm,),
        in_specs=[tok, _mod_spec(layer, None), wspec, wspec, wspec],
        out_specs=out_specs, out_shape=out_shape,
        compiler_params=_cparams(("parallel",)),
    )(h, mod, wq, wk, wvt)


KEY_TILE = 256


def _attn_kernel(*refs, key_lens, lam_init):
    nseg = len(key_lens)
    q_ref = refs[0]
    k_refs = refs[1:1 + nseg]
    vt_refs = refs[1 + nseg:1 + 2 * nseg]
    lam_ref, subln_ref, y_ref, acc_s = refs[1 + 2 * nseg:]
    tq = q_ref.shape[0]

    q = q_ref[...]
    lane = lax.broadcasted_iota(jnp.int32, q.shape, 1)
    zero = jnp.zeros_like(q)
    qm = [jnp.where(lane < DIFF_D, q, zero), jnp.where(lane >= DIFF_D, q, zero)]
    acc_s[...] = jnp.zeros_like(acc_s)

    def tile_update(k_t, vt_t, stats):
        new = []
        for m in range(2):
            m_old, l_old = stats[m]
            s_t = _dot_nt(k_t, qm[m])
            m_new = jnp.maximum(m_old, jnp.max(s_t, axis=0, keepdims=True))
            alpha = jnp.exp(m_old - m_new)
            p = jnp.exp(s_t - m_new)
            l_new = alpha * l_old + jnp.sum(p, axis=0, keepdims=True)
            acc_s[m] = alpha * acc_s[m] + _dot(vt_t, p.astype(BF16))
            new.append((m_new, l_new))
        return tuple(new)

    init = (jnp.full((1, tq), -1e30, F32), jnp.zeros((1, tq), F32))
    stats = (init, init)
    for k_ref, vt_ref, n_keys in zip(k_refs, vt_refs, key_lens):
        tk = min(KEY_TILE, n_keys)

        def body(j, st, k_ref=k_ref, vt_ref=vt_ref, tk=tk):
            k0 = pl.multiple_of(j * tk, tk)
            return tile_update(k_ref[pl.ds(k0, tk), :], vt_ref[:, pl.ds(k0, tk)], st)
        stats = lax.fori_loop(0, n_keys // tk, body, stats)

    lv = lam_ref[...]
    lam = (jnp.exp(jnp.sum(lv[0:1] * lv[1:2], axis=1, keepdims=True))
           - jnp.exp(jnp.sum(lv[2:3] * lv[3:4], axis=1, keepdims=True)) + lam_init)
    o_t = acc_s[0] / stats[0][1] - lam * (acc_s[1] / stats[1][1])
    ms = jnp.mean(o_t * o_t, axis=0, keepdims=True)
    y_t = o_t * lax.rsqrt(ms + NORM_EPS) * subln_ref[...] * (1.0 - lam_init)
    y_ref[...] = y_t.T.astype(BF16)


def _attn_call(q, ks, vts, lam_vec, subln_col, bsz, n_q, key_lens, lam_init):
    tq = min(512, n_q)
    nq = n_q // tq
    nseg = len(key_lens)
    in_specs = [pl.BlockSpec((tq, LANES), lambda b, h, i: (b * nq + i, h))]
    in_specs += [pl.BlockSpec((n, LANES), lambda b, h, i: (b, h)) for n in key_lens]
    in_specs += [pl.BlockSpec((LANES, n), lambda b, h, i: (h, b)) for n in key_lens]
    in_specs += [pl.BlockSpec((4, DIFF_D), lambda b, h, i: (0, 0)),
                 pl.BlockSpec((DIFF_DV, 1), lambda b, h, i: (0, 0))]
    kern = functools.partial(_attn_kernel, key_lens=tuple(key_lens), lam_init=lam_init)
    return pl.pallas_call(
        kern,
        grid=(bsz, DIFF_HEADS, nq),
        in_specs=in_specs,
        out_specs=pl.BlockSpec((tq, LANES), lambda b, h, i: (b * nq + i, h)),
        out_shape=jax.ShapeDtypeStruct((bsz * n_q, D_MODEL), BF16),
        scratch_shapes=[pltpu.VMEM((2, DIFF_DV, tq), F32)],
        compiler_params=_cparams(("parallel", "parallel", "arbitrary")),
    )(q, *ks, *vts, lam_vec, subln_col)


def _rope_tables(n_lat):
    rows = n_lat // GRID_W
    row = jnp.repeat(jnp.arange(rows), GRID_W).astype(F32)
    col = jnp.tile(jnp.arange(GRID_W), rows).astype(F32)
    half = DIFF_D // 2
    inv = ROPE_THETA ** (-(jnp.arange(0, half, 2, dtype=F32) / half))
    ang_r = row[:, None] * inv
    ang_c = col[:, None] * inv
    ang = jnp.concatenate([ang_r, ang_r, ang_c, ang_c], axis=-1)
    cos, sin = jnp.cos(ang), jnp.sin(ang)
    return jnp.concatenate([cos, cos], axis=-1), jnp.concatenate([sin, sin], axis=-1)


def _rotate_columns(w):
    d, n = w.shape
    quarter = DIFF_D // 4
    g = w.reshape(d, n // DIFF_D, 4, quarter)
    rot = jnp.stack([-g[:, :, 1], g[:, :, 0], -g[:, :, 3], g[:, :, 2]], axis=2)
    return rot.reshape(d, n)


def _lru_gate_weights(gate_w, gate_b, lam):
    nt = LRU_W // LANES
    per = LANES // LRU_BW
    blocks = gate_w.reshape(2, 2, nt, per, LRU_BW, LRU_BW)
    eye = jnp.eye(per, dtype=gate_w.dtype)
    dense = jnp.einsum('dgtpab,pq->dgtpaqb', blocks, eye).reshape(2, 2, nt, LANES, LANES)
    w = jnp.transpose(dense, (2, 3, 0, 1, 4)).reshape(nt, LANES, 4 * LANES)
    b = jnp.transpose(gate_b.reshape(2, 2, nt, LANES), (2, 0, 1, 3)).reshape(nt, 1, 4 * LANES)
    lm = jnp.transpose(lam.reshape(2, nt, LANES), (1, 0, 2)).reshape(nt, 1, 2 * LANES)
    return w.astype(BF16), b, lm


def _per_head_gates(ab, n_rows):
    t = ab.reshape(n_rows, 2, 2, GDN_HEADS)
    return jnp.transpose(t, (3, 0, 1, 2)).reshape(GDN_HEADS, n_rows, 4)


def kernel(x, c, ctx, c_ctx, ada_w, ada_b, ln_g, ln_b, mlp_w1, mlp_w2, mix_w_out, ev_w_in, ev_qkv_conv,
           ev_a_log, ev_dt_bias, ev_gdn_norm, ev_lru_conv_w, ev_lru_conv_b, ev_lru_gate_w, ev_lru_gate_b,
           ev_lru_lambda, od_w_qkv, od_lambda, od_subln):
    bsz, n_lat, d = x.shape
    n_ctx = ctx.shape[1]
    depth = ada_w.shape[0]
    assert d == D_MODEL and bsz + 1 <= MOD_ROWS
    assert n_lat % CHUNK == 0 and n_ctx % CHUNK == 0 and n_lat % GRID_W == 0
    alpha = (2 * depth) ** 0.25

    h_lat = x.reshape(bsz * n_lat, d)
    h_ctx = ctx.reshape(bsz * n_ctx, d)
    cc = jnp.concatenate([c_ctx[None, :], c, jnp.zeros((MOD_ROWS - 1 - bsz, d), F32)], axis=0)
    mod = _ada_call(cc, ada_w, ada_b).reshape(depth, MOD_ROWS, 6, d)
    cos2, sin2 = _rope_tables(n_lat)

    for i in range(depth):
        last = i == depth - 1
        j = i // 2
        w_out = mix_w_out[i].astype(BF16)
        lng0, lnb0 = ln_g[i, 0][None, :], ln_b[i, 0][None, :]
        lng1, lnb1 = ln_g[i, 1][None, :], ln_b[i, 1][None, :]
        w1 = mlp_w1[i].astype(BF16)
        w2 = mlp_w2[i].astype(BF16)
        if i % 2 == 0:
            w_in = ev_w_in[j]
            c0, c1, c2 = 4 * GDN_W, 4 * GDN_W + 4 * GDN_HEADS, 4 * GDN_W + 4 * GDN_HEADS + 2 * LRU_W
            w_main = jnp.concatenate([w_in[:, :c0], w_in[:, c1:c2]], axis=1).astype(BF16)
            w_ab = jnp.pad(w_in[:, c0:c1], ((0, 0), (0, LANES - 4 * GDN_HEADS))).astype(BF16)
            qkv_l, z_l, xr_l, gate_l, ab_l = _inproj_even_call(h_lat, mod, i, n_lat, w_main, w_ab)
            qkv_c, z_c, xr_c, gate_c, ab_c = _inproj_even_call(h_ctx, mod, i, None, w_main, w_ab)
            gparams = jnp.stack([ev_a_log[j, 0], ev_a_log[j, 1], ev_dt_bias[j, 0], ev_dt_bias[j, 1]],
                                axis=-1).reshape(GDN_HEADS, 1, 4)
            o_l, o_c = _gdn_call(qkv_l, qkv_c, _per_head_gates(ab_l, bsz * n_lat),
                                 _per_head_gates(ab_c, bsz * n_ctx), ev_qkv_conv[j], gparams,
                                 bsz, n_lat, n_ctx)
            gw, gb, lm = _lru_gate_weights(ev_lru_gate_w[j], ev_lru_gate_b[j], ev_lru_lambda[j])
            hr_l, hr_c = _lru_call(xr_l, xr_c, ev_lru_conv_w[j], ev_lru_conv_b[j][None, :], gw, gb, lm,
                                   bsz, n_lat, n_ctx)
            gn = ev_gdn_norm[j][None, :]
            h_lat = _out_even_call(o_l, z_l, hr_l, gate_l, h_lat, mod, i, n_lat, gn, w_out, lng0, lnb0, alpha)
            if not last:
                h_ctx = _out_even_call(o_c, z_c, hr_c, gate_c, h_ctx, mod, i, None, gn, w_out, lng0, lnb0,
                                       alpha)
        else:
            lam_init = 0.8 - 0.6 * math.exp(-0.3 * i)
            wq = od_w_qkv[j][:, :D_MODEL] * (DIFF_D ** -0.5)
            wk = od_w_qkv[j][:, D_MODEL:2 * D_MODEL]
            wvt = od_w_qkv[j][:, 2 * D_MODEL:].T.astype(BF16)
            rope_w = (wq.astype(BF16), _rotate_columns(wq).astype(BF16), wk.astype(BF16),
                      _rotate_columns(wk).astype(BF16), wvt)
            q_l, k_l, vt_l = _qkv_call(h_lat, mod, i, n_lat, rope_w, (cos2, sin2))
            q_c, k_c, vt_c = _qkv_call(h_ctx, mod, i, None, (rope_w[0], rope_w[2], wvt), None)
            subln_col = od_subln[j][:, None]
            y_l = _attn_call(q_l, [k_c, k_l], [vt_c, vt_l], od_lambda[j], subln_col, bsz, n_lat,
                             [n_ctx, n_lat], lam_init)
            h_lat = _out_odd_call(y_l, h_lat, mod, i, n_lat, w_out, lng0, lnb0, alpha)
            if not last:
                y_c = _attn_call(q_c, [k_c], [vt_c], od_lambda[j], subln_col, bsz, n_ctx, [n_ctx], lam_init)
                h_ctx = _out_odd_call(y_c, h_ctx, mod, i, None, w_out, lng0, lnb0, alpha)
        h_lat = _mlp_call(h_lat, mod, i, n_lat, w1, w2, lng1, lnb1, alpha)
        if not last:
            h_ctx = _mlp_call(h_ctx, mod, i, None, w1, w2, lng1, lnb1, alpha)
    return h_lat.reshape(bsz, n_lat, d)
```

```python
import functools
import math

import jax
import jax.numpy as jnp
from jax import lax
from jax.experimental import pallas as pl
from jax.experimental.pallas import tpu as pltpu

F32 = jnp.float32
BF16 = jnp.bfloat16

D_MODEL = 1024
D_FF = 4 * D_MODEL
CONV_K = 4
CONV_LEFT = 2
GDN_HEADS = 4
GDN_DK = 128
GDN_W = GDN_HEADS * GDN_DK
CHUNK = 64
LRU_W = D_MODEL - GDN_W
LRU_BLOCKS = 8
LRU_BW = LRU_W // LRU_BLOCKS
LRU_C = 8.0
DIFF_HEADS = 8
DIFF_D = 64
DIFF_DV = 128
GRID_W = 64
ROPE_THETA = 10000.0
NORM_EPS = 1e-6
LANES = 128
SUBLANES = 8
MOD_ROWS = 16
VMEM_LIMIT = 56 * 1024 * 1024


def _cparams(sem):
    return pltpu.CompilerParams(dimension_semantics=sem, vmem_limit_bytes=VMEM_LIMIT)


def _sigmoid(x):
    return 1.0 / (1.0 + jnp.exp(-x))


def _silu(x):
    return x * _sigmoid(x)


def _softplus(x):
    return jnp.maximum(x, 0.0) + jnp.log1p(jnp.exp(-jnp.abs(x)))


def _gelu_tanh(x):
    return 0.5 * x * (1.0 + jnp.tanh(math.sqrt(2.0 / math.pi) * (x + 0.044715 * (x * x * x))))


def _modulate(h, mod_ref, shift_row, scale_row):
    return h * (1.0 + mod_ref[scale_row:scale_row + 1, :]) + mod_ref[shift_row:shift_row + 1, :]


def _layernorm(x, g, b):
    mu = jnp.mean(x, axis=-1, keepdims=True)
    xc = x - mu
    var = jnp.mean(xc * xc, axis=-1, keepdims=True)
    return xc * lax.rsqrt(var + NORM_EPS) * g + b


def _dot(a, b):
    return jnp.dot(a, b, preferred_element_type=F32)


def _dot_nt(a, b):
    return lax.dot_general(a, b, (((1,), (1,)), ((), ())), preferred_element_type=F32)


def _dot_tn(a, b):
    return lax.dot_general(a, b, (((0,), (0,)), ((), ())), preferred_element_type=F32)


def _ada_kernel(c_ref, w_ref, b_ref, o_ref):
    s = _silu(c_ref[...])
    o_ref[...] = jnp.dot(s, w_ref[...], preferred_element_type=F32,
                         precision=lax.Precision.HIGHEST) + b_ref[...]


def _ada_call(cc, ada_w, ada_b):
    depth, d, n = ada_w.shape
    tn = D_MODEL
    return pl.pallas_call(
        _ada_kernel,
        grid=(depth, n // tn),
        in_specs=[pl.BlockSpec((MOD_ROWS, d), lambda i, j: (0, 0)),
                  pl.BlockSpec((None, d, tn), lambda i, j: (i, 0, j)),
                  pl.BlockSpec((None, 1, tn), lambda i, j: (i, 0, j))],
        out_specs=pl.BlockSpec((None, MOD_ROWS, tn), lambda i, j: (i, 0, j)),
        out_shape=jax.ShapeDtypeStruct((depth, MOD_ROWS, n), F32),
        compiler_params=_cparams(("parallel", "parallel")),
    )(cc, ada_w, ada_b.reshape(depth, 1, n))


def _mod_spec(layer, rows_per_mod):
    if rows_per_mod is None:
        return pl.BlockSpec((None, None, 6, D_MODEL), lambda t: (layer, 0, 0, 0))
    return pl.BlockSpec((None, None, 6, D_MODEL), lambda t: (layer, 1 + t // rows_per_mod, 0, 0))


def _const_spec(shape):
    nd = len(shape)
    return pl.BlockSpec(shape, lambda t: (0,) * nd)


def _token_tile(n_rows, per_batch):
    tm = min(512, n_rows if per_batch is None else per_batch)
    assert n_rows % tm == 0 and (per_batch is None or per_batch % tm == 0)
    return tm


def _inproj_even_kernel(h_ref, mod_ref, w_ref, wab_ref, qkv_ref, z_ref, xr_ref, gate_ref, ab_ref):
    u = _modulate(h_ref[...], mod_ref, 0, 1).astype(BF16)
    qkv_ref[...] = _dot(u, w_ref[:, 0:3 * GDN_W])
    z_ref[...] = _dot(u, w_ref[:, 3 * GDN_W:4 * GDN_W])
    xr_ref[...] = _dot(u, w_ref[:, 4 * GDN_W:4 * GDN_W + LRU_W])
    gate_ref[...] = _dot(u, w_ref[:, 4 * GDN_W + LRU_W:])
    ab_ref[...] = _dot(u, wab_ref[...])[:, 0:4 * GDN_HEADS]


def _inproj_even_call(h, mod, layer, per_batch, w_main, w_ab):
    n = h.shape[0]
    tm = _token_tile(n, per_batch)
    rpm = None if per_batch is None else per_batch // tm
    tok = lambda w: pl.BlockSpec((tm, w), lambda t: (t, 0))
    return pl.pallas_call(
        _inproj_even_kernel,
        grid=(n // tm,),
        in_specs=[tok(D_MODEL), _mod_spec(layer, rpm), _const_spec(w_main.shape), _const_spec(w_ab.shape)],
        out_specs=[tok(3 * GDN_W), tok(GDN_W), tok(LRU_W), tok(LRU_W), tok(4 * GDN_HEADS)],
        out_shape=[jax.ShapeDtypeStruct((n, 3 * GDN_W), F32), jax.ShapeDtypeStruct((n, GDN_W), F32),
                   jax.ShapeDtypeStruct((n, LRU_W), F32), jax.ShapeDtypeStruct((n, LRU_W), F32),
                   jax.ShapeDtypeStruct((n, 4 * GDN_HEADS), F32)],
        compiler_params=_cparams(("parallel",)),
    )(h, mod, w_main, w_ab)


def _dwconv(x_ref, w_ref, pad_ref, n):
    zeros = jnp.zeros((SUBLANES, LANES), F32)
    pad_ref[0:SUBLANES, :] = zeros
    pad_ref[SUBLANES + n:2 * SUBLANES + n, :] = zeros
    pad_ref[SUBLANES:SUBLANES + n, :] = x_ref[...]
    y = None
    for j in range(CONV_K):
        off = SUBLANES + j - CONV_LEFT
        term = pad_ref[off:off + n, :] * w_ref[j:j + 1, :]
        y = term if y is None else y + term
    return y


def _hp_parts(x):
    hi = x.astype(BF16).astype(F32)
    return hi, x - hi


def _mm_hp(x, y):
    xh, xl = _hp_parts(x)
    yh, yl = _hp_parts(y)
    x2 = xh + pltpu.roll(xl, CHUNK, 1)
    lhs = jnp.concatenate([x2, x2], axis=1).astype(BF16)
    rhs = jnp.concatenate([yh, yh, yl, yl], axis=0).astype(BF16)
    return _dot(lhs, rhs)


def _unit_tri_inverse(a, eye):
    p = eye - a
    q = _mm_hp(a, a)
    sq = 2
    while sq * 2 < CHUNK:
        r = _mm_hp(jnp.concatenate([p, q], axis=0), q)
        p = p + r[0:CHUNK]
        q = r[CHUNK:2 * CHUNK]
        sq *= 2
    return p + _mm_hp(p, q)


def _gdn_kernel(ql_ref, kl_ref, vl_ref, qc_ref, kc_ref, vc_ref, wq_ref, wk_ref, wv_ref,
                abl_ref, abc_ref, gp_ref, ol_ref, oc_ref,
                pad_s, q_s, k_s, v_s, ab_s, mt_s, nt_s, qe_s, gl_s, st_s, *, n_lat, n_ctx):
    n_tot = n_ctx + n_lat

    def prep(x_ref, w_ref, n, dst, base, kind):
        y = _silu(_dwconv(x_ref, w_ref, pad_s, n))
        if kind != "v":
            y = y * lax.rsqrt(jnp.sum(y * y, axis=-1, keepdims=True) + NORM_EPS)
        if kind == "q":
            y = y * (GDN_DK ** -0.5)
        dst[base:base + n, :] = y

    prep(qc_ref, wq_ref, n_ctx, q_s, 0, "q")
    prep(kc_ref, wk_ref, n_ctx, k_s, 0, "k")
    prep(vc_ref, wv_ref, n_ctx, v_s, 0, "v")
    prep(ql_ref, wq_ref, n_lat, q_s, n_ctx, "q")
    prep(kl_ref, wk_ref, n_lat, k_s, n_ctx, "k")
    prep(vl_ref, wv_ref, n_lat, v_s, n_ctx, "v")
    ab_s[0:n_ctx, :] = abc_ref[...]
    ab_s[n_ctx:n_tot, :] = abl_ref[...]

    row = lax.broadcasted_iota(jnp.int32, (CHUNK, LANES), 0)
    col = lax.broadcasted_iota(jnp.int32, (CHUNK, LANES), 1)
    left = col < CHUNK
    eye = jnp.where(row == col, 1.0, 0.0).astype(F32)
    incl = [jnp.logical_and(row >= col, left), jnp.logical_and(row <= col, left)]
    strict = [jnp.logical_and(row > col, left), jnp.logical_and(row < col, left)]
    ones_left = jnp.where(left, 1.0, 0.0).astype(F32)
    gp = gp_ref[...]

    def chunk_group(chunk0, out_ref, group, gi):
        chains = []
        for g in range(group):
            lc = gi * group + g
            c = chunk0 + lc
            rows = pl.ds(pl.multiple_of(c * CHUNK, CHUNK), CHUNK)
            orow = pl.ds(pl.multiple_of(lc * CHUNK, CHUNK), CHUNK)
            q_c = q_s[rows, :]
            k_c = k_s[rows, :]
            v_c = v_s[rows, :]
            ab_c = ab_s[rows, :]
            kb = k_c.astype(BF16)
            k_pad = jnp.concatenate([kb, jnp.zeros_like(kb)], axis=0)
            qk_kk = _dot_nt(jnp.concatenate([q_c.astype(BF16), kb], axis=0), k_pad)
            for d in range(2):
                chains.append(dict(c=c, rows=rows, orow=orow, d=d, q=q_c, k=k_c, v=v_c, ab=ab_c,
                                   qk_raw=qk_kk[0:CHUNK], kk=qk_kk[CHUNK:2 * CHUNK]))
        for ch in chains:
            d = ch["d"]
            a_col = ch["ab"][:, d:d + 1]
            b_col = ch["ab"][:, 2 + d:3 + d]
            g_col = -jnp.exp(gp[:, d:d + 1]) * _softplus(a_col + gp[:, 2 + d:3 + d])
            ch["beta"] = _sigmoid(b_col)
            g_b = jnp.broadcast_to(g_col, (CHUNK, LANES))
            m_t = incl[1 - d]
            gh, gl = _hp_parts(g_b)
            cum = jnp.where(incl[d], 1.0, 0.0).astype(F32)
            lhs1 = cum + pltpu.roll(ones_left, CHUNK, 1)
            lhs = jnp.concatenate([lhs1, lhs1], axis=1).astype(BF16)
            rhs = jnp.concatenate([gh, jnp.where(m_t, -gh, 0.0), gl, jnp.where(m_t, -gl, 0.0)],
                                  axis=0).astype(BF16)
            ch["diff"] = _dot(lhs, rhs)
            ch["tot"] = jnp.sum(g_b, axis=0, keepdims=True)
        for ch in chains:
            d = ch["d"]
            ch["decay"] = jnp.exp(jnp.where(incl[d], ch["diff"], -1e30))
            a_mat = jnp.where(strict[d], ch["beta"] * ch["kk"] * ch["decay"], 0.0)
            ch["p"] = eye - a_mat
            ch["a"] = a_mat
        for ch in chains:
            ch["q2"] = _mm_hp(ch["a"], ch["a"])
        sq = 2
        while sq * 2 < CHUNK:
            for ch in chains:
                r = _mm_hp(jnp.concatenate([ch["p"], ch["q2"]], axis=0), ch["q2"])
                ch["p"] = ch["p"] + r[0:CHUNK]
                ch["q2"] = r[CHUNK:2 * CHUNK]
            sq *= 2
        for ch in chains:
            ch["t"] = ch["p"] + _mm_hp(ch["p"], ch["q2"])
        for ch in chains:
            gam = jnp.where(left, pltpu.roll(ch["diff"], CHUNK, 1), ch["diff"])
            egam = jnp.exp(gam)
            beta = ch["beta"]
            rhs_uw = jnp.concatenate([ch["k"] * (beta * egam), ch["v"] * beta], axis=1).astype(BF16)
            ch["wu"] = _dot(ch["t"][:, 0:CHUNK].astype(BF16), rhs_uw).astype(BF16)
            ch["qg"] = ch["q"] * egam
            ch["kd"] = (ch["k"] * jnp.exp(ch["tot"] - gam)).astype(BF16)
        for ch in chains:
            d, c = ch["d"], ch["c"]
            qk_m = (ch["qk_raw"] * ch["decay"])[:, 0:CHUNK].astype(BF16)
            x = _dot(qk_m, ch["wu"])
            mn = _dot_tn(ch["wu"], ch["kd"])
            qe_s[d, ch["rows"], :] = (ch["qg"] - x[:, 0:LANES]).astype(BF16)
            out_ref[ch["orow"], :] = out_ref[ch["orow"], :] + x[:, LANES:2 * LANES]
            mt_s[d, c] = mn[0:LANES].astype(BF16)
            nt_s[d, c] = mn[LANES:2 * LANES]
            gl_s[d, c] = jnp.broadcast_to(jnp.exp(ch["tot"]), (SUBLANES, LANES))

    def prep_segment(chunk0, n_chunks, out_ref):
        group = 4 if n_chunks % 4 == 0 else (2 if n_chunks % 2 == 0 else 1)

        def prep(gi, carry):
            chunk_group(chunk0, out_ref, group, gi)
            return carry
        lax.fori_loop(0, n_chunks // group, prep, 0)

    def rec_segment(chunk0, n_chunks, out_ref):
        def step(s, carry):
            for d in range(2):
                lc = s if d == 0 else n_chunks - 1 - s
                c = chunk0 + lc
                rows = pl.ds(pl.multiple_of(c * CHUNK, CHUNK), CHUNK)
                orow = pl.ds(pl.multiple_of(lc * CHUNK, CHUNK), CHUNK)
                st = st_s[d]
                stb = st.astype(BF16)
                r = _dot(stb, mt_s[d, c])
                out_ref[orow, :] = out_ref[orow, :] + _dot_nt(qe_s[d, rows, :], stb)
                st_s[d] = st * gl_s[d, c][0:1, :] + nt_s[d, c] - r
            return carry
        lax.fori_loop(0, n_chunks, step, 0)

    st_s[...] = jnp.zeros_like(st_s)
    ol_ref[...] = jnp.zeros_like(ol_ref)
    oc_ref[...] = jnp.zeros_like(oc_ref)
    prep_segment(0, n_ctx // CHUNK, oc_ref)
    prep_segment(n_ctx // CHUNK, n_lat // CHUNK, ol_ref)
    rec_segment(0, n_ctx // CHUNK, oc_ref)
    rec_segment(n_ctx // CHUNK, n_lat // CHUNK, ol_ref)


def _gdn_call(qkv_l, qkv_c, ab_l, ab_c, conv_w, gparams, bsz, n_lat, n_ctx):
    n_tot = n_lat + n_ctx
    nh = GDN_HEADS
    lat = lambda off: pl.BlockSpec((n_lat, LANES), lambda b, h: (b, off + h))
    ctx = lambda off: pl.BlockSpec((n_ctx, LANES), lambda b, h: (b, off + h))
    cw = lambda off: pl.BlockSpec((CONV_K, LANES), lambda b, h: (0, off + h))
    kern = functools.partial(_gdn_kernel, n_lat=n_lat, n_ctx=n_ctx)
    return pl.pallas_call(
        kern,
        grid=(bsz, nh),
        in_specs=[lat(0), lat(nh), lat(2 * nh), ctx(0), ctx(nh), ctx(2 * nh), cw(0), cw(nh), cw(2 * nh),
                  pl.BlockSpec((None, n_lat, 4), lambda b, h: (h, b, 0)),
                  pl.BlockSpec((None, n_ctx, 4), lambda b, h: (h, b, 0)),
                  pl.BlockSpec((None, 1, 4), lambda b, h: (h, 0, 0))],
        out_specs=[pl.BlockSpec((n_lat, LANES), lambda b, h: (b, h)),
                   pl.BlockSpec((n_ctx, LANES), lambda b, h: (b, h))],
        out_shape=[jax.ShapeDtypeStruct((bsz * n_lat, GDN_W), F32),
                   jax.ShapeDtypeStruct((bsz * n_ctx, GDN_W), F32)],
        scratch_shapes=[
            pltpu.VMEM((n_lat + 2 * SUBLANES, LANES), F32),
            pltpu.VMEM((n_tot, LANES), F32),
            pltpu.VMEM((n_tot, LANES), F32),
            pltpu.VMEM((n_tot, LANES), F32),
            pltpu.VMEM((n_tot, 4), F32),
            pltpu.VMEM((2, n_tot // CHUNK, GDN_DK, LANES), BF16),
            pltpu.VMEM((2, n_tot // CHUNK, LANES, GDN_DK), F32),
            pltpu.VMEM((2, n_tot, LANES), BF16),
            pltpu.VMEM((2, n_tot // CHUNK, SUBLANES, LANES), F32),
            pltpu.VMEM((2, GDN_DK, LANES), F32),
        ],
        compiler_params=_cparams(("parallel", "arbitrary")),
    )(qkv_l, qkv_l, qkv_l, qkv_c, qkv_c, qkv_c, conv_w, conv_w, conv_w, ab_l, ab_c, gparams)


def _lru_kernel(xl_ref, xc_ref, cw_ref, cb_ref, gw_ref, gb_ref, lam_ref, hl_ref, hc_ref,
                pad_s, x_s, a_s, u_s, *, n_lat, n_ctx):
    n_tot = n_ctx + n_lat
    x_s[0:n_ctx, :] = _dwconv(xc_ref, cw_ref, pad_s, n_ctx) + cb_ref[...]
    x_s[n_ctx:n_tot, :] = _dwconv(xl_ref, cw_ref, pad_s, n_lat) + cb_ref[...]

    sp = [_softplus(-lam_ref[:, d * LANES:(d + 1) * LANES]) for d in range(2)]
    tile = min(256, n_ctx)

    def coeffs(t, carry):
        rows = pl.ds(pl.multiple_of(t * tile, tile), tile)
        x = x_s[rows, :]
        gates = _dot(x.astype(BF16), gw_ref[...]) + gb_ref[...]
        for d in range(2):
            r = _sigmoid(gates[:, (2 * d) * LANES:(2 * d + 1) * LANES])
            i = _sigmoid(gates[:, (2 * d + 1) * LANES:(2 * d + 2) * LANES])
            log_a = -LRU_C * r * sp[d]
            a_s[d, rows, :] = jnp.exp(log_a)
            th = jnp.tanh(log_a)
            u_s[d, rows, :] = jnp.sqrt(-2.0 * th / (1.0 - th)) * (i * x)
        return carry

    lax.fori_loop(0, n_tot // tile, coeffs, 0)

    sub = lax.broadcasted_iota(jnp.int32, (SUBLANES, LANES), 0)

    def group_scan(d, g0):
        rows = pl.ds(pl.multiple_of(g0, SUBLANES), SUBLANES)
        a = a_s[d, rows, :]
        u = u_s[d, rows, :]
        sh = 1
        while sh < SUBLANES:
            if d == 0:
                a_sh, u_sh, ok = pltpu.roll(a, sh, 0), pltpu.roll(u, sh, 0), sub >= sh
            else:
                a_sh, u_sh = pltpu.roll(a, SUBLANES - sh, 0), pltpu.roll(u, SUBLANES - sh, 0)
                ok = sub < SUBLANES - sh
            u = u + a * jnp.where(ok, u_sh, 0.0)
            a = a * jnp.where(ok, a_sh, 1.0)
            sh *= 2
        return rows, u, a

    def fwd(base, n, h0):
        def body(g, h):
            rows, u, a = group_scan(0, base + g * SUBLANES)
            hh = u + a * h
            u_s[0, rows, :] = hh
            return jnp.broadcast_to(hh[SUBLANES - 1:SUBLANES, :], (SUBLANES, LANES))
        return lax.fori_loop(0, n // SUBLANES, body, h0, unroll=4)

    def bwd(base, n, h0, out_ref):
        def body(g, h):
            lg = n // SUBLANES - 1 - g
            rows, u, a = group_scan(1, base + lg * SUBLANES)
            hh = u + a * h
            orow = pl.ds(pl.multiple_of(lg * SUBLANES, SUBLANES), SUBLANES)
            out_ref[orow, :] = u_s[0, rows, :] + hh
            return jnp.broadcast_to(hh[0:1, :], (SUBLANES, LANES))
        return lax.fori_loop(0, n // SUBLANES, body, h0, unroll=4)

    zero = jnp.zeros((SUBLANES, LANES), F32)
    h = fwd(0, n_ctx, zero)
    fwd(n_ctx, n_lat, h)
    h = bwd(0, n_ctx, zero, hc_ref)
    bwd(n_ctx, n_lat, h, hl_ref)


def _lru_call(xr_l, xr_c, conv_w, conv_b, gate_w, gate_b, lam, bsz, n_lat, n_ctx):
    n_tot = n_lat + n_ctx
    nt = LRU_W // LANES
    kern = functools.partial(_lru_kernel, n_lat=n_lat, n_ctx=n_ctx)
    return pl.pallas_call(
        kern,
        grid=(bsz, nt),
        in_specs=[pl.BlockSpec((n_lat, LANES), lambda b, j: (b, j)),
                  pl.BlockSpec((n_ctx, LANES), lambda b, j: (b, j)),
                  pl.BlockSpec((CONV_K, LANES), lambda b, j: (0, j)),
                  pl.BlockSpec((1, LANES), lambda b, j: (0, j)),
                  pl.BlockSpec((None, LANES, 4 * LANES), lambda b, j: (j, 0, 0)),
                  pl.BlockSpec((None, 1, 4 * LANES), lambda b, j: (j, 0, 0)),
                  pl.BlockSpec((None, 1, 2 * LANES), lambda b, j: (j, 0, 0))],
        out_specs=[pl.BlockSpec((n_lat, LANES), lambda b, j: (b, j)),
                   pl.BlockSpec((n_ctx, LANES), lambda b, j: (b, j))],
        out_shape=[jax.ShapeDtypeStruct((bsz * n_lat, LRU_W), F32),
                   jax.ShapeDtypeStruct((bsz * n_ctx, LRU_W), F32)],
        scratch_shapes=[pltpu.VMEM((n_lat + 2 * SUBLANES, LANES), F32),
                        pltpu.VMEM((n_tot, LANES), F32),
                        pltpu.VMEM((2, n_tot, LANES), F32),
                        pltpu.VMEM((2, n_tot, LANES), F32)],
        compiler_params=_cparams(("parallel", "arbitrary")),
    )(xr_l, xr_c, conv_w, conv_b, gate_w, gate_b, lam)


def _residual_ln(h, y, mod_ref, gate_row, lng_ref, lnb_ref, alpha):
    return _layernorm(alpha * h + mod_ref[gate_row:gate_row + 1, :] * y, lng_ref[...], lnb_ref[...])


def _out_even_kernel(o_ref, z_ref, hr_ref, gate_ref, h_ref, mod_ref, gn_ref, w_ref, lng_ref, lnb_ref,
                     out_ref, *, alpha):
    parts = []
    for hd in range(GDN_HEADS):
        sl = slice(hd * LANES, (hd + 1) * LANES)
        o = o_ref[:, sl]
        on = o * lax.rsqrt(jnp.mean(o * o, axis=-1, keepdims=True) + NORM_EPS) * gn_ref[...]
        parts.append(on * _silu(z_ref[:, sl]))
    y_gdn = jnp.concatenate(parts, axis=1).astype(BF16)
    y_lru = (hr_ref[...] * _gelu_tanh(gate_ref[...])).astype(BF16)
    y = _dot(y_gdn, w_ref[0:GDN_W, :]) + _dot(y_lru, w_ref[GDN_W:, :])
    out_ref[...] = _residual_ln(h_ref[...], y, mod_ref, 2, lng_ref, lnb_ref, alpha)


def _out_odd_kernel(y_ref, h_ref, mod_ref, w_ref, lng_ref, lnb_ref, out_ref, *, alpha):
    y = _dot(y_ref[...], w_ref[...])
    out_ref[...] = _residual_ln(h_ref[...], y, mod_ref, 2, lng_ref, lnb_ref, alpha)


def _out_even_call(o, z, hr, gate, h, mod, layer, per_batch, gn, w_out, ln_g, ln_b, alpha):
    n = h.shape[0]
    tm = _token_tile(n, per_batch)
    rpm = None if per_batch is None else per_batch // tm
    tok = lambda w: pl.BlockSpec((tm, w), lambda t: (t, 0))
    return pl.pallas_call(
        functools.partial(_out_even_kernel, alpha=alpha),
        grid=(n // tm,),
        in_specs=[tok(GDN_W), tok(GDN_W), tok(LRU_W), tok(LRU_W), tok(D_MODEL), _mod_spec(layer, rpm),
                  _const_spec(gn.shape), _const_spec(w_out.shape), _const_spec(ln_g.shape),
                  _const_spec(ln_b.shape)],
        out_specs=tok(D_MODEL),
        out_shape=jax.ShapeDtypeStruct((n, D_MODEL), F32),
        compiler_params=_cparams(("parallel",)),
    )(o, z, hr, gate, h, mod, gn, w_out, ln_g, ln_b)


def _out_odd_call(y, h, mod, layer, per_batch, w_out, ln_g, ln_b, alpha):
    n = h.shape[0]
    tm = _token_tile(n, per_batch)
    rpm = None if per_batch is None else per_batch // tm
    tok = lambda w: pl.BlockSpec((tm, w), lambda t: (t, 0))
    return pl.pallas_call(
        functools.partial(_out_odd_kernel, alpha=alpha),
        grid=(n // tm,),
        in_specs=[tok(D_MODEL), tok(D_MODEL), _mod_spec(layer, rpm), _const_spec(w_out.shape),
                  _const_spec(ln_g.shape), _const_spec(ln_b.shape)],
        out_specs=tok(D_MODEL),
        out_shape=jax.ShapeDtypeStruct((n, D_MODEL), F32),
        compiler_params=_cparams(("parallel",)),
    )(y, h, mod, w_out, ln_g, ln_b)


FF_TILE = 1024


def _mlp_kernel(h_ref, mod_ref, w1_ref, w2_ref, lng_ref, lnb_ref, out_ref, *, alpha):
    h = h_ref[...]
    u = _modulate(h, mod_ref, 3, 4).astype(BF16)
    acc = None
    for j in range(D_FF // FF_TILE):
        sl = slice(j * FF_TILE, (j + 1) * FF_TILE)
        a = jnp.maximum(_dot(u, w1_ref[:, sl]), 0.0)
        part = _dot((a * a).astype(BF16), w2_ref[sl, :])
        acc = part if acc is None else acc + part
    out_ref[...] = _residual_ln(h, acc, mod_ref, 5, lng_ref, lnb_ref, alpha)


def _mlp_call(h, mod, layer, per_batch, w1, w2, ln_g, ln_b, alpha):
    n = h.shape[0]
    tm = _token_tile(n, per_batch)
    rpm = None if per_batch is None else per_batch // tm
    tok = pl.BlockSpec((tm, D_MODEL), lambda t: (t, 0))
    return pl.pallas_call(
        functools.partial(_mlp_kernel, alpha=alpha),
        grid=(n // tm,),
        in_specs=[tok, _mod_spec(layer, rpm), _const_spec(w1.shape), _const_spec(w2.shape),
                  _const_spec(ln_g.shape), _const_spec(ln_b.shape)],
        out_specs=tok,
        out_shape=jax.ShapeDtypeStruct((n, D_MODEL), F32),
        compiler_params=_cparams(("parallel",)),
    )(h, mod, w1, w2, ln_g, ln_b)


def _qkv_rope_kernel(h_ref, mod_ref, wq_ref, wqr_ref, wk_ref, wkr_ref, wvt_ref, cos_ref, sin_ref,
                     q_ref, k_ref, vt_ref):
    u = _modulate(h_ref[...], mod_ref, 0, 1).astype(BF16)
    cos = jnp.concatenate([cos_ref[...]] * DIFF_HEADS, axis=1)
    sin = jnp.concatenate([sin_ref[...]] * DIFF_HEADS, axis=1)
    q_ref[...] = (_dot(u, wq_ref[...]) * cos + _dot(u, wqr_ref[...]) * sin).astype(BF16)
    k_ref[...] = (_dot(u, wk_ref[...]) * cos + _dot(u, wkr_ref[...]) * sin).astype(BF16)
    vt_ref[...] = _dot_nt(wvt_ref[...], u).astype(BF16)


def _qkv_plain_kernel(h_ref, mod_ref, wq_ref, wk_ref, wvt_ref, q_ref, k_ref, vt_ref):
    u = _modulate(h_ref[...], mod_ref, 0, 1).astype(BF16)
    q_ref[...] = _dot(u, wq_ref[...]).astype(BF16)
    k_ref[...] = _dot(u, wk_ref[...]).astype(BF16)
    vt_ref[...] = _dot_nt(wvt_ref[...], u).astype(BF16)


def _qkv_call(h, mod, layer, per_batch, weights, rope):
    n = h.shape[0]
    tm = _token_tile(n, per_batch)
    tok = pl.BlockSpec((tm, D_MODEL), lambda t: (t, 0))
    out_specs = [tok, tok, pl.BlockSpec((D_MODEL, tm), lambda t: (0, t))]
    out_shape = [jax.ShapeDtypeStruct((n, D_MODEL), BF16), jax.ShapeDtypeStruct((n, D_MODEL), BF16),
                 jax.ShapeDtypeStruct((D_MODEL, n), BF16)]
    wspec = _const_spec((D_MODEL, D_MODEL))
    if rope is not None:
        cos2, sin2 = rope
        rpm = per_batch // tm
        tab = pl.BlockSpec((tm, LANES), lambda t: (t % rpm, 0))
        wq, wqr, wk, wkr, wvt = weights
        return pl.pallas_call(
            _qkv_rope_kernel, grid=(n // tm,),
            in_specs=[tok, _mod_spec(layer, rpm), wspec, wspec, wspec, wspec, wspec, tab, tab],
            out_specs=out_specs, out_shape=out_shape,
            compiler_params=_cparams(("parallel",)),
        )(h, mod, wq, wqr, wk, wkr, wvt, cos2, sin2)
    wq, wk, wvt = weights
    return pl.pallas_call(
        _qkv_plain_kernel, grid=(n // tm,),
        in_specs=[tok, _mod_spec(layer, None), wspec, wspec, wspec],
        out_specs=out_specs, out_shape=out_shape,
        compiler_params=_cparams(("parallel",)),
    )(h, mod, wq, wk, wvt)


KEY_TILE = 256


def _attn_kernel(*refs, key_lens, lam_init):
    nseg = len(key_lens)
    q_ref = refs[0]
    k_refs = refs[1:1 + nseg]
    vt_refs = refs[1 + nseg:1 + 2 * nseg]
    lam_ref, subln_ref, y_ref, acc_s, sc_s = refs[1 + 2 * nseg:]
    tq = q_ref.shape[0]

    q = q_ref[...]
    lane = lax.broadcasted_iota(jnp.int32, q.shape, 1)
    zero = jnp.zeros_like(q)
    qm = [jnp.where(lane < DIFF_D, q, zero), jnp.where(lane >= DIFF_D, q, zero)]
    acc_s[...] = jnp.zeros_like(acc_s)

    def scores(k_t):
        return [_dot_nt(k_t, qm[0]), _dot_nt(k_t, qm[1])]

    def absorb(s_pair, vt_t, stats):
        new = []
        for m in range(2):
            m_old, l_old = stats[m]
            s_t = s_pair[m]
            m_new = jnp.maximum(m_old, jnp.max(s_t, axis=0, keepdims=True))
            alpha = jnp.exp2(m_old - m_new)
            p = jnp.exp2(s_t - m_new)
            l_new = alpha * l_old + jnp.sum(p, axis=0, keepdims=True)
            acc_s[m] = alpha * acc_s[m] + _dot(vt_t, p.astype(BF16))
            new.append((m_new, l_new))
        return tuple(new)

    init = (jnp.full((1, tq), -1e30, F32), jnp.zeros((1, tq), F32))
    stats = (init, init)
    for k_ref, vt_ref, n_keys in zip(k_refs, vt_refs, key_lens):
        tk = min(KEY_TILE, n_keys)
        nt = n_keys // tk
        if nt == 1:
            stats = absorb(scores(k_ref[...]), vt_ref[...], stats)
            continue
        assert nt % 2 == 0

        def k_tile(j, k_ref=k_ref, tk=tk):
            return k_ref[pl.ds(pl.multiple_of(j * tk, tk), tk), :]

        def vt_tile(j, vt_ref=vt_ref, tk=tk):
            return vt_ref[:, pl.ds(pl.multiple_of(j * tk, tk), tk)]

        s0 = scores(k_tile(0))
        sc_s[0] = s0[0]
        sc_s[1] = s0[1]

        def body(i, st, k_tile=k_tile, vt_tile=vt_tile, nt=nt):
            s_b = scores(k_tile(2 * i + 1))
            st = absorb([sc_s[0], sc_s[1]], vt_tile(2 * i), st)
            s_a = scores(k_tile(jnp.minimum(2 * i + 2, nt - 1)))
            st = absorb(s_b, vt_tile(2 * i + 1), st)
            sc_s[0] = s_a[0]
            sc_s[1] = s_a[1]
            return st
        stats = lax.fori_loop(0, nt // 2, body, stats)

    lv = lam_ref[...]
    lam = (jnp.exp(jnp.sum(lv[0:1] * lv[1:2], axis=1, keepdims=True))
           - jnp.exp(jnp.sum(lv[2:3] * lv[3:4], axis=1, keepdims=True)) + lam_init)
    o_t = acc_s[0] / stats[0][1] - lam * (acc_s[1] / stats[1][1])
    ms = jnp.mean(o_t * o_t, axis=0, keepdims=True)
    y_t = o_t * lax.rsqrt(ms + NORM_EPS) * subln_ref[...] * (1.0 - lam_init)
    y_ref[...] = y_t.T.astype(BF16)


def _attn_call(q, ks, vts, lam_vec, subln_col, bsz, n_q, key_lens, lam_init):
    tq = min(512, n_q)
    nq = n_q // tq
    nseg = len(key_lens)
    in_specs = [pl.BlockSpec((tq, LANES), lambda b, h, i: (b * nq + i, h))]
    in_specs += [pl.BlockSpec((n, LANES), lambda b, h, i: (b, h)) for n in key_lens]
    in_specs += [pl.BlockSpec((LANES, n), lambda b, h, i: (h, b)) for n in key_lens]
    in_specs += [pl.BlockSpec((4, DIFF_D), lambda b, h, i: (0, 0)),
                 pl.BlockSpec((DIFF_DV, 1), lambda b, h, i: (0, 0))]
    kern = functools.partial(_attn_kernel, key_lens=tuple(key_lens), lam_init=lam_init)
    return pl.pallas_call(
        kern,
        grid=(bsz, DIFF_HEADS, nq),
        in_specs=in_specs,
        out_specs=pl.BlockSpec((tq, LANES), lambda b, h, i: (b * nq + i, h)),
        out_shape=jax.ShapeDtypeStruct((bsz * n_q, D_MODEL), BF16),
        scratch_shapes=[pltpu.VMEM((2, DIFF_DV, tq), F32),
                        pltpu.VMEM((2, min(KEY_TILE, max(key_lens)), tq), F32)],
        compiler_params=_cparams(("parallel", "parallel", "arbitrary")),
    )(q, *ks, *vts, lam_vec, subln_col)


def _rope_tables(n_lat):
    rows = n_lat // GRID_W
    row = jnp.repeat(jnp.arange(rows), GRID_W).astype(F32)
    col = jnp.tile(jnp.arange(GRID_W), rows).astype(F32)
    half = DIFF_D // 2
    inv = ROPE_THETA ** (-(jnp.arange(0, half, 2, dtype=F32) / half))
    ang_r = row[:, None] * inv
    ang_c = col[:, None] * inv
    ang = jnp.concatenate([ang_r, ang_r, ang_c, ang_c], axis=-1)
    cos, sin = jnp.cos(ang), jnp.sin(ang)
    return jnp.concatenate([cos, cos], axis=-1), jnp.concatenate([sin, sin], axis=-1)


def _rotate_columns(w):
    d, n = w.shape
    quarter = DIFF_D // 4
    g = w.reshape(d, n // DIFF_D, 4, quarter)
    rot = jnp.stack([-g[:, :, 1], g[:, :, 0], -g[:, :, 3], g[:, :, 2]], axis=2)
    return rot.reshape(d, n)


def _lru_gate_weights(gate_w, gate_b, lam):
    nt = LRU_W // LANES
    per = LANES // LRU_BW
    blocks = gate_w.reshape(2, 2, nt, per, LRU_BW, LRU_BW)
    eye = jnp.eye(per, dtype=gate_w.dtype)
    dense = jnp.einsum('dgtpab,pq->dgtpaqb', blocks, eye).reshape(2, 2, nt, LANES, LANES)
    w = jnp.transpose(dense, (2, 3, 0, 1, 4)).reshape(nt, LANES, 4 * LANES)
    b = jnp.transpose(gate_b.reshape(2, 2, nt, LANES), (2, 0, 1, 3)).reshape(nt, 1, 4 * LANES)
    lm = jnp.transpose(lam.reshape(2, nt, LANES), (1, 0, 2)).reshape(nt, 1, 2 * LANES)
    return w.astype(BF16), b, lm


def _per_head_gates(ab, n_rows):
    t = ab.reshape(n_rows, 2, 2, GDN_HEADS)
    return jnp.transpose(t, (3, 0, 1, 2)).reshape(GDN_HEADS, n_rows, 4)


def kernel(x, c, ctx, c_ctx, ada_w, ada_b, ln_g, ln_b, mlp_w1, mlp_w2, mix_w_out, ev_w_in, ev_qkv_conv,
           ev_a_log, ev_dt_bias, ev_gdn_norm, ev_lru_conv_w, ev_lru_conv_b, ev_lru_gate_w, ev_lru_gate_b,
           ev_lru_lambda, od_w_qkv, od_lambda, od_subln):
    bsz, n_lat, d = x.shape
    n_ctx = ctx.shape[1]
    depth = ada_w.shape[0]
    assert d == D_MODEL and bsz + 1 <= MOD_ROWS
    assert n_lat % CHUNK == 0 and n_ctx % CHUNK == 0 and n_lat % GRID_W == 0
    alpha = (2 * depth) ** 0.25

    h_lat = x.reshape(bsz * n_lat, d)
    h_ctx = ctx.reshape(bsz * n_ctx, d)
    cc = jnp.concatenate([c_ctx[None, :], c, jnp.zeros((MOD_ROWS - 1 - bsz, d), F32)], axis=0)
    mod = _ada_call(cc, ada_w, ada_b).reshape(depth, MOD_ROWS, 6, d)
    cos2, sin2 = _rope_tables(n_lat)

    for i in range(depth):
        last = i == depth - 1
        j = i // 2
        w_out = mix_w_out[i].astype(BF16)
        lng0, lnb0 = ln_g[i, 0][None, :], ln_b[i, 0][None, :]
        lng1, lnb1 = ln_g[i, 1][None, :], ln_b[i, 1][None, :]
        w1 = mlp_w1[i].astype(BF16)
        w2 = mlp_w2[i].astype(BF16)
        if i % 2 == 0:
            w_in = ev_w_in[j]
            c0, c1, c2 = 4 * GDN_W, 4 * GDN_W + 4 * GDN_HEADS, 4 * GDN_W + 4 * GDN_HEADS + 2 * LRU_W
            w_main = jnp.concatenate([w_in[:, :c0], w_in[:, c1:c2]], axis=1).astype(BF16)
            w_ab = jnp.pad(w_in[:, c0:c1], ((0, 0), (0, LANES - 4 * GDN_HEADS))).astype(BF16)
            qkv_l, z_l, xr_l, gate_l, ab_l = _inproj_even_call(h_lat, mod, i, n_lat, w_main, w_ab)
            qkv_c, z_c, xr_c, gate_c, ab_c = _inproj_even_call(h_ctx, mod, i, None, w_main, w_ab)
            gparams = jnp.stack([ev_a_log[j, 0], ev_a_log[j, 1], ev_dt_bias[j, 0], ev_dt_bias[j, 1]],
                                axis=-1).reshape(GDN_HEADS, 1, 4)
            o_l, o_c = _gdn_call(qkv_l, qkv_c, _per_head_gates(ab_l, bsz * n_lat),
                                 _per_head_gates(ab_c, bsz * n_ctx), ev_qkv_conv[j], gparams,
                                 bsz, n_lat, n_ctx)
            gw, gb, lm = _lru_gate_weights(ev_lru_gate_w[j], ev_lru_gate_b[j], ev_lru_lambda[j])
            hr_l, hr_c = _lru_call(xr_l, xr_c, ev_lru_conv_w[j], ev_lru_conv_b[j][None, :], gw, gb, lm,
                                   bsz, n_lat, n_ctx)
            gn = ev_gdn_norm[j][None, :]
            h_lat = _out_even_call(o_l, z_l, hr_l, gate_l, h_lat, mod, i, n_lat, gn, w_out, lng0, lnb0, alpha)
            if not last:
                h_ctx = _out_even_call(o_c, z_c, hr_c, gate_c, h_ctx, mod, i, None, gn, w_out, lng0, lnb0,
                                       alpha)
        else:
            lam_init = 0.8 - 0.6 * math.exp(-0.3 * i)
            wq = od_w_qkv[j][:, :D_MODEL] * (DIFF_D ** -0.5 * math.log2(math.e))
            wk = od_w_qkv[j][:, D_MODEL:2 * D_MODEL]
            wvt = od_w_qkv[j][:, 2 * D_MODEL:].T.astype(BF16)
            rope_w = (wq.astype(BF16), _rotate_columns(wq).astype(BF16), wk.astype(BF16),
                      _rotate_columns(wk).astype(BF16), wvt)
            q_l, k_l, vt_l = _qkv_call(h_lat, mod, i, n_lat, rope_w, (cos2, sin2))
            q_c, k_c, vt_c = _qkv_call(h_ctx, mod, i, None, (rope_w[0], rope_w[2], wvt), None)
            subln_col = od_subln[j][:, None]
            y_l = _attn_call(q_l, [k_c, k_l], [vt_c, vt_l], od_lambda[j], subln_col, bsz, n_lat,
                             [n_ctx, n_lat], lam_init)
            h_lat = _out_odd_call(y_l, h_lat, mod, i, n_lat, w_out, lng0, lnb0, alpha)
            if not last:
                y_c = _attn_call(q_c, [k_c], [vt_c], od_lambda[j], subln_col, bsz, n_ctx, [n_ctx], lam_init)
                h_ctx = _out_odd_call(y_c, h_ctx, mod, i, None, w_out, lng0, lnb0, alpha)
        h_lat = _mlp_call(h_lat, mod, i, n_lat, w1, w2, lng1, lnb1, alpha)
        if not last:
            h_ctx = _mlp_call(h_ctx, mod, i, None, w1, w2, lng1, lnb1, alpha)
    return h_lat.reshape(bsz, n_lat, d)
```

```python
import functools
import math

import jax
import jax.numpy as jnp
from jax import lax
from jax.experimental import pallas as pl
from jax.experimental.pallas import tpu as pltpu

F32 = jnp.float32
BF16 = jnp.bfloat16

D_MODEL = 1024
D_FF = 4 * D_MODEL
CONV_K = 4
CONV_LEFT = 2
GDN_HEADS = 4
GDN_DK = 128
GDN_W = GDN_HEADS * GDN_DK
CHUNK = 64
LRU_W = D_MODEL - GDN_W
LRU_BLOCKS = 8
LRU_BW = LRU_W // LRU_BLOCKS
LRU_C = 8.0
DIFF_HEADS = 8
DIFF_D = 64
DIFF_DV = 128
ROPE_Q = DIFF_D // 4
GRID_W = 64
ROPE_THETA = 10000.0
NORM_EPS = 1e-6
LANES = 128
SUBLANES = 8
MOD_ROWS = 16
VMEM_LIMIT = 56 * 1024 * 1024


def _cparams(sem):
    return pltpu.CompilerParams(dimension_semantics=sem, vmem_limit_bytes=VMEM_LIMIT)


def _sigmoid(x):
    return 1.0 / (1.0 + jnp.exp(-x))


def _silu(x):
    return x * _sigmoid(x)


def _softplus(x):
    return jnp.maximum(x, 0.0) + jnp.log1p(jnp.exp(-jnp.abs(x)))


def _gelu_tanh(x):
    return 0.5 * x * (1.0 + jnp.tanh(math.sqrt(2.0 / math.pi) * (x + 0.044715 * (x * x * x))))


def _modulate(h, mod_ref, shift_row, scale_row):
    return h * (1.0 + mod_ref[scale_row:scale_row + 1, :]) + mod_ref[shift_row:shift_row + 1, :]


def _layernorm(x, g, b):
    mu = jnp.mean(x, axis=-1, keepdims=True)
    xc = x - mu
    var = jnp.mean(xc * xc, axis=-1, keepdims=True)
    return xc * lax.rsqrt(var + NORM_EPS) * g + b


def _dot(a, b):
    return jnp.dot(a, b, preferred_element_type=F32)


def _dot_nt(a, b):
    return lax.dot_general(a, b, (((1,), (1,)), ((), ())), preferred_element_type=F32)


def _dot_tn(a, b):
    return lax.dot_general(a, b, (((0,), (0,)), ((), ())), preferred_element_type=F32)


def _ada_kernel(c_ref, w_ref, b_ref, o_ref):
    s = _silu(c_ref[...])
    o_ref[...] = jnp.dot(s, w_ref[...], preferred_element_type=F32,
                         precision=lax.Precision.HIGHEST) + b_ref[...]


def _ada_call(cc, ada_w, ada_b):
    depth, d, n = ada_w.shape
    tn = D_MODEL
    return pl.pallas_call(
        _ada_kernel,
        grid=(depth, n // tn),
        in_specs=[pl.BlockSpec((MOD_ROWS, d), lambda i, j: (0, 0)),
                  pl.BlockSpec((None, d, tn), lambda i, j: (i, 0, j)),
                  pl.BlockSpec((None, 1, tn), lambda i, j: (i, 0, j))],
        out_specs=pl.BlockSpec((None, MOD_ROWS, tn), lambda i, j: (i, 0, j)),
        out_shape=jax.ShapeDtypeStruct((depth, MOD_ROWS, n), F32),
        compiler_params=_cparams(("parallel", "parallel")),
    )(cc, ada_w, ada_b.reshape(depth, 1, n))


def _mod_spec(layer, rows_per_mod):
    if rows_per_mod is None:
        return pl.BlockSpec((None, None, 6, D_MODEL), lambda t: (layer, 0, 0, 0))
    return pl.BlockSpec((None, None, 6, D_MODEL), lambda t: (layer, 1 + t // rows_per_mod, 0, 0))


def _const_spec(shape):
    nd = len(shape)
    return pl.BlockSpec(shape, lambda t: (0,) * nd)


def _token_tile(n_rows, per_batch):
    tm = min(512, n_rows if per_batch is None else per_batch)
    assert n_rows % tm == 0 and (per_batch is None or per_batch % tm == 0)
    return tm


def _inproj_even_kernel(h_ref, mod_ref, w_ref, wab_ref, qkv_ref, z_ref, xr_ref, gate_ref, ab_ref):
    u = _modulate(h_ref[...], mod_ref, 0, 1).astype(BF16)
    qkv_ref[...] = _dot(u, w_ref[:, 0:3 * GDN_W])
    z_ref[...] = _dot(u, w_ref[:, 3 * GDN_W:4 * GDN_W])
    xr_ref[...] = _dot(u, w_ref[:, 4 * GDN_W:4 * GDN_W + LRU_W])
    gate_ref[...] = _dot(u, w_ref[:, 4 * GDN_W + LRU_W:])
    ab_ref[...] = _dot(u, wab_ref[...])[:, 0:4 * GDN_HEADS]


def _inproj_even_call(h, mod, layer, per_batch, w_main, w_ab):
    n = h.shape[0]
    tm = _token_tile(n, per_batch)
    rpm = None if per_batch is None else per_batch // tm
    tok = lambda w: pl.BlockSpec((tm, w), lambda t: (t, 0))
    return pl.pallas_call(
        _inproj_even_kernel,
        grid=(n // tm,),
        in_specs=[tok(D_MODEL), _mod_spec(layer, rpm), _const_spec(w_main.shape), _const_spec(w_ab.shape)],
        out_specs=[tok(3 * GDN_W), tok(GDN_W), tok(LRU_W), tok(LRU_W), tok(4 * GDN_HEADS)],
        out_shape=[jax.ShapeDtypeStruct((n, 3 * GDN_W), F32), jax.ShapeDtypeStruct((n, GDN_W), F32),
                   jax.ShapeDtypeStruct((n, LRU_W), F32), jax.ShapeDtypeStruct((n, LRU_W), F32),
                   jax.ShapeDtypeStruct((n, 4 * GDN_HEADS), F32)],
        compiler_params=_cparams(("parallel",)),
    )(h, mod, w_main, w_ab)


def _dwconv(x_ref, w_ref, pad_ref, n):
    zeros = jnp.zeros((SUBLANES, LANES), F32)
    pad_ref[0:SUBLANES, :] = zeros
    pad_ref[SUBLANES + n:2 * SUBLANES + n, :] = zeros
    pad_ref[SUBLANES:SUBLANES + n, :] = x_ref[...]
    y = None
    for j in range(CONV_K):
        off = SUBLANES + j - CONV_LEFT
        term = pad_ref[off:off + n, :] * w_ref[j:j + 1, :]
        y = term if y is None else y + term
    return y


def _hp_parts(x):
    hi = x.astype(BF16).astype(F32)
    return hi, x - hi


def _mm_hp(x, y):
    xh, xl = _hp_parts(x)
    yh, yl = _hp_parts(y)
    x2 = xh + pltpu.roll(xl, CHUNK, 1)
    lhs = jnp.concatenate([x2, x2], axis=1).astype(BF16)
    rhs = jnp.concatenate([yh, yh, yl, yl], axis=0).astype(BF16)
    return _dot(lhs, rhs)


def _unit_tri_inverse(a, eye):
    p = eye - a
    q = _mm_hp(a, a)
    sq = 2
    while sq * 2 < CHUNK:
        r = _mm_hp(jnp.concatenate([p, q], axis=0), q)
        p = p + r[0:CHUNK]
        q = r[CHUNK:2 * CHUNK]
        sq *= 2
    return p + _mm_hp(p, q)


def _gdn_kernel(ql_ref, kl_ref, vl_ref, qc_ref, kc_ref, vc_ref, wq_ref, wk_ref, wv_ref,
                abl_ref, abc_ref, gp_ref, ol_ref, oc_ref,
                pad_s, q_s, k_s, v_s, ab_s, mt_s, nt_s, qe_s, gl_s, st_s, *, n_lat, n_ctx):
    n_tot = n_ctx + n_lat

    def prep(x_ref, w_ref, n, dst, base, kind):
        y = _silu(_dwconv(x_ref, w_ref, pad_s, n))
        if kind != "v":
            y = y * lax.rsqrt(jnp.sum(y * y, axis=-1, keepdims=True) + NORM_EPS)
        if kind == "q":
            y = y * (GDN_DK ** -0.5)
        dst[base:base + n, :] = y

    prep(qc_ref, wq_ref, n_ctx, q_s, 0, "q")
    prep(kc_ref, wk_ref, n_ctx, k_s, 0, "k")
    prep(vc_ref, wv_ref, n_ctx, v_s, 0, "v")
    prep(ql_ref, wq_ref, n_lat, q_s, n_ctx, "q")
    prep(kl_ref, wk_ref, n_lat, k_s, n_ctx, "k")
    prep(vl_ref, wv_ref, n_lat, v_s, n_ctx, "v")
    ab_s[0:n_ctx, :] = abc_ref[...]
    ab_s[n_ctx:n_tot, :] = abl_ref[...]

    row = lax.broadcasted_iota(jnp.int32, (CHUNK, LANES), 0)
    col = lax.broadcasted_iota(jnp.int32, (CHUNK, LANES), 1)
    left = col < CHUNK
    eye = jnp.where(row == col, 1.0, 0.0).astype(F32)
    incl = [jnp.logical_and(row >= col, left), jnp.logical_and(row <= col, left)]
    strict = [jnp.logical_and(row > col, left), jnp.logical_and(row < col, left)]
    ones_left = jnp.where(left, 1.0, 0.0).astype(F32)
    gp = gp_ref[...]

    def chunk_group(chunk0, out_ref, group, gi):
        chains = []
        for g in range(group):
            lc = gi * group + g
            c = chunk0 + lc
            rows = pl.ds(pl.multiple_of(c * CHUNK, CHUNK), CHUNK)
            orow = pl.ds(pl.multiple_of(lc * CHUNK, CHUNK), CHUNK)
            q_c = q_s[rows, :]
            k_c = k_s[rows, :]
            v_c = v_s[rows, :]
            ab_c = ab_s[rows, :]
            kb = k_c.astype(BF16)
            k_pad = jnp.concatenate([kb, jnp.zeros_like(kb)], axis=0)
            qk_kk = _dot_nt(jnp.concatenate([q_c.astype(BF16), kb], axis=0), k_pad)
            for d in range(2):
                chains.append(dict(c=c, rows=rows, orow=orow, d=d, q=q_c, k=k_c, v=v_c, ab=ab_c,
                                   qk_raw=qk_kk[0:CHUNK], kk=qk_kk[CHUNK:2 * CHUNK]))
        for ch in chains:
            d = ch["d"]
            a_col = ch["ab"][:, d:d + 1]
            b_col = ch["ab"][:, 2 + d:3 + d]
            g_col = -jnp.exp(gp[:, d:d + 1]) * _softplus(a_col + gp[:, 2 + d:3 + d])
            ch["beta"] = _sigmoid(b_col)
            g_b = jnp.broadcast_to(g_col, (CHUNK, LANES))
            m_t = incl[1 - d]
            gh, gl = _hp_parts(g_b)
            cum = jnp.where(incl[d], 1.0, 0.0).astype(F32)
            lhs1 = cum + pltpu.roll(ones_left, CHUNK, 1)
            lhs = jnp.concatenate([lhs1, lhs1], axis=1).astype(BF16)
            rhs = jnp.concatenate([gh, jnp.where(m_t, -gh, 0.0), gl, jnp.where(m_t, -gl, 0.0)],
                                  axis=0).astype(BF16)
            ch["diff"] = _dot(lhs, rhs)
            ch["tot"] = jnp.sum(g_b, axis=0, keepdims=True)
        for ch in chains:
            d = ch["d"]
            ch["decay"] = jnp.exp(jnp.where(incl[d], ch["diff"], -1e30))
            a_mat = jnp.where(strict[d], ch["beta"] * ch["kk"] * ch["decay"], 0.0)
            ch["p"] = eye - a_mat
            ch["a"] = a_mat
        for ch in chains:
            ch["q2"] = _mm_hp(ch["a"], ch["a"])
        sq = 2
        while sq * 2 < CHUNK:
            for ch in chains:
                r = _mm_hp(jnp.concatenate([ch["p"], ch["q2"]], axis=0), ch["q2"])
                ch["p"] = ch["p"] + r[0:CHUNK]
                ch["q2"] = r[CHUNK:2 * CHUNK]
            sq *= 2
        for ch in chains:
            ch["t"] = ch["p"] + _mm_hp(ch["p"], ch["q2"])
        for ch in chains:
            gam = jnp.where(left, pltpu.roll(ch["diff"], CHUNK, 1), ch["diff"])
            egam = jnp.exp(gam)
            beta = ch["beta"]
            rhs_uw = jnp.concatenate([ch["k"] * (beta * egam), ch["v"] * beta], axis=1).astype(BF16)
            ch["wu"] = _dot(ch["t"][:, 0:CHUNK].astype(BF16), rhs_uw).astype(BF16)
            ch["qg"] = ch["q"] * egam
            ch["kd"] = (ch["k"] * jnp.exp(ch["tot"] - gam)).astype(BF16)
        for ch in chains:
            d, c = ch["d"], ch["c"]
            qk_m = (ch["qk_raw"] * ch["decay"])[:, 0:CHUNK].astype(BF16)
            x = _dot(qk_m, ch["wu"])
            mn = _dot_tn(ch["wu"], ch["kd"])
            qe_s[d, ch["rows"], :] = (ch["qg"] - x[:, 0:LANES]).astype(BF16)
            out_ref[ch["orow"], :] = out_ref[ch["orow"], :] + x[:, LANES:2 * LANES]
            mt_s[d, c] = mn[0:LANES].astype(BF16)
            nt_s[d, c] = mn[LANES:2 * LANES]
            gl_s[d, c] = jnp.broadcast_to(jnp.exp(ch["tot"]), (SUBLANES, LANES))

    def prep_segment(chunk0, n_chunks, out_ref):
        group = max(g for g in (8, 4, 2, 1) if n_chunks % g == 0)

        def prep(gi, carry):
            chunk_group(chunk0, out_ref, group, gi)
            return carry
        lax.fori_loop(0, n_chunks // group, prep, 0)

    def rec_segment(chunk0, n_chunks, out_ref):
        def step(s, carry):
            for d in range(2):
                lc = s if d == 0 else n_chunks - 1 - s
                c = chunk0 + lc
                rows = pl.ds(pl.multiple_of(c * CHUNK, CHUNK), CHUNK)
                orow = pl.ds(pl.multiple_of(lc * CHUNK, CHUNK), CHUNK)
                st = st_s[d]
                stb = st.astype(BF16)
                r = _dot(stb, mt_s[d, c])
                out_ref[orow, :] = out_ref[orow, :] + _dot_nt(qe_s[d, rows, :], stb)
                st_s[d] = st * gl_s[d, c][0:1, :] + nt_s[d, c] - r
            return carry
        lax.fori_loop(0, n_chunks, step, 0)

    st_s[...] = jnp.zeros_like(st_s)
    ol_ref[...] = jnp.zeros_like(ol_ref)
    oc_ref[...] = jnp.zeros_like(oc_ref)
    prep_segment(0, n_ctx // CHUNK, oc_ref)
    prep_segment(n_ctx // CHUNK, n_lat // CHUNK, ol_ref)
    rec_segment(0, n_ctx // CHUNK, oc_ref)
    rec_segment(n_ctx // CHUNK, n_lat // CHUNK, ol_ref)


def _gdn_call(qkv_l, qkv_c, ab_l, ab_c, conv_w, gparams, bsz, n_lat, n_ctx):
    n_tot = n_lat + n_ctx
    nh = GDN_HEADS
    lat = lambda off: pl.BlockSpec((n_lat, LANES), lambda b, h: (b, off + h))
    ctx = lambda off: pl.BlockSpec((n_ctx, LANES), lambda b, h: (b, off + h))
    cw = lambda off: pl.BlockSpec((CONV_K, LANES), lambda b, h: (0, off + h))
    kern = functools.partial(_gdn_kernel, n_lat=n_lat, n_ctx=n_ctx)
    return pl.pallas_call(
        kern,
        grid=(bsz, nh),
        in_specs=[lat(0), lat(nh), lat(2 * nh), ctx(0), ctx(nh), ctx(2 * nh), cw(0), cw(nh), cw(2 * nh),
                  pl.BlockSpec((None, n_lat, 4), lambda b, h: (h, b, 0)),
                  pl.BlockSpec((None, n_ctx, 4), lambda b, h: (h, b, 0)),
                  pl.BlockSpec((None, 1, 4), lambda b, h: (h, 0, 0))],
        out_specs=[pl.BlockSpec((n_lat, LANES), lambda b, h: (b, h)),
                   pl.BlockSpec((n_ctx, LANES), lambda b, h: (b, h))],
        out_shape=[jax.ShapeDtypeStruct((bsz * n_lat, GDN_W), F32),
                   jax.ShapeDtypeStruct((bsz * n_ctx, GDN_W), F32)],
        scratch_shapes=[
            pltpu.VMEM((n_lat + 2 * SUBLANES, LANES), F32),
            pltpu.VMEM((n_tot, LANES), F32),
            pltpu.VMEM((n_tot, LANES), F32),
            pltpu.VMEM((n_tot, LANES), F32),
            pltpu.VMEM((n_tot, 4), F32),
            pltpu.VMEM((2, n_tot // CHUNK, GDN_DK, LANES), BF16),
            pltpu.VMEM((2, n_tot // CHUNK, LANES, GDN_DK), F32),
            pltpu.VMEM((2, n_tot, LANES), BF16),
            pltpu.VMEM((2, n_tot // CHUNK, SUBLANES, LANES), F32),
            pltpu.VMEM((2, GDN_DK, LANES), F32),
        ],
        compiler_params=_cparams(("parallel", "arbitrary")),
    )(qkv_l, qkv_l, qkv_l, qkv_c, qkv_c, qkv_c, conv_w, conv_w, conv_w, ab_l, ab_c, gparams)


def _lru_kernel(xl_ref, xc_ref, cw_ref, cb_ref, gw_ref, gb_ref, lam_ref, hl_ref, hc_ref,
                pad_s, x_s, a_s, u_s, *, n_lat, n_ctx):
    n_tot = n_ctx + n_lat
    x_s[0:n_ctx, :] = _dwconv(xc_ref, cw_ref, pad_s, n_ctx) + cb_ref[...]
    x_s[n_ctx:n_tot, :] = _dwconv(xl_ref, cw_ref, pad_s, n_lat) + cb_ref[...]

    sp = [_softplus(-lam_ref[:, d * LANES:(d + 1) * LANES]) for d in range(2)]
    tile = min(256, n_ctx)

    def coeffs(t, carry):
        rows = pl.ds(pl.multiple_of(t * tile, tile), tile)
        x = x_s[rows, :]
        gates = _dot(x.astype(BF16), gw_ref[...]) + gb_ref[...]
        for d in range(2):
            r = _sigmoid(gates[:, (2 * d) * LANES:(2 * d + 1) * LANES])
            i = _sigmoid(gates[:, (2 * d + 1) * LANES:(2 * d + 2) * LANES])
            log_a = -LRU_C * r * sp[d]
            a_s[d, rows, :] = jnp.exp(log_a)
            th = jnp.tanh(log_a)
            u_s[d, rows, :] = jnp.sqrt(-2.0 * th / (1.0 - th)) * (i * x)
        return carry

    lax.fori_loop(0, n_tot // tile, coeffs, 0)

    sub = lax.broadcasted_iota(jnp.int32, (SUBLANES, LANES), 0)

    def group_scan(d, g0):
        rows = pl.ds(pl.multiple_of(g0, SUBLANES), SUBLANES)
        a = a_s[d, rows, :]
        u = u_s[d, rows, :]
        sh = 1
        while sh < SUBLANES:
            if d == 0:
                a_sh, u_sh, ok = pltpu.roll(a, sh, 0), pltpu.roll(u, sh, 0), sub >= sh
            else:
                a_sh, u_sh = pltpu.roll(a, SUBLANES - sh, 0), pltpu.roll(u, SUBLANES - sh, 0)
                ok = sub < SUBLANES - sh
            u = u + a * jnp.where(ok, u_sh, 0.0)
            a = a * jnp.where(ok, a_sh, 1.0)
            sh *= 2
        return rows, u, a

    def fwd(base, n, h0):
        def body(g, h):
            rows, u, a = group_scan(0, base + g * SUBLANES)
            hh = u + a * h
            u_s[0, rows, :] = hh
            return jnp.broadcast_to(hh[SUBLANES - 1:SUBLANES, :], (SUBLANES, LANES))
        return lax.fori_loop(0, n // SUBLANES, body, h0, unroll=4)

    def bwd(base, n, h0, out_ref):
        def body(g, h):
            lg = n // SUBLANES - 1 - g
            rows, u, a = group_scan(1, base + lg * SUBLANES)
            hh = u + a * h
            orow = pl.ds(pl.multiple_of(lg * SUBLANES, SUBLANES), SUBLANES)
            out_ref[orow, :] = u_s[0, rows, :] + hh
            return jnp.broadcast_to(hh[0:1, :], (SUBLANES, LANES))
        return lax.fori_loop(0, n // SUBLANES, body, h0, unroll=4)

    zero = jnp.zeros((SUBLANES, LANES), F32)
    h = fwd(0, n_ctx, zero)
    fwd(n_ctx, n_lat, h)
    h = bwd(0, n_ctx, zero, hc_ref)
    bwd(n_ctx, n_lat, h, hl_ref)


def _lru_call(xr_l, xr_c, conv_w, conv_b, gate_w, gate_b, lam, bsz, n_lat, n_ctx):
    n_tot = n_lat + n_ctx
    nt = LRU_W // LANES
    kern = functools.partial(_lru_kernel, n_lat=n_lat, n_ctx=n_ctx)
    return pl.pallas_call(
        kern,
        grid=(bsz, nt),
        in_specs=[pl.BlockSpec((n_lat, LANES), lambda b, j: (b, j)),
                  pl.BlockSpec((n_ctx, LANES), lambda b, j: (b, j)),
                  pl.BlockSpec((CONV_K, LANES), lambda b, j: (0, j)),
                  pl.BlockSpec((1, LANES), lambda b, j: (0, j)),
                  pl.BlockSpec((None, LANES, 4 * LANES), lambda b, j: (j, 0, 0)),
                  pl.BlockSpec((None, 1, 4 * LANES), lambda b, j: (j, 0, 0)),
                  pl.BlockSpec((None, 1, 2 * LANES), lambda b, j: (j, 0, 0))],
        out_specs=[pl.BlockSpec((n_lat, LANES), lambda b, j: (b, j)),
                   pl.BlockSpec((n_ctx, LANES), lambda b, j: (b, j))],
        out_shape=[jax.ShapeDtypeStruct((bsz * n_lat, LRU_W), F32),
                   jax.ShapeDtypeStruct((bsz * n_ctx, LRU_W), F32)],
        scratch_shapes=[pltpu.VMEM((n_lat + 2 * SUBLANES, LANES), F32),
                        pltpu.VMEM((n_tot, LANES), F32),
                        pltpu.VMEM((2, n_tot, LANES), F32),
                        pltpu.VMEM((2, n_tot, LANES), F32)],
        compiler_params=_cparams(("parallel", "arbitrary")),
    )(xr_l, xr_c, conv_w, conv_b, gate_w, gate_b, lam)


def _residual_ln(h, y, mod_ref, gate_row, lng_ref, lnb_ref, alpha):
    return _layernorm(alpha * h + mod_ref[gate_row:gate_row + 1, :] * y, lng_ref[...], lnb_ref[...])


def _out_even_kernel(o_ref, z_ref, hr_ref, gate_ref, h_ref, mod_ref, gn_ref, w_ref, lng_ref, lnb_ref,
                     out_ref, *, alpha):
    parts = []
    for hd in range(GDN_HEADS):
        sl = slice(hd * LANES, (hd + 1) * LANES)
        o = o_ref[:, sl]
        on = o * lax.rsqrt(jnp.mean(o * o, axis=-1, keepdims=True) + NORM_EPS) * gn_ref[...]
        parts.append(on * _silu(z_ref[:, sl]))
    y_gdn = jnp.concatenate(parts, axis=1).astype(BF16)
    y_lru = (hr_ref[...] * _gelu_tanh(gate_ref[...])).astype(BF16)
    y = _dot(y_gdn, w_ref[0:GDN_W, :]) + _dot(y_lru, w_ref[GDN_W:, :])
    out_ref[...] = _residual_ln(h_ref[...], y, mod_ref, 2, lng_ref, lnb_ref, alpha)


def _out_odd_kernel(y_ref, h_ref, mod_ref, w_ref, lng_ref, lnb_ref, out_ref, *, alpha):
    y = _dot(y_ref[...], w_ref[...])
    out_ref[...] = _residual_ln(h_ref[...], y, mod_ref, 2, lng_ref, lnb_ref, alpha)


def _out_even_call(o, z, hr, gate, h, mod, layer, per_batch, gn, w_out, ln_g, ln_b, alpha):
    n = h.shape[0]
    tm = _token_tile(n, per_batch)
    rpm = None if per_batch is None else per_batch // tm
    tok = lambda w: pl.BlockSpec((tm, w), lambda t: (t, 0))
    return pl.pallas_call(
        functools.partial(_out_even_kernel, alpha=alpha),
        grid=(n // tm,),
        in_specs=[tok(GDN_W), tok(GDN_W), tok(LRU_W), tok(LRU_W), tok(D_MODEL), _mod_spec(layer, rpm),
                  _const_spec(gn.shape), _const_spec(w_out.shape), _const_spec(ln_g.shape),
                  _const_spec(ln_b.shape)],
        out_specs=tok(D_MODEL),
        out_shape=jax.ShapeDtypeStruct((n, D_MODEL), F32),
        compiler_params=_cparams(("parallel",)),
    )(o, z, hr, gate, h, mod, gn, w_out, ln_g, ln_b)


def _out_odd_call(y, h, mod, layer, per_batch, w_out, ln_g, ln_b, alpha):
    n = h.shape[0]
    tm = _token_tile(n, per_batch)
    rpm = None if per_batch is None else per_batch // tm
    tok = lambda w: pl.BlockSpec((tm, w), lambda t: (t, 0))
    return pl.pallas_call(
        functools.partial(_out_odd_kernel, alpha=alpha),
        grid=(n // tm,),
        in_specs=[tok(D_MODEL), tok(D_MODEL), _mod_spec(layer, rpm), _const_spec(w_out.shape),
                  _const_spec(ln_g.shape), _const_spec(ln_b.shape)],
        out_specs=tok(D_MODEL),
        out_shape=jax.ShapeDtypeStruct((n, D_MODEL), F32),
        compiler_params=_cparams(("parallel",)),
    )(y, h, mod, w_out, ln_g, ln_b)


FF_TILE = 1024


def _mlp_kernel(h_ref, mod_ref, w1_ref, w2_ref, lng_ref, lnb_ref, out_ref, *, alpha):
    h = h_ref[...]
    u = _modulate(h, mod_ref, 3, 4).astype(BF16)
    acc = None
    for j in range(D_FF // FF_TILE):
        sl = slice(j * FF_TILE, (j + 1) * FF_TILE)
        a = jnp.maximum(_dot(u, w1_ref[:, sl]), 0.0)
        part = _dot((a * a).astype(BF16), w2_ref[sl, :])
        acc = part if acc is None else acc + part
    out_ref[...] = _residual_ln(h, acc, mod_ref, 5, lng_ref, lnb_ref, alpha)


def _mlp_call(h, mod, layer, per_batch, w1, w2, ln_g, ln_b, alpha):
    n = h.shape[0]
    tm = _token_tile(n, per_batch)
    rpm = None if per_batch is None else per_batch // tm
    tok = pl.BlockSpec((tm, D_MODEL), lambda t: (t, 0))
    return pl.pallas_call(
        functools.partial(_mlp_kernel, alpha=alpha),
        grid=(n // tm,),
        in_specs=[tok, _mod_spec(layer, rpm), _const_spec(w1.shape), _const_spec(w2.shape),
                  _const_spec(ln_g.shape), _const_spec(ln_b.shape)],
        out_specs=tok,
        out_shape=jax.ShapeDtypeStruct((n, D_MODEL), F32),
        compiler_params=_cparams(("parallel",)),
    )(h, mod, w1, w2, ln_g, ln_b)


def _qkv_rope_kernel(h_ref, mod_ref, wq_ref, wk_ref, wvt_ref, cos_ref, sin_ref, q_ref, k_ref, vt_ref):
    u = _modulate(h_ref[...], mod_ref, 0, 1).astype(BF16)
    cos = cos_ref[...]
    sin_signed = sin_ref[...]
    lane = lax.broadcasted_iota(jnp.int32, cos.shape, 1)
    partner_above = (lane % (2 * ROPE_Q)) < ROPE_Q

    def rope(y):
        parts = []
        for hd in range(DIFF_HEADS):
            t = y[:, hd * LANES:(hd + 1) * LANES]
            partner = jnp.where(partner_above, pltpu.roll(t, LANES - ROPE_Q, 1), pltpu.roll(t, ROPE_Q, 1))
            parts.append(t * cos + partner * sin_signed)
        return jnp.concatenate(parts, axis=1).astype(BF16)

    q_ref[...] = rope(_dot(u, wq_ref[...]))
    k_ref[...] = rope(_dot(u, wk_ref[...]))
    vt_ref[...] = _dot_nt(wvt_ref[...], u).astype(BF16)


def _qkv_plain_kernel(h_ref, mod_ref, wq_ref, wk_ref, wvt_ref, q_ref, k_ref, vt_ref):
    u = _modulate(h_ref[...], mod_ref, 0, 1).astype(BF16)
    q_ref[...] = _dot(u, wq_ref[...]).astype(BF16)
    k_ref[...] = _dot(u, wk_ref[...]).astype(BF16)
    vt_ref[...] = _dot_nt(wvt_ref[...], u).astype(BF16)


def _qkv_call(h, mod, layer, per_batch, weights, rope):
    n = h.shape[0]
    tm = _token_tile(n, per_batch)
    tok = pl.BlockSpec((tm, D_MODEL), lambda t: (t, 0))
    out_specs = [tok, tok, pl.BlockSpec((D_MODEL, tm), lambda t: (0, t))]
    out_shape = [jax.ShapeDtypeStruct((n, D_MODEL), BF16), jax.ShapeDtypeStruct((n, D_MODEL), BF16),
                 jax.ShapeDtypeStruct((D_MODEL, n), BF16)]
    wspec = _const_spec((D_MODEL, D_MODEL))
    if rope is not None:
        cos2, sin2 = rope
        rpm = per_batch // tm
        tab = pl.BlockSpec((tm, LANES), lambda t: (t % rpm, 0))
        wq, wk, wvt = weights
        return pl.pallas_call(
            _qkv_rope_kernel, grid=(n // tm,),
            in_specs=[tok, _mod_spec(layer, rpm), wspec, wspec, wspec, tab, tab],
            out_specs=out_specs, out_shape=out_shape,
            compiler_params=_cparams(("parallel",)),
        )(h, mod, wq, wk, wvt, cos2, sin2)
    wq, wk, wvt = weights
    return pl.pallas_call(
        _qkv_plain_kernel, grid=(n // tm,),
        in_specs=[tok, _mod_spec(layer, None), wspec, wspec, wspec],
        out_specs=out_specs, out_shape=out_shape,
        compiler_params=_cparams(("parallel",)),
    )(h, mod, wq, wk, wvt)


KEY_TILE = 256


def _attn_kernel(*refs, key_lens, lam_init):
    nseg = len(key_lens)
    q_ref = refs[0]
    k_refs = refs[1:1 + nseg]
    vt_refs = refs[1 + nseg:1 + 2 * nseg]
    lam_ref, subln_ref, y_ref, acc_s = refs[1 + 2 * nseg:]
    tq = q_ref.shape[0]

    q = q_ref[...]
    lane = lax.broadcasted_iota(jnp.int32, q.shape, 1)
    zero = jnp.zeros_like(q)
    qm = [jnp.where(lane < DIFF_D, q, zero), jnp.where(lane >= DIFF_D, q, zero)]
    acc_s[...] = jnp.zeros_like(acc_s)

    def scores(k_t):
        return [_dot_nt(k_t, qm[0]), _dot_nt(k_t, qm[1])]

    def absorb(s_pair, vt_t, stats):
        new = []
        for m in range(2):
            m_old, l_old = stats[m]
            s_t = s_pair[m]
            m_new = jnp.maximum(m_old, jnp.max(s_t, axis=0, keepdims=True))
            alpha = jnp.exp2(m_old - m_new)
            p = jnp.exp2(s_t - m_new)
            l_new = alpha * l_old + jnp.sum(p, axis=0, keepdims=True)
            acc_s[m] = alpha * acc_s[m] + _dot(vt_t, p.astype(BF16))
            new.append((m_new, l_new))
        return tuple(new)

    init = (jnp.full((1, tq), -1e30, F32), jnp.zeros((1, tq), F32))
    stats = (init, init)
    tiles = []
    for k_ref, vt_ref, n_keys in zip(k_refs, vt_refs, key_lens):
        tk = min(KEY_TILE, n_keys)
        tiles += [(k_ref, vt_ref, j * tk, tk) for j in range(n_keys // tk)]
    s_next = scores(tiles[0][0][tiles[0][2]:tiles[0][2] + tiles[0][3], :])
    for t, (k_ref, vt_ref, k0, tk) in enumerate(tiles):
        s_cur = s_next
        if t + 1 < len(tiles):
            nk_ref, _, nk0, ntk = tiles[t + 1]
            s_next = scores(nk_ref[nk0:nk0 + ntk, :])
        stats = absorb(s_cur, vt_ref[:, k0:k0 + tk], stats)

    lv = lam_ref[...]
    lam = (jnp.exp(jnp.sum(lv[0:1] * lv[1:2], axis=1, keepdims=True))
           - jnp.exp(jnp.sum(lv[2:3] * lv[3:4], axis=1, keepdims=True)) + lam_init)
    o_t = acc_s[0] / stats[0][1] - lam * (acc_s[1] / stats[1][1])
    ms = jnp.mean(o_t * o_t, axis=0, keepdims=True)
    y_t = o_t * lax.rsqrt(ms + NORM_EPS) * subln_ref[...] * (1.0 - lam_init)
    y_ref[...] = y_t.T.astype(BF16)


def _attn_call(q, ks, vts, lam_vec, subln_col, bsz, n_q, key_lens, lam_init):
    tq = min(512, n_q)
    nq = n_q // tq
    nseg = len(key_lens)
    in_specs = [pl.BlockSpec((tq, LANES), lambda b, h, i: (b * nq + i, h))]
    in_specs += [pl.BlockSpec((n, LANES), lambda b, h, i: (b, h)) for n in key_lens]
    in_specs += [pl.BlockSpec((LANES, n), lambda b, h, i: (h, b)) for n in key_lens]
    in_specs += [pl.BlockSpec((4, DIFF_D), lambda b, h, i: (0, 0)),
                 pl.BlockSpec((DIFF_DV, 1), lambda b, h, i: (0, 0))]
    kern = functools.partial(_attn_kernel, key_lens=tuple(key_lens), lam_init=lam_init)
    return pl.pallas_call(
        kern,
        grid=(bsz, DIFF_HEADS, nq),
        in_specs=in_specs,
        out_specs=pl.BlockSpec((tq, LANES), lambda b, h, i: (b * nq + i, h)),
        out_shape=jax.ShapeDtypeStruct((bsz * n_q, D_MODEL), BF16),
        scratch_shapes=[pltpu.VMEM((2, DIFF_DV, tq), F32)],
        compiler_params=_cparams(("parallel", "parallel", "arbitrary")),
    )(q, *ks, *vts, lam_vec, subln_col)


def _rope_tables(n_lat):
    rows = n_lat // GRID_W
    row = jnp.repeat(jnp.arange(rows), GRID_W).astype(F32)
    col = jnp.tile(jnp.arange(GRID_W), rows).astype(F32)
    half = DIFF_D // 2
    inv = ROPE_THETA ** (-(jnp.arange(0, half, 2, dtype=F32) / half))
    ang_r = row[:, None] * inv
    ang_c = col[:, None] * inv
    ang = jnp.concatenate([ang_r, ang_r, ang_c, ang_c], axis=-1)
    cos, sin = jnp.cos(ang), jnp.sin(ang)
    sign = jnp.where((jnp.arange(DIFF_D) % (2 * ROPE_Q)) < ROPE_Q, -1.0, 1.0).astype(F32)
    sin_signed = sin * sign
    return jnp.concatenate([cos, cos], axis=-1), jnp.concatenate([sin_signed, sin_signed], axis=-1)


def _lru_gate_weights(gate_w, gate_b, lam):
    nt = LRU_W // LANES
    per = LANES // LRU_BW
    blocks = gate_w.reshape(2, 2, nt, per, LRU_BW, LRU_BW)
    eye = jnp.eye(per, dtype=gate_w.dtype)
    dense = jnp.einsum('dgtpab,pq->dgtpaqb', blocks, eye).reshape(2, 2, nt, LANES, LANES)
    w = jnp.transpose(dense, (2, 3, 0, 1, 4)).reshape(nt, LANES, 4 * LANES)
    b = jnp.transpose(gate_b.reshape(2, 2, nt, LANES), (2, 0, 1, 3)).reshape(nt, 1, 4 * LANES)
    lm = jnp.transpose(lam.reshape(2, nt, LANES), (1, 0, 2)).reshape(nt, 1, 2 * LANES)
    return w.astype(BF16), b, lm


def _per_head_gates(ab, n_rows):
    t = ab.reshape(n_rows, 2, 2, GDN_HEADS)
    return jnp.transpose(t, (3, 0, 1, 2)).reshape(GDN_HEADS, n_rows, 4)


def kernel(x, c, ctx, c_ctx, ada_w, ada_b, ln_g, ln_b, mlp_w1, mlp_w2, mix_w_out, ev_w_in, ev_qkv_conv,
           ev_a_log, ev_dt_bias, ev_gdn_norm, ev_lru_conv_w, ev_lru_conv_b, ev_lru_gate_w, ev_lru_gate_b,
           ev_lru_lambda, od_w_qkv, od_lambda, od_subln):
    bsz, n_lat, d = x.shape
    n_ctx = ctx.shape[1]
    depth = ada_w.shape[0]
    assert d == D_MODEL and bsz + 1 <= MOD_ROWS
    assert n_lat % CHUNK == 0 and n_ctx % CHUNK == 0 and n_lat % GRID_W == 0
    alpha = (2 * depth) ** 0.25

    h_lat = x.reshape(bsz * n_lat, d)
    h_ctx = ctx.reshape(bsz * n_ctx, d)
    cc = jnp.concatenate([c_ctx[None, :], c, jnp.zeros((MOD_ROWS - 1 - bsz, d), F32)], axis=0)
    mod = _ada_call(cc, ada_w, ada_b).reshape(depth, MOD_ROWS, 6, d)
    cos2, sin2 = _rope_tables(n_lat)

    for i in range(depth):
        last = i == depth - 1
        j = i // 2
        w_out = mix_w_out[i].astype(BF16)
        lng0, lnb0 = ln_g[i, 0][None, :], ln_b[i, 0][None, :]
        lng1, lnb1 = ln_g[i, 1][None, :], ln_b[i, 1][None, :]
        w1 = mlp_w1[i].astype(BF16)
        w2 = mlp_w2[i].astype(BF16)
        if i % 2 == 0:
            w_in = ev_w_in[j]
            c0, c1, c2 = 4 * GDN_W, 4 * GDN_W + 4 * GDN_HEADS, 4 * GDN_W + 4 * GDN_HEADS + 2 * LRU_W
            w_main = jnp.concatenate([w_in[:, :c0], w_in[:, c1:c2]], axis=1).astype(BF16)
            w_ab = jnp.pad(w_in[:, c0:c1], ((0, 0), (0, LANES - 4 * GDN_HEADS))).astype(BF16)
            qkv_l, z_l, xr_l, gate_l, ab_l = _inproj_even_call(h_lat, mod, i, n_lat, w_main, w_ab)
            qkv_c, z_c, xr_c, gate_c, ab_c = _inproj_even_call(h_ctx, mod, i, None, w_main, w_ab)
            gparams = jnp.stack([ev_a_log[j, 0], ev_a_log[j, 1], ev_dt_bias[j, 0], ev_dt_bias[j, 1]],
                                axis=-1).reshape(GDN_HEADS, 1, 4)
            o_l, o_c = _gdn_call(qkv_l, qkv_c, _per_head_gates(ab_l, bsz * n_lat),
                                 _per_head_gates(ab_c, bsz * n_ctx), ev_qkv_conv[j], gparams,
                                 bsz, n_lat, n_ctx)
            gw, gb, lm = _lru_gate_weights(ev_lru_gate_w[j], ev_lru_gate_b[j], ev_lru_lambda[j])
            hr_l, hr_c = _lru_call(xr_l, xr_c, ev_lru_conv_w[j], ev_lru_conv_b[j][None, :], gw, gb, lm,
                                   bsz, n_lat, n_ctx)
            gn = ev_gdn_norm[j][None, :]
            h_lat = _out_even_call(o_l, z_l, hr_l, gate_l, h_lat, mod, i, n_lat, gn, w_out, lng0, lnb0, alpha)
            if not last:
                h_ctx = _out_even_call(o_c, z_c, hr_c, gate_c, h_ctx, mod, i, None, gn, w_out, lng0, lnb0,
                                       alpha)
        else:
            lam_init = 0.8 - 0.6 * math.exp(-0.3 * i)
            wq = od_w_qkv[j][:, :D_MODEL] * (DIFF_D ** -0.5 * math.log2(math.e))
            wk = od_w_qkv[j][:, D_MODEL:2 * D_MODEL]
            wvt = od_w_qkv[j][:, 2 * D_MODEL:].T.astype(BF16)
            qkv_w = (wq.astype(BF16), wk.astype(BF16), wvt)
            q_l, k_l, vt_l = _qkv_call(h_lat, mod, i, n_lat, qkv_w, (cos2, sin2))
            q_c, k_c, vt_c = _qkv_call(h_ctx, mod, i, None, qkv_w, None)
            subln_col = od_subln[j][:, None]
            y_l = _attn_call(q_l, [k_c, k_l], [vt_c, vt_l], od_lambda[j], subln_col, bsz, n_lat,
                             [n_ctx, n_lat], lam_init)
            h_lat = _out_odd_call(y_l, h_lat, mod, i, n_lat, w_out, lng0, lnb0, alpha)
            if not last:
                y_c = _attn_call(q_c, [k_c], [vt_c], od_lambda[j], subln_col, bsz, n_ctx, [n_ctx], lam_init)
                h_ctx = _out_odd_call(y_c, h_ctx, mod, i, None, w_out, lng0, lnb0, alpha)
        h_lat = _mlp_call(h_lat, mod, i, n_lat, w1, w2, lng1, lnb1, alpha)
        if not last:
            h_ctx = _mlp_call(h_ctx, mod, i, None, w1, w2, lng1, lnb1, alpha)
    return h_lat.reshape(bsz, n_lat, d)
```

```python
import functools
import math

import jax
import jax.numpy as jnp
from jax import lax
from jax.experimental import pallas as pl
from jax.experimental.pallas import tpu as pltpu

F32 = jnp.float32
BF16 = jnp.bfloat16

D_MODEL = 1024
D_FF = 4 * D_MODEL
CONV_K = 4
CONV_LEFT = 2
GDN_HEADS = 4
GDN_DK = 128
GDN_W = GDN_HEADS * GDN_DK
CHUNK = 64
LRU_W = D_MODEL - GDN_W
LRU_BLOCKS = 8
LRU_BW = LRU_W // LRU_BLOCKS
LRU_C = 8.0
DIFF_HEADS = 8
DIFF_D = 64
DIFF_DV = 128
ROPE_Q = DIFF_D // 4
GRID_W = 64
ROPE_THETA = 10000.0
NORM_EPS = 1e-6
LANES = 128
SUBLANES = 8
MOD_ROWS = 16
VMEM_LIMIT = 56 * 1024 * 1024


def _cparams(sem):
    return pltpu.CompilerParams(dimension_semantics=sem, vmem_limit_bytes=VMEM_LIMIT)


def _sigmoid(x):
    return 0.5 * jnp.tanh(0.5 * x) + 0.5


def _silu(x):
    return x * _sigmoid(x)


def _softplus(x):
    return jnp.maximum(x, 0.0) + jnp.log1p(jnp.exp(-jnp.abs(x)))


def _gelu_tanh(x):
    return 0.5 * x * (1.0 + jnp.tanh(math.sqrt(2.0 / math.pi) * (x + 0.044715 * (x * x * x))))


def _modulate(h, mod_ref, shift_row, scale_row):
    return h * (1.0 + mod_ref[scale_row:scale_row + 1, :]) + mod_ref[shift_row:shift_row + 1, :]


def _layernorm(x, g, b):
    mu = jnp.mean(x, axis=-1, keepdims=True)
    xc = x - mu
    var = jnp.mean(xc * xc, axis=-1, keepdims=True)
    return xc * lax.rsqrt(var + NORM_EPS) * g + b


def _dot(a, b):
    return jnp.dot(a, b, preferred_element_type=F32)


def _dot_nt(a, b):
    return lax.dot_general(a, b, (((1,), (1,)), ((), ())), preferred_element_type=F32)


def _dot_tn(a, b):
    return lax.dot_general(a, b, (((0,), (0,)), ((), ())), preferred_element_type=F32)


def _ada_kernel(c_ref, w_ref, b_ref, o_ref):
    s = _silu(c_ref[...])
    o_ref[...] = jnp.dot(s, w_ref[...], preferred_element_type=F32,
                         precision=lax.Precision.HIGHEST) + b_ref[...]


def _ada_call(cc, ada_w, ada_b):
    depth, d, n = ada_w.shape
    tn = D_MODEL
    return pl.pallas_call(
        _ada_kernel,
        grid=(depth, n // tn),
        in_specs=[pl.BlockSpec((MOD_ROWS, d), lambda i, j: (0, 0)),
                  pl.BlockSpec((None, d, tn), lambda i, j: (i, 0, j)),
                  pl.BlockSpec((None, 1, tn), lambda i, j: (i, 0, j))],
        out_specs=pl.BlockSpec((None, MOD_ROWS, tn), lambda i, j: (i, 0, j)),
        out_shape=jax.ShapeDtypeStruct((depth, MOD_ROWS, n), F32),
        compiler_params=_cparams(("parallel", "parallel")),
    )(cc, ada_w, ada_b.reshape(depth, 1, n))


def _mod_spec(layer, rows_per_mod):
    if rows_per_mod is None:
        return pl.BlockSpec((None, None, 6, D_MODEL), lambda t: (layer, 0, 0, 0))
    return pl.BlockSpec((None, None, 6, D_MODEL), lambda t: (layer, 1 + t // rows_per_mod, 0, 0))


def _const_spec(shape):
    nd = len(shape)
    return pl.BlockSpec(shape, lambda t: (0,) * nd)


def _token_tile(n_rows, per_batch):
    tm = min(512, n_rows if per_batch is None else per_batch)
    assert n_rows % tm == 0 and (per_batch is None or per_batch % tm == 0)
    return tm


def _inproj_even_kernel(h_ref, mod_ref, w_ref, wab_ref, qkv_ref, z_ref, xr_ref, gate_ref, ab_ref):
    u = _modulate(h_ref[...], mod_ref, 0, 1).astype(BF16)
    qkv_ref[...] = _dot(u, w_ref[:, 0:3 * GDN_W])
    z_ref[...] = _dot(u, w_ref[:, 3 * GDN_W:4 * GDN_W])
    xr_ref[...] = _dot(u, w_ref[:, 4 * GDN_W:4 * GDN_W + LRU_W])
    gate_ref[...] = _dot(u, w_ref[:, 4 * GDN_W + LRU_W:])
    ab_ref[...] = _dot(u, wab_ref[...])[:, 0:4 * GDN_HEADS]


def _inproj_even_call(h, mod, layer, per_batch, w_main, w_ab):
    n = h.shape[0]
    tm = _token_tile(n, per_batch)
    rpm = None if per_batch is None else per_batch // tm
    tok = lambda w: pl.BlockSpec((tm, w), lambda t: (t, 0))
    return pl.pallas_call(
        _inproj_even_kernel,
        grid=(n // tm,),
        in_specs=[tok(D_MODEL), _mod_spec(layer, rpm), _const_spec(w_main.shape), _const_spec(w_ab.shape)],
        out_specs=[tok(3 * GDN_W), tok(GDN_W), tok(LRU_W), tok(LRU_W), tok(4 * GDN_HEADS)],
        out_shape=[jax.ShapeDtypeStruct((n, 3 * GDN_W), F32), jax.ShapeDtypeStruct((n, GDN_W), F32),
                   jax.ShapeDtypeStruct((n, LRU_W), F32), jax.ShapeDtypeStruct((n, LRU_W), F32),
                   jax.ShapeDtypeStruct((n, 4 * GDN_HEADS), F32)],
        compiler_params=_cparams(("parallel",)),
    )(h, mod, w_main, w_ab)


def _dwconv(x_ref, w_ref, pad_ref, n):
    zeros = jnp.zeros((SUBLANES, LANES), F32)
    pad_ref[0:SUBLANES, :] = zeros
    pad_ref[SUBLANES + n:2 * SUBLANES + n, :] = zeros
    pad_ref[SUBLANES:SUBLANES + n, :] = x_ref[...]
    y = None
    for j in range(CONV_K):
        off = SUBLANES + j - CONV_LEFT
        term = pad_ref[off:off + n, :] * w_ref[j:j + 1, :]
        y = term if y is None else y + term
    return y


def _hp_parts(x):
    hi = x.astype(BF16).astype(F32)
    return hi, x - hi


def _mm_hp(x, y):
    xh, xl = _hp_parts(x)
    yh, yl = _hp_parts(y)
    x2 = xh + pltpu.roll(xl, CHUNK, 1)
    lhs = jnp.concatenate([x2, x2], axis=1).astype(BF16)
    rhs = jnp.concatenate([yh, yh, yl, yl], axis=0).astype(BF16)
    return _dot(lhs, rhs)


def _unit_tri_inverse(a, eye):
    p = eye - a
    q = _mm_hp(a, a)
    sq = 2
    while sq * 2 < CHUNK:
        r = _mm_hp(jnp.concatenate([p, q], axis=0), q)
        p = p + r[0:CHUNK]
        q = r[CHUNK:2 * CHUNK]
        sq *= 2
    return p + _mm_hp(p, q)


def _gdn_kernel(ql_ref, kl_ref, vl_ref, qc_ref, kc_ref, vc_ref, wq_ref, wk_ref, wv_ref,
                abl_ref, abc_ref, gp_ref, ol_ref, oc_ref,
                pad_s, q_s, k_s, v_s, ab_s, mt_s, nt_s, qe_s, gl_s, st_s, *, n_lat, n_ctx):
    n_tot = n_ctx + n_lat

    def prep(x_ref, w_ref, n, dst, base, kind):
        y = _silu(_dwconv(x_ref, w_ref, pad_s, n))
        if kind != "v":
            y = y * lax.rsqrt(jnp.sum(y * y, axis=-1, keepdims=True) + NORM_EPS)
        if kind == "q":
            y = y * (GDN_DK ** -0.5)
        dst[base:base + n, :] = y

    prep(qc_ref, wq_ref, n_ctx, q_s, 0, "q")
    prep(kc_ref, wk_ref, n_ctx, k_s, 0, "k")
    prep(vc_ref, wv_ref, n_ctx, v_s, 0, "v")
    prep(ql_ref, wq_ref, n_lat, q_s, n_ctx, "q")
    prep(kl_ref, wk_ref, n_lat, k_s, n_ctx, "k")
    prep(vl_ref, wv_ref, n_lat, v_s, n_ctx, "v")
    ab_s[0:n_ctx, :] = abc_ref[...]
    ab_s[n_ctx:n_tot, :] = abl_ref[...]

    row = lax.broadcasted_iota(jnp.int32, (CHUNK, LANES), 0)
    col = lax.broadcasted_iota(jnp.int32, (CHUNK, LANES), 1)
    left = col < CHUNK
    eye = jnp.where(row == col, 1.0, 0.0).astype(F32)
    incl = [jnp.logical_and(row >= col, left), jnp.logical_and(row <= col, left)]
    strict = [jnp.logical_and(row > col, left), jnp.logical_and(row < col, left)]
    ones_left = jnp.where(left, 1.0, 0.0).astype(F32)
    gp = gp_ref[...]

    def chunk_group(chunk0, out_ref, group, gi):
        chains = []
        for g in range(group):
            lc = gi * group + g
            c = chunk0 + lc
            rows = pl.ds(pl.multiple_of(c * CHUNK, CHUNK), CHUNK)
            orow = pl.ds(pl.multiple_of(lc * CHUNK, CHUNK), CHUNK)
            q_c = q_s[rows, :]
            k_c = k_s[rows, :]
            v_c = v_s[rows, :]
            ab_c = ab_s[rows, :]
            kb = k_c.astype(BF16)
            k_pad = jnp.concatenate([kb, jnp.zeros_like(kb)], axis=0)
            qk_kk = _dot_nt(jnp.concatenate([q_c.astype(BF16), kb], axis=0), k_pad)
            for d in range(2):
                chains.append(dict(c=c, rows=rows, orow=orow, d=d, q=q_c, k=k_c, v=v_c, ab=ab_c,
                                   qk_raw=qk_kk[0:CHUNK], kk=qk_kk[CHUNK:2 * CHUNK]))
        for ch in chains:
            d = ch["d"]
            a_col = ch["ab"][:, d:d + 1]
            b_col = ch["ab"][:, 2 + d:3 + d]
            g_col = -jnp.exp(gp[:, d:d + 1]) * _softplus(a_col + gp[:, 2 + d:3 + d])
            ch["beta"] = _sigmoid(b_col)
            g_b = jnp.broadcast_to(g_col, (CHUNK, LANES))
            m_t = incl[1 - d]
            gh, gl = _hp_parts(g_b)
            cum = jnp.where(incl[d], 1.0, 0.0).astype(F32)
            lhs1 = cum + pltpu.roll(ones_left, CHUNK, 1)
            lhs = jnp.concatenate([lhs1, lhs1], axis=1).astype(BF16)
            rhs = jnp.concatenate([gh, jnp.where(m_t, -gh, 0.0), gl, jnp.where(m_t, -gl, 0.0)],
                                  axis=0).astype(BF16)
            ch["diff"] = _dot(lhs, rhs)
            ch["tot"] = jnp.sum(g_b, axis=0, keepdims=True)
        for ch in chains:
            d = ch["d"]
            ch["decay"] = jnp.exp(jnp.where(incl[d], ch["diff"], -1e30))
            a_mat = jnp.where(strict[d], ch["beta"] * ch["kk"] * ch["decay"], 0.0)
            ch["p"] = eye - a_mat
            ch["a"] = a_mat
        for ch in chains:
            ch["q2"] = _mm_hp(ch["a"], ch["a"])
        sq = 2
        while sq * 2 < CHUNK:
            for ch in chains:
                r = _mm_hp(jnp.concatenate([ch["p"], ch["q2"]], axis=0), ch["q2"])
                ch["p"] = ch["p"] + r[0:CHUNK]
                ch["q2"] = r[CHUNK:2 * CHUNK]
            sq *= 2
        for ch in chains:
            ch["t"] = ch["p"] + _mm_hp(ch["p"], ch["q2"])
        for ch in chains:
            gam = jnp.where(left, pltpu.roll(ch["diff"], CHUNK, 1), ch["diff"])
            egam = jnp.exp(gam)
            beta = ch["beta"]
            rhs_uw = jnp.concatenate([ch["k"] * (beta * egam), ch["v"] * beta], axis=1).astype(BF16)
            ch["wu"] = _dot(ch["t"][:, 0:CHUNK].astype(BF16), rhs_uw).astype(BF16)
            ch["qg"] = ch["q"] * egam
            ch["kd"] = (ch["k"] * jnp.exp(ch["tot"] - gam)).astype(BF16)
        for ch in chains:
            d, c = ch["d"], ch["c"]
            qk_m = (ch["qk_raw"] * ch["decay"])[:, 0:CHUNK].astype(BF16)
            x = _dot(qk_m, ch["wu"])
            mn = _dot_tn(ch["wu"], ch["kd"])
            qe_s[d, ch["rows"], :] = (ch["qg"] - x[:, 0:LANES]).astype(BF16)
            out_ref[ch["orow"], :] = out_ref[ch["orow"], :] + x[:, LANES:2 * LANES]
            mt_s[d, c] = mn[0:LANES].astype(BF16)
            nt_s[d, c] = mn[LANES:2 * LANES]
            gl_s[d, c] = jnp.broadcast_to(jnp.exp(ch["tot"]), (SUBLANES, LANES))

    def prep_segment(chunk0, n_chunks, out_ref):
        group = max(g for g in (8, 4, 2, 1) if n_chunks % g == 0)

        def prep(gi, carry):
            chunk_group(chunk0, out_ref, group, gi)
            return carry
        lax.fori_loop(0, n_chunks // group, prep, 0)

    def rec_segment(chunk0, n_chunks, out_ref):
        def step(s, carry):
            for d in range(2):
                lc = s if d == 0 else n_chunks - 1 - s
                c = chunk0 + lc
                rows = pl.ds(pl.multiple_of(c * CHUNK, CHUNK), CHUNK)
                orow = pl.ds(pl.multiple_of(lc * CHUNK, CHUNK), CHUNK)
                st = st_s[d]
                stb = st.astype(BF16)
                r = _dot(stb, mt_s[d, c])
                out_ref[orow, :] = out_ref[orow, :] + _dot_nt(qe_s[d, rows, :], stb)
                st_s[d] = st * gl_s[d, c][0:1, :] + nt_s[d, c] - r
            return carry
        lax.fori_loop(0, n_chunks, step, 0)

    st_s[...] = jnp.zeros_like(st_s)
    ol_ref[...] = jnp.zeros_like(ol_ref)
    oc_ref[...] = jnp.zeros_like(oc_ref)
    prep_segment(0, n_ctx // CHUNK, oc_ref)
    prep_segment(n_ctx // CHUNK, n_lat // CHUNK, ol_ref)
    rec_segment(0, n_ctx // CHUNK, oc_ref)
    rec_segment(n_ctx // CHUNK, n_lat // CHUNK, ol_ref)


def _gdn_call(qkv_l, qkv_c, ab_l, ab_c, conv_w, gparams, bsz, n_lat, n_ctx):
    n_tot = n_lat + n_ctx
    nh = GDN_HEADS
    lat = lambda off: pl.BlockSpec((n_lat, LANES), lambda b, h: (b, off + h))
    ctx = lambda off: pl.BlockSpec((n_ctx, LANES), lambda b, h: (b, off + h))
    cw = lambda off: pl.BlockSpec((CONV_K, LANES), lambda b, h: (0, off + h))
    kern = functools.partial(_gdn_kernel, n_lat=n_lat, n_ctx=n_ctx)
    return pl.pallas_call(
        kern,
        grid=(bsz, nh),
        in_specs=[lat(0), lat(nh), lat(2 * nh), ctx(0), ctx(nh), ctx(2 * nh), cw(0), cw(nh), cw(2 * nh),
                  pl.BlockSpec((None, n_lat, 4), lambda b, h: (h, b, 0)),
                  pl.BlockSpec((None, n_ctx, 4), lambda b, h: (h, b, 0)),
                  pl.BlockSpec((None, 1, 4), lambda b, h: (h, 0, 0))],
        out_specs=[pl.BlockSpec((n_lat, LANES), lambda b, h: (b, h)),
                   pl.BlockSpec((n_ctx, LANES), lambda b, h: (b, h))],
        out_shape=[jax.ShapeDtypeStruct((bsz * n_lat, GDN_W), F32),
                   jax.ShapeDtypeStruct((bsz * n_ctx, GDN_W), F32)],
        scratch_shapes=[
            pltpu.VMEM((n_lat + 2 * SUBLANES, LANES), F32),
            pltpu.VMEM((n_tot, LANES), F32),
            pltpu.VMEM((n_tot, LANES), F32),
            pltpu.VMEM((n_tot, LANES), F32),
            pltpu.VMEM((n_tot, 4), F32),
            pltpu.VMEM((2, n_tot // CHUNK, GDN_DK, LANES), BF16),
            pltpu.VMEM((2, n_tot // CHUNK, LANES, GDN_DK), F32),
            pltpu.VMEM((2, n_tot, LANES), BF16),
            pltpu.VMEM((2, n_tot // CHUNK, SUBLANES, LANES), F32),
            pltpu.VMEM((2, GDN_DK, LANES), F32),
        ],
        compiler_params=_cparams(("parallel", "arbitrary")),
    )(qkv_l, qkv_l, qkv_l, qkv_c, qkv_c, qkv_c, conv_w, conv_w, conv_w, ab_l, ab_c, gparams)


def _lru_kernel(xl_ref, xc_ref, cw_ref, cb_ref, gw_ref, gb_ref, lam_ref, hl_ref, hc_ref,
                pad_s, x_s, a_s, u_s, *, n_lat, n_ctx):
    n_tot = n_ctx + n_lat
    x_s[0:n_ctx, :] = _dwconv(xc_ref, cw_ref, pad_s, n_ctx) + cb_ref[...]
    x_s[n_ctx:n_tot, :] = _dwconv(xl_ref, cw_ref, pad_s, n_lat) + cb_ref[...]

    sp = [_softplus(-lam_ref[:, d * LANES:(d + 1) * LANES]) for d in range(2)]
    tile = min(256, n_ctx)

    def coeffs(t, carry):
        rows = pl.ds(pl.multiple_of(t * tile, tile), tile)
        x = x_s[rows, :]
        gates = _dot(x.astype(BF16), gw_ref[...]) + gb_ref[...]
        for d in range(2):
            r = _sigmoid(gates[:, (2 * d) * LANES:(2 * d + 1) * LANES])
            i = _sigmoid(gates[:, (2 * d + 1) * LANES:(2 * d + 2) * LANES])
            a = jnp.exp(-LRU_C * r * sp[d])
            a_s[d, rows, :] = a
            u_s[d, rows, :] = jnp.sqrt(1.0 - a * a) * (i * x)
        return carry

    lax.fori_loop(0, n_tot // tile, coeffs, 0)

    sub = lax.broadcasted_iota(jnp.int32, (SUBLANES, LANES), 0)

    def group_scan(d, g0):
        rows = pl.ds(pl.multiple_of(g0, SUBLANES), SUBLANES)
        a = a_s[d, rows, :]
        u = u_s[d, rows, :]
        sh = 1
        while sh < SUBLANES:
            if d == 0:
                a_sh, u_sh, ok = pltpu.roll(a, sh, 0), pltpu.roll(u, sh, 0), sub >= sh
            else:
                a_sh, u_sh = pltpu.roll(a, SUBLANES - sh, 0), pltpu.roll(u, SUBLANES - sh, 0)
                ok = sub < SUBLANES - sh
            u = u + a * jnp.where(ok, u_sh, 0.0)
            a = a * jnp.where(ok, a_sh, 1.0)
            sh *= 2
        return rows, u, a

    def fwd(base, n, h0):
        def body(g, h):
            rows, u, a = group_scan(0, base + g * SUBLANES)
            hh = u + a * h
            u_s[0, rows, :] = hh
            return jnp.broadcast_to(hh[SUBLANES - 1:SUBLANES, :], (SUBLANES, LANES))
        return lax.fori_loop(0, n // SUBLANES, body, h0, unroll=4)

    def bwd(base, n, h0, out_ref):
        def body(g, h):
            lg = n // SUBLANES - 1 - g
            rows, u, a = group_scan(1, base + lg * SUBLANES)
            hh = u + a * h
            orow = pl.ds(pl.multiple_of(lg * SUBLANES, SUBLANES), SUBLANES)
            out_ref[orow, :] = u_s[0, rows, :] + hh
            return jnp.broadcast_to(hh[0:1, :], (SUBLANES, LANES))
        return lax.fori_loop(0, n // SUBLANES, body, h0, unroll=4)

    zero = jnp.zeros((SUBLANES, LANES), F32)
    h = fwd(0, n_ctx, zero)
    fwd(n_ctx, n_lat, h)
    h = bwd(0, n_ctx, zero, hc_ref)
    bwd(n_ctx, n_lat, h, hl_ref)


def _lru_call(xr_l, xr_c, conv_w, conv_b, gate_w, gate_b, lam, bsz, n_lat, n_ctx):
    n_tot = n_lat + n_ctx
    nt = LRU_W // LANES
    kern = functools.partial(_lru_kernel, n_lat=n_lat, n_ctx=n_ctx)
    return pl.pallas_call(
        kern,
        grid=(bsz, nt),
        in_specs=[pl.BlockSpec((n_lat, LANES), lambda b, j: (b, j)),
                  pl.BlockSpec((n_ctx, LANES), lambda b, j: (b, j)),
                  pl.BlockSpec((CONV_K, LANES), lambda b, j: (0, j)),
                  pl.BlockSpec((1, LANES), lambda b, j: (0, j)),
                  pl.BlockSpec((None, LANES, 4 * LANES), lambda b, j: (j, 0, 0)),
                  pl.BlockSpec((None, 1, 4 * LANES), lambda b, j: (j, 0, 0)),
                  pl.BlockSpec((None, 1, 2 * LANES), lambda b, j: (j, 0, 0))],
        out_specs=[pl.BlockSpec((n_lat, LANES), lambda b, j: (b, j)),
                   pl.BlockSpec((n_ctx, LANES), lambda b, j: (b, j))],
        out_shape=[jax.ShapeDtypeStruct((bsz * n_lat, LRU_W), F32),
                   jax.ShapeDtypeStruct((bsz * n_ctx, LRU_W), F32)],
        scratch_shapes=[pltpu.VMEM((n_lat + 2 * SUBLANES, LANES), F32),
                        pltpu.VMEM((n_tot, LANES), F32),
                        pltpu.VMEM((2, n_tot, LANES), F32),
                        pltpu.VMEM((2, n_tot, LANES), F32)],
        compiler_params=_cparams(("parallel", "arbitrary")),
    )(xr_l, xr_c, conv_w, conv_b, gate_w, gate_b, lam)


def _residual_ln(h, y, mod_ref, gate_row, lng_ref, lnb_ref, alpha):
    return _layernorm(alpha * h + mod_ref[gate_row:gate_row + 1, :] * y, lng_ref[...], lnb_ref[...])


def _out_even_kernel(o_ref, z_ref, hr_ref, gate_ref, h_ref, mod_ref, gn_ref, w_ref, lng_ref, lnb_ref,
                     out_ref, *, alpha):
    parts = []
    for hd in range(GDN_HEADS):
        sl = slice(hd * LANES, (hd + 1) * LANES)
        o = o_ref[:, sl]
        on = o * lax.rsqrt(jnp.mean(o * o, axis=-1, keepdims=True) + NORM_EPS) * gn_ref[...]
        parts.append(on * _silu(z_ref[:, sl]))
    y_gdn = jnp.concatenate(parts, axis=1).astype(BF16)
    y_lru = (hr_ref[...] * _gelu_tanh(gate_ref[...])).astype(BF16)
    y = _dot(y_gdn, w_ref[0:GDN_W, :]) + _dot(y_lru, w_ref[GDN_W:, :])
    out_ref[...] = _residual_ln(h_ref[...], y, mod_ref, 2, lng_ref, lnb_ref, alpha)


def _out_odd_kernel(y_ref, h_ref, mod_ref, w_ref, lng_ref, lnb_ref, out_ref, *, alpha):
    y = _dot(y_ref[...], w_ref[...])
    out_ref[...] = _residual_ln(h_ref[...], y, mod_ref, 2, lng_ref, lnb_ref, alpha)


def _out_even_call(o, z, hr, gate, h, mod, layer, per_batch, gn, w_out, ln_g, ln_b, alpha):
    n = h.shape[0]
    tm = _token_tile(n, per_batch)
    rpm = None if per_batch is None else per_batch // tm
    tok = lambda w: pl.BlockSpec((tm, w), lambda t: (t, 0))
    return pl.pallas_call(
        functools.partial(_out_even_kernel, alpha=alpha),
        grid=(n // tm,),
        in_specs=[tok(GDN_W), tok(GDN_W), tok(LRU_W), tok(LRU_W), tok(D_MODEL), _mod_spec(layer, rpm),
                  _const_spec(gn.shape), _const_spec(w_out.shape), _const_spec(ln_g.shape),
                  _const_spec(ln_b.shape)],
        out_specs=tok(D_MODEL),
        out_shape=jax.ShapeDtypeStruct((n, D_MODEL), F32),
        compiler_params=_cparams(("parallel",)),
    )(o, z, hr, gate, h, mod, gn, w_out, ln_g, ln_b)


def _out_odd_call(y, h, mod, layer, per_batch, w_out, ln_g, ln_b, alpha):
    n = h.shape[0]
    tm = _token_tile(n, per_batch)
    rpm = None if per_batch is None else per_batch // tm
    tok = lambda w: pl.BlockSpec((tm, w), lambda t: (t, 0))
    return pl.pallas_call(
        functools.partial(_out_odd_kernel, alpha=alpha),
        grid=(n // tm,),
        in_specs=[tok(D_MODEL), tok(D_MODEL), _mod_spec(layer, rpm), _const_spec(w_out.shape),
                  _const_spec(ln_g.shape), _const_spec(ln_b.shape)],
        out_specs=tok(D_MODEL),
        out_shape=jax.ShapeDtypeStruct((n, D_MODEL), F32),
        compiler_params=_cparams(("parallel",)),
    )(y, h, mod, w_out, ln_g, ln_b)


FF_TILE = 1024


def _mlp_kernel(h_ref, mod_ref, w1_ref, w2_ref, lng_ref, lnb_ref, out_ref, *, alpha):
    h = h_ref[...]
    u = _modulate(h, mod_ref, 3, 4).astype(BF16)
    acc = None
    for j in range(D_FF // FF_TILE):
        sl = slice(j * FF_TILE, (j + 1) * FF_TILE)
        a = jnp.maximum(_dot(u, w1_ref[:, sl]), 0.0)
        part = _dot((a * a).astype(BF16), w2_ref[sl, :])
        acc = part if acc is None else acc + part
    out_ref[...] = _residual_ln(h, acc, mod_ref, 5, lng_ref, lnb_ref, alpha)


def _mlp_call(h, mod, layer, per_batch, w1, w2, ln_g, ln_b, alpha):
    n = h.shape[0]
    tm = _token_tile(n, per_batch)
    rpm = None if per_batch is None else per_batch // tm
    tok = pl.BlockSpec((tm, D_MODEL), lambda t: (t, 0))
    return pl.pallas_call(
        functools.partial(_mlp_kernel, alpha=alpha),
        grid=(n // tm,),
        in_specs=[tok, _mod_spec(layer, rpm), _const_spec(w1.shape), _const_spec(w2.shape),
                  _const_spec(ln_g.shape), _const_spec(ln_b.shape)],
        out_specs=tok,
        out_shape=jax.ShapeDtypeStruct((n, D_MODEL), F32),
        compiler_params=_cparams(("parallel",)),
    )(h, mod, w1, w2, ln_g, ln_b)


def _qkv_rope_kernel(h_ref, mod_ref, wq_ref, wk_ref, wvt_ref, cos_ref, sin_ref, q_ref, k_ref, vt_ref):
    u = _modulate(h_ref[...], mod_ref, 0, 1).astype(BF16)
    cos = cos_ref[...]
    sin_signed = sin_ref[...]
    lane = lax.broadcasted_iota(jnp.int32, cos.shape, 1)
    partner_above = (lane % (2 * ROPE_Q)) < ROPE_Q

    def rope(y):
        parts = []
        for hd in range(DIFF_HEADS):
            t = y[:, hd * LANES:(hd + 1) * LANES]
            partner = jnp.where(partner_above, pltpu.roll(t, LANES - ROPE_Q, 1), pltpu.roll(t, ROPE_Q, 1))
            parts.append(t * cos + partner * sin_signed)
        return jnp.concatenate(parts, axis=1).astype(BF16)

    q_ref[...] = rope(_dot(u, wq_ref[...]))
    k_ref[...] = rope(_dot(u, wk_ref[...]))
    vt_ref[...] = _dot_nt(wvt_ref[...], u).astype(BF16)


def _qkv_plain_kernel(h_ref, mod_ref, wq_ref, wk_ref, wvt_ref, q_ref, k_ref, vt_ref):
    u = _modulate(h_ref[...], mod_ref, 0, 1).astype(BF16)
    q_ref[...] = _dot(u, wq_ref[...]).astype(BF16)
    k_ref[...] = _dot(u, wk_ref[...]).astype(BF16)
    vt_ref[...] = _dot_nt(wvt_ref[...], u).astype(BF16)


def _qkv_call(h, mod, layer, per_batch, weights, rope):
    n = h.shape[0]
    tm = _token_tile(n, per_batch)
    tok = pl.BlockSpec((tm, D_MODEL), lambda t: (t, 0))
    out_specs = [tok, tok, pl.BlockSpec((D_MODEL, tm), lambda t: (0, t))]
    out_shape = [jax.ShapeDtypeStruct((n, D_MODEL), BF16), jax.ShapeDtypeStruct((n, D_MODEL), BF16),
                 jax.ShapeDtypeStruct((D_MODEL, n), BF16)]
    wspec = _const_spec((D_MODEL, D_MODEL))
    if rope is not None:
        cos2, sin2 = rope
        rpm = per_batch // tm
        tab = pl.BlockSpec((tm, LANES), lambda t: (t % rpm, 0))
        wq, wk, wvt = weights
        return pl.pallas_call(
            _qkv_rope_kernel, grid=(n // tm,),
            in_specs=[tok, _mod_spec(layer, rpm), wspec, wspec, wspec, tab, tab],
            out_specs=out_specs, out_shape=out_shape,
            compiler_params=_cparams(("parallel",)),
        )(h, mod, wq, wk, wvt, cos2, sin2)
    wq, wk, wvt = weights
    return pl.pallas_call(
        _qkv_plain_kernel, grid=(n // tm,),
        in_specs=[tok, _mod_spec(layer, None), wspec, wspec, wspec],
        out_specs=out_specs, out_shape=out_shape,
        compiler_params=_cparams(("parallel",)),
    )(h, mod, wq, wk, wvt)


KEY_TILE = 512
SCORE_LOOKAHEAD = 2
Q_TILE = 1024


def _attn_kernel(*refs, key_lens, lam_init):
    nseg = len(key_lens)
    q_ref = refs[0]
    k_refs = refs[1:1 + nseg]
    vt_refs = refs[1 + nseg:1 + 2 * nseg]
    lam_ref, subln_ref, y_ref, acc_s = refs[1 + 2 * nseg:]
    tq = q_ref.shape[0]

    q = q_ref[...]
    lane = lax.broadcasted_iota(jnp.int32, q.shape, 1)
    zero = jnp.zeros_like(q)
    qm = [jnp.where(lane < DIFF_D, q, zero), jnp.where(lane >= DIFF_D, q, zero)]
    acc_s[...] = jnp.zeros_like(acc_s)

    tiles = []
    for k_ref, vt_ref, n_keys in zip(k_refs, vt_refs, key_lens):
        tk = min(KEY_TILE, n_keys)
        tiles += [(k_ref, vt_ref, j * tk, tk) for j in range(n_keys // tk)]

    def tile_scores(t):
        k_ref, _, k0, tk = tiles[t]
        k_t = k_ref[k0:k0 + tk, :]
        return [_dot_nt(k_t, qm[0]), _dot_nt(k_t, qm[1])]

    def absorb(s_pair, t, stats):
        _, vt_ref, k0, tk = tiles[t]
        vt_t = vt_ref[:, k0:k0 + tk]
        new = []
        for m in range(2):
            m_old, l_old = stats[m]
            s_t = s_pair[m]
            m_new = jnp.maximum(m_old, jnp.max(s_t, axis=0, keepdims=True))
            alpha = jnp.exp2(m_old - m_new)
            p = jnp.exp2(s_t - m_new)
            l_new = alpha * l_old + jnp.sum(p, axis=0, keepdims=True)
            acc_s[m] = alpha * acc_s[m] + _dot(vt_t, p.astype(BF16))
            new.append((m_new, l_new))
        return tuple(new)

    init = (jnp.full((1, tq), -1e30, F32), jnp.zeros((1, tq), F32))
    stats = (init, init)
    ahead = [tile_scores(t) for t in range(min(SCORE_LOOKAHEAD, len(tiles)))]
    for t in range(len(tiles)):
        if t + SCORE_LOOKAHEAD < len(tiles):
            ahead.append(tile_scores(t + SCORE_LOOKAHEAD))
        stats = absorb(ahead.pop(0), t, stats)

    lv = lam_ref[...]
    lam = (jnp.exp(jnp.sum(lv[0:1] * lv[1:2], axis=1, keepdims=True))
           - jnp.exp(jnp.sum(lv[2:3] * lv[3:4], axis=1, keepdims=True)) + lam_init)
    o_t = acc_s[0] / stats[0][1] - lam * (acc_s[1] / stats[1][1])
    ms = jnp.mean(o_t * o_t, axis=0, keepdims=True)
    y_t = o_t * lax.rsqrt(ms + NORM_EPS) * subln_ref[...] * (1.0 - lam_init)
    y_ref[...] = y_t.T.astype(BF16)


def _attn_call(q, ks, vts, lam_vec, subln_col, bsz, n_q, key_lens, lam_init):
    tq = min(Q_TILE, n_q)
    nq = n_q // tq
    nseg = len(key_lens)
    in_specs = [pl.BlockSpec((tq, LANES), lambda b, h, i: (b * nq + i, h))]
    in_specs += [pl.BlockSpec((n, LANES), lambda b, h, i: (b, h)) for n in key_lens]
    in_specs += [pl.BlockSpec((LANES, n), lambda b, h, i: (h, b)) for n in key_lens]
    in_specs += [pl.BlockSpec((4, DIFF_D), lambda b, h, i: (0, 0)),
                 pl.BlockSpec((DIFF_DV, 1), lambda b, h, i: (0, 0))]
    kern = functools.partial(_attn_kernel, key_lens=tuple(key_lens), lam_init=lam_init)
    return pl.pallas_call(
        kern,
        grid=(bsz, DIFF_HEADS, nq),
        in_specs=in_specs,
        out_specs=pl.BlockSpec((tq, LANES), lambda b, h, i: (b * nq + i, h)),
        out_shape=jax.ShapeDtypeStruct((bsz * n_q, D_MODEL), BF16),
        scratch_shapes=[pltpu.VMEM((2, DIFF_DV, tq), F32)],
        compiler_params=_cparams(("parallel", "parallel", "arbitrary")),
    )(q, *ks, *vts, lam_vec, subln_col)


def _rope_tables(n_lat):
    rows = n_lat // GRID_W
    row = jnp.repeat(jnp.arange(rows), GRID_W).astype(F32)
    col = jnp.tile(jnp.arange(GRID_W), rows).astype(F32)
    half = DIFF_D // 2
    inv = ROPE_THETA ** (-(jnp.arange(0, half, 2, dtype=F32) / half))
    ang_r = row[:, None] * inv
    ang_c = col[:, None] * inv
    ang = jnp.concatenate([ang_r, ang_r, ang_c, ang_c], axis=-1)
    cos, sin = jnp.cos(ang), jnp.sin(ang)
    sign = jnp.where((jnp.arange(DIFF_D) % (2 * ROPE_Q)) < ROPE_Q, -1.0, 1.0).astype(F32)
    sin_signed = sin * sign
    return jnp.concatenate([cos, cos], axis=-1), jnp.concatenate([sin_signed, sin_signed], axis=-1)


def _lru_gate_weights(gate_w, gate_b, lam):
    nt = LRU_W // LANES
    per = LANES // LRU_BW
    blocks = gate_w.reshape(2, 2, nt, per, LRU_BW, LRU_BW)
    eye = jnp.eye(per, dtype=gate_w.dtype)
    dense = jnp.einsum('dgtpab,pq->dgtpaqb', blocks, eye).reshape(2, 2, nt, LANES, LANES)
    w = jnp.transpose(dense, (2, 3, 0, 1, 4)).reshape(nt, LANES, 4 * LANES)
    b = jnp.transpose(gate_b.reshape(2, 2, nt, LANES), (2, 0, 1, 3)).reshape(nt, 1, 4 * LANES)
    lm = jnp.transpose(lam.reshape(2, nt, LANES), (1, 0, 2)).reshape(nt, 1, 2 * LANES)
    return w.astype(BF16), b, lm


def _per_head_gates(ab, n_rows):
    t = ab.reshape(n_rows, 2, 2, GDN_HEADS)
    return jnp.transpose(t, (3, 0, 1, 2)).reshape(GDN_HEADS, n_rows, 4)


def kernel(x, c, ctx, c_ctx, ada_w, ada_b, ln_g, ln_b, mlp_w1, mlp_w2, mix_w_out, ev_w_in, ev_qkv_conv,
           ev_a_log, ev_dt_bias, ev_gdn_norm, ev_lru_conv_w, ev_lru_conv_b, ev_lru_gate_w, ev_lru_gate_b,
           ev_lru_lambda, od_w_qkv, od_lambda, od_subln):
    bsz, n_lat, d = x.shape
    n_ctx = ctx.shape[1]
    depth = ada_w.shape[0]
    assert d == D_MODEL and bsz + 1 <= MOD_ROWS
    assert n_lat % CHUNK == 0 and n_ctx % CHUNK == 0 and n_lat % GRID_W == 0
    alpha = (2 * depth) ** 0.25

    h_lat = x.reshape(bsz * n_lat, d)
    h_ctx = ctx.reshape(bsz * n_ctx, d)
    cc = jnp.concatenate([c_ctx[None, :], c, jnp.zeros((MOD_ROWS - 1 - bsz, d), F32)], axis=0)
    mod = _ada_call(cc, ada_w, ada_b).reshape(depth, MOD_ROWS, 6, d)
    cos2, sin2 = _rope_tables(n_lat)

    for i in range(depth):
        last = i == depth - 1
        j = i // 2
        w_out = mix_w_out[i].astype(BF16)
        lng0, lnb0 = ln_g[i, 0][None, :], ln_b[i, 0][None, :]
        lng1, lnb1 = ln_g[i, 1][None, :], ln_b[i, 1][None, :]
        w1 = mlp_w1[i].astype(BF16)
        w2 = mlp_w2[i].astype(BF16)
        if i % 2 == 0:
            w_in = ev_w_in[j]
            c0, c1, c2 = 4 * GDN_W, 4 * GDN_W + 4 * GDN_HEADS, 4 * GDN_W + 4 * GDN_HEADS + 2 * LRU_W
            w_main = jnp.concatenate([w_in[:, :c0], w_in[:, c1:c2]], axis=1).astype(BF16)
            w_ab = jnp.pad(w_in[:, c0:c1], ((0, 0), (0, LANES - 4 * GDN_HEADS))).astype(BF16)
            qkv_l, z_l, xr_l, gate_l, ab_l = _inproj_even_call(h_lat, mod, i, n_lat, w_main, w_ab)
            qkv_c, z_c, xr_c, gate_c, ab_c = _inproj_even_call(h_ctx, mod, i, None, w_main, w_ab)
            gparams = jnp.stack([ev_a_log[j, 0], ev_a_log[j, 1], ev_dt_bias[j, 0], ev_dt_bias[j, 1]],
                                axis=-1).reshape(GDN_HEADS, 1, 4)
            o_l, o_c = _gdn_call(qkv_l, qkv_c, _per_head_gates(ab_l, bsz * n_lat),
                                 _per_head_gates(ab_c, bsz * n_ctx), ev_qkv_conv[j], gparams,
                                 bsz, n_lat, n_ctx)
            gw, gb, lm = _lru_gate_weights(ev_lru_gate_w[j], ev_lru_gate_b[j], ev_lru_lambda[j])
            hr_l, hr_c = _lru_call(xr_l, xr_c, ev_lru_conv_w[j], ev_lru_conv_b[j][None, :], gw, gb, lm,
                                   bsz, n_lat, n_ctx)
            gn = ev_gdn_norm[j][None, :]
            h_lat = _out_even_call(o_l, z_l, hr_l, gate_l, h_lat, mod, i, n_lat, gn, w_out, lng0, lnb0, alpha)
            if not last:
                h_ctx = _out_even_call(o_c, z_c, hr_c, gate_c, h_ctx, mod, i, None, gn, w_out, lng0, lnb0,
                                       alpha)
        else:
            lam_init = 0.8 - 0.6 * math.exp(-0.3 * i)
            wq = od_w_qkv[j][:, :D_MODEL] * (DIFF_D ** -0.5 * math.log2(math.e))
            wk = od_w_qkv[j][:, D_MODEL:2 * D_MODEL]
            wvt = od_w_qkv[j][:, 2 * D_MODEL:].T.astype(BF16)
            qkv_w = (wq.astype(BF16), wk.astype(BF16), wvt)
            q_l, k_l, vt_l = _qkv_call(h_lat, mod, i, n_lat, qkv_w, (cos2, sin2))
            q_c, k_c, vt_c = _qkv_call(h_ctx, mod, i, None, qkv_w, None)
            subln_col = od_subln[j][:, None]
            y_l = _attn_call(q_l, [k_c, k_l], [vt_c, vt_l], od_lambda[j], subln_col, bsz, n_lat,
                             [n_ctx, n_lat], lam_init)
            h_lat = _out_odd_call(y_l, h_lat, mod, i, n_lat, w_out, lng0, lnb0, alpha)
            if not last:
                y_c = _attn_call(q_c, [k_c], [vt_c], od_lambda[j], subln_col, bsz, n_ctx, [n_ctx], lam_init)
                h_ctx = _out_odd_call(y_c, h_ctx, mod, i, None, w_out, lng0, lnb0, alpha)
        h_lat = _mlp_call(h_lat, mod, i, n_lat, w1, w2, lng1, lnb1, alpha)
        if not last:
            h_ctx = _mlp_call(h_ctx, mod, i, None, w1, w2, lng1, lnb1, alpha)
    return h_lat.reshape(bsz, n_lat, d)
```

```python
import functools
import math

import jax
import jax.numpy as jnp
from jax import lax
from jax.experimental import pallas as pl
from jax.experimental.pallas import tpu as pltpu

F32 = jnp.float32
BF16 = jnp.bfloat16

D_MODEL = 1024
D_FF = 4 * D_MODEL
CONV_K = 4
CONV_LEFT = 2
GDN_HEADS = 4
GDN_DK = 128
GDN_W = GDN_HEADS * GDN_DK
CHUNK = 64
LRU_W = D_MODEL - GDN_W
LRU_BLOCKS = 8
LRU_BW = LRU_W // LRU_BLOCKS
LRU_C = 8.0
DIFF_HEADS = 8
DIFF_D = 64
DIFF_DV = 128
ROPE_Q = DIFF_D // 4
GRID_W = 64
ROPE_THETA = 10000.0
NORM_EPS = 1e-6
LANES = 128
SUBLANES = 8
MOD_ROWS = 16
VMEM_LIMIT = 56 * 1024 * 1024


def _cparams(sem):
    return pltpu.CompilerParams(dimension_semantics=sem, vmem_limit_bytes=VMEM_LIMIT)


def _sigmoid(x):
    return 0.5 * jnp.tanh(0.5 * x) + 0.5


def _silu(x):
    return x * _sigmoid(x)


def _softplus(x):
    return jnp.maximum(x, 0.0) + jnp.log1p(jnp.exp(-jnp.abs(x)))


def _gelu_tanh(x):
    return 0.5 * x * (1.0 + jnp.tanh(math.sqrt(2.0 / math.pi) * (x + 0.044715 * (x * x * x))))


def _modulate(h, mod_ref, shift_row, scale_row):
    return h * (1.0 + mod_ref[scale_row:scale_row + 1, :]) + mod_ref[shift_row:shift_row + 1, :]


def _layernorm(x, g, b):
    mu = jnp.mean(x, axis=-1, keepdims=True)
    xc = x - mu
    var = jnp.mean(xc * xc, axis=-1, keepdims=True)
    return xc * lax.rsqrt(var + NORM_EPS) * g + b


def _dot(a, b):
    return jnp.dot(a, b, preferred_element_type=F32)


def _dot_nt(a, b):
    return lax.dot_general(a, b, (((1,), (1,)), ((), ())), preferred_element_type=F32)


def _dot_tn(a, b):
    return lax.dot_general(a, b, (((0,), (0,)), ((), ())), preferred_element_type=F32)


def _ada_kernel(c_ref, w_ref, b_ref, o_ref):
    s = _silu(c_ref[...])
    o_ref[...] = jnp.dot(s, w_ref[...], preferred_element_type=F32,
                         precision=lax.Precision.HIGHEST) + b_ref[...]


def _ada_call(cc, ada_w, ada_b):
    depth, d, n = ada_w.shape
    tn = D_MODEL
    return pl.pallas_call(
        _ada_kernel,
        grid=(depth, n // tn),
        in_specs=[pl.BlockSpec((MOD_ROWS, d), lambda i, j: (0, 0)),
                  pl.BlockSpec((None, d, tn), lambda i, j: (i, 0, j)),
                  pl.BlockSpec((None, 1, tn), lambda i, j: (i, 0, j))],
        out_specs=pl.BlockSpec((None, MOD_ROWS, tn), lambda i, j: (i, 0, j)),
        out_shape=jax.ShapeDtypeStruct((depth, MOD_ROWS, n), F32),
        compiler_params=_cparams(("parallel", "parallel")),
    )(cc, ada_w, ada_b.reshape(depth, 1, n))


def _mod_spec(layer, rows_per_mod):
    if rows_per_mod is None:
        return pl.BlockSpec((None, None, 6, D_MODEL), lambda t: (layer, 0, 0, 0))
    return pl.BlockSpec((None, None, 6, D_MODEL), lambda t: (layer, 1 + t // rows_per_mod, 0, 0))


def _const_spec(shape):
    nd = len(shape)
    return pl.BlockSpec(shape, lambda t: (0,) * nd, pipeline_mode=pl.Buffered(1))


def _token_tile(n_rows, per_batch):
    tm = min(512, n_rows if per_batch is None else per_batch)
    assert n_rows % tm == 0 and (per_batch is None or per_batch % tm == 0)
    return tm


def _inproj_even_kernel(h_ref, mod_ref, w_ref, wab_ref, qkv_ref, z_ref, xr_ref, gate_ref, ab_ref):
    u = _modulate(h_ref[...], mod_ref, 0, 1).astype(BF16)
    qkv_ref[...] = _dot(u, w_ref[:, 0:3 * GDN_W])
    z_ref[...] = _dot(u, w_ref[:, 3 * GDN_W:4 * GDN_W])
    xr_ref[...] = _dot(u, w_ref[:, 4 * GDN_W:4 * GDN_W + LRU_W])
    gate_ref[...] = _dot(u, w_ref[:, 4 * GDN_W + LRU_W:])
    ab_ref[...] = _dot(u, wab_ref[...])[:, 0:4 * GDN_HEADS]


def _inproj_even_call(h, mod, layer, per_batch, w_main, w_ab):
    n = h.shape[0]
    tm = _token_tile(n, per_batch)
    rpm = None if per_batch is None else per_batch // tm
    tok = lambda w: pl.BlockSpec((tm, w), lambda t: (t, 0))
    return pl.pallas_call(
        _inproj_even_kernel,
        grid=(n // tm,),
        in_specs=[tok(D_MODEL), _mod_spec(layer, rpm), _const_spec(w_main.shape), _const_spec(w_ab.shape)],
        out_specs=[tok(3 * GDN_W), tok(GDN_W), tok(LRU_W), tok(LRU_W), tok(4 * GDN_HEADS)],
        out_shape=[jax.ShapeDtypeStruct((n, 3 * GDN_W), F32), jax.ShapeDtypeStruct((n, GDN_W), F32),
                   jax.ShapeDtypeStruct((n, LRU_W), F32), jax.ShapeDtypeStruct((n, LRU_W), F32),
                   jax.ShapeDtypeStruct((n, 4 * GDN_HEADS), F32)],
        compiler_params=_cparams(("parallel",)),
    )(h, mod, w_main, w_ab)


def _dwconv(x_ref, w_ref, pad_ref, n):
    zeros = jnp.zeros((SUBLANES, LANES), F32)
    pad_ref[0:SUBLANES, :] = zeros
    pad_ref[SUBLANES + n:2 * SUBLANES + n, :] = zeros
    pad_ref[SUBLANES:SUBLANES + n, :] = x_ref[...]
    y = None
    for j in range(CONV_K):
        off = SUBLANES + j - CONV_LEFT
        term = pad_ref[off:off + n, :] * w_ref[j:j + 1, :]
        y = term if y is None else y + term
    return y


def _hp_parts(x):
    hi = x.astype(BF16).astype(F32)
    return hi, x - hi


def _mm_hp(x, y):
    xh, xl = _hp_parts(x)
    yh, yl = _hp_parts(y)
    x2 = xh + pltpu.roll(xl, CHUNK, 1)
    lhs = jnp.concatenate([x2, x2], axis=1).astype(BF16)
    rhs = jnp.concatenate([yh, yh, yl, yl], axis=0).astype(BF16)
    return _dot(lhs, rhs)


def _unit_tri_inverse(a, eye):
    p = eye - a
    q = _mm_hp(a, a)
    sq = 2
    while sq * 2 < CHUNK:
        r = _mm_hp(jnp.concatenate([p, q], axis=0), q)
        p = p + r[0:CHUNK]
        q = r[CHUNK:2 * CHUNK]
        sq *= 2
    return p + _mm_hp(p, q)


def _gdn_kernel(ql_ref, kl_ref, vl_ref, qc_ref, kc_ref, vc_ref, wq_ref, wk_ref, wv_ref,
                abl_ref, abc_ref, gp_ref, ol_ref, oc_ref,
                pad_s, q_s, k_s, v_s, ab_s, mt_s, nt_s, qe_s, gl_s, st_s, *, n_lat, n_ctx):
    n_tot = n_ctx + n_lat

    def prep(x_ref, w_ref, n, dst, base, kind):
        y = _silu(_dwconv(x_ref, w_ref, pad_s, n))
        if kind != "v":
            y = y * lax.rsqrt(jnp.sum(y * y, axis=-1, keepdims=True) + NORM_EPS)
        if kind == "q":
            y = y * (GDN_DK ** -0.5)
        dst[base:base + n, :] = y

    prep(qc_ref, wq_ref, n_ctx, q_s, 0, "q")
    prep(kc_ref, wk_ref, n_ctx, k_s, 0, "k")
    prep(vc_ref, wv_ref, n_ctx, v_s, 0, "v")
    prep(ql_ref, wq_ref, n_lat, q_s, n_ctx, "q")
    prep(kl_ref, wk_ref, n_lat, k_s, n_ctx, "k")
    prep(vl_ref, wv_ref, n_lat, v_s, n_ctx, "v")
    ab_s[0:n_ctx, :] = abc_ref[...]
    ab_s[n_ctx:n_tot, :] = abl_ref[...]

    row = lax.broadcasted_iota(jnp.int32, (CHUNK, LANES), 0)
    col = lax.broadcasted_iota(jnp.int32, (CHUNK, LANES), 1)
    left = col < CHUNK
    eye = jnp.where(row == col, 1.0, 0.0).astype(F32)
    incl = [jnp.logical_and(row >= col, left), jnp.logical_and(row <= col, left)]
    strict = [jnp.logical_and(row > col, left), jnp.logical_and(row < col, left)]
    ones_left = jnp.where(left, 1.0, 0.0).astype(F32)
    gp = gp_ref[...]

    def chunk_group(chunk0, out_ref, group, gi):
        chains = []
        for g in range(group):
            lc = gi * group + g
            c = chunk0 + lc
            rows = pl.ds(pl.multiple_of(c * CHUNK, CHUNK), CHUNK)
            orow = pl.ds(pl.multiple_of(lc * CHUNK, CHUNK), CHUNK)
            q_c = q_s[rows, :]
            k_c = k_s[rows, :]
            v_c = v_s[rows, :]
            ab_c = ab_s[rows, :]
            kb = k_c.astype(BF16)
            k_pad = jnp.concatenate([kb, jnp.zeros_like(kb)], axis=0)
            qk_kk = _dot_nt(jnp.concatenate([q_c.astype(BF16), kb], axis=0), k_pad)
            for d in range(2):
                chains.append(dict(c=c, rows=rows, orow=orow, d=d, q=q_c, k=k_c, v=v_c, ab=ab_c,
                                   qk_raw=qk_kk[0:CHUNK], kk=qk_kk[CHUNK:2 * CHUNK]))
        for ch in chains:
            d = ch["d"]
            a_col = ch["ab"][:, d:d + 1]
            b_col = ch["ab"][:, 2 + d:3 + d]
            g_col = -jnp.exp(gp[:, d:d + 1]) * _softplus(a_col + gp[:, 2 + d:3 + d])
            ch["beta"] = _sigmoid(b_col)
            g_b = jnp.broadcast_to(g_col, (CHUNK, LANES))
            m_t = incl[1 - d]
            gh, gl = _hp_parts(g_b)
            cum = jnp.where(incl[d], 1.0, 0.0).astype(F32)
            lhs1 = cum + pltpu.roll(ones_left, CHUNK, 1)
            lhs = jnp.concatenate([lhs1, lhs1], axis=1).astype(BF16)
            rhs = jnp.concatenate([gh, jnp.where(m_t, -gh, 0.0), gl, jnp.where(m_t, -gl, 0.0)],
                                  axis=0).astype(BF16)
            ch["diff"] = _dot(lhs, rhs)
            ch["tot"] = jnp.sum(g_b, axis=0, keepdims=True)
        for ch in chains:
            d = ch["d"]
            ch["decay"] = jnp.exp(jnp.where(incl[d], ch["diff"], -1e30))
            a_mat = jnp.where(strict[d], ch["beta"] * ch["kk"] * ch["decay"], 0.0)
            ch["p"] = eye - a_mat
            ch["a"] = a_mat
        for ch in chains:
            ch["q2"] = _mm_hp(ch["a"], ch["a"])
        sq = 2
        while sq * 2 < CHUNK:
            for ch in chains:
                r = _mm_hp(jnp.concatenate([ch["p"], ch["q2"]], axis=0), ch["q2"])
                ch["p"] = ch["p"] + r[0:CHUNK]
                ch["q2"] = r[CHUNK:2 * CHUNK]
            sq *= 2
        for ch in chains:
            ch["t"] = ch["p"] + _mm_hp(ch["p"], ch["q2"])
        for ch in chains:
            gam = jnp.where(left, pltpu.roll(ch["diff"], CHUNK, 1), ch["diff"])
            egam = jnp.exp(gam)
            beta = ch["beta"]
            rhs_uw = jnp.concatenate([ch["k"] * (beta * egam), ch["v"] * beta], axis=1).astype(BF16)
            ch["wu"] = _dot(ch["t"][:, 0:CHUNK].astype(BF16), rhs_uw).astype(BF16)
            ch["qg"] = ch["q"] * egam
            ch["kd"] = (ch["k"] * jnp.exp(ch["tot"] - gam)).astype(BF16)
        for ch in chains:
            d, c = ch["d"], ch["c"]
            qk_m = (ch["qk_raw"] * ch["decay"])[:, 0:CHUNK].astype(BF16)
            x = _dot(qk_m, ch["wu"])
            mn = _dot_tn(ch["wu"], ch["kd"])
            qe_s[d, ch["rows"], :] = (ch["qg"] - x[:, 0:LANES]).astype(BF16)
            out_ref[ch["orow"], :] = out_ref[ch["orow"], :] + x[:, LANES:2 * LANES]
            mt_s[d, c] = mn[0:LANES].astype(BF16)
            nt_s[d, c] = mn[LANES:2 * LANES]
            gl_s[d, c] = jnp.broadcast_to(jnp.exp(ch["tot"]), (SUBLANES, LANES))

    def prep_segment(chunk0, n_chunks, out_ref):
        group = max(g for g in (8, 4, 2, 1) if n_chunks % g == 0)

        def prep(gi, carry):
            chunk_group(chunk0, out_ref, group, gi)
            return carry
        lax.fori_loop(0, n_chunks // group, prep, 0)

    def rec_segment(chunk0, n_chunks, out_ref):
        def step(s, carry):
            for d in range(2):
                lc = s if d == 0 else n_chunks - 1 - s
                c = chunk0 + lc
                rows = pl.ds(pl.multiple_of(c * CHUNK, CHUNK), CHUNK)
                orow = pl.ds(pl.multiple_of(lc * CHUNK, CHUNK), CHUNK)
                st = st_s[d]
                stb = st.astype(BF16)
                r = _dot(stb, mt_s[d, c])
                out_ref[orow, :] = out_ref[orow, :] + _dot_nt(qe_s[d, rows, :], stb)
                st_s[d] = st * gl_s[d, c][0:1, :] + nt_s[d, c] - r
            return carry
        lax.fori_loop(0, n_chunks, step, 0)

    st_s[...] = jnp.zeros_like(st_s)
    ol_ref[...] = jnp.zeros_like(ol_ref)
    oc_ref[...] = jnp.zeros_like(oc_ref)
    prep_segment(0, n_ctx // CHUNK, oc_ref)
    prep_segment(n_ctx // CHUNK, n_lat // CHUNK, ol_ref)
    rec_segment(0, n_ctx // CHUNK, oc_ref)
    rec_segment(n_ctx // CHUNK, n_lat // CHUNK, ol_ref)


def _gdn_call(qkv_l, qkv_c, ab_l, ab_c, conv_w, gparams, bsz, n_lat, n_ctx):
    n_tot = n_lat + n_ctx
    nh = GDN_HEADS
    lat = lambda off: pl.BlockSpec((n_lat, LANES), lambda b, h: (b, off + h))
    ctx = lambda off: pl.BlockSpec((n_ctx, LANES), lambda b, h: (b, off + h))
    cw = lambda off: pl.BlockSpec((CONV_K, LANES), lambda b, h: (0, off + h))
    kern = functools.partial(_gdn_kernel, n_lat=n_lat, n_ctx=n_ctx)
    return pl.pallas_call(
        kern,
        grid=(bsz, nh),
        in_specs=[lat(0), lat(nh), lat(2 * nh), ctx(0), ctx(nh), ctx(2 * nh), cw(0), cw(nh), cw(2 * nh),
                  pl.BlockSpec((None, n_lat, 4), lambda b, h: (h, b, 0)),
                  pl.BlockSpec((None, n_ctx, 4), lambda b, h: (h, b, 0)),
                  pl.BlockSpec((None, 1, 4), lambda b, h: (h, 0, 0))],
        out_specs=[pl.BlockSpec((n_lat, LANES), lambda b, h: (b, h)),
                   pl.BlockSpec((n_ctx, LANES), lambda b, h: (b, h))],
        out_shape=[jax.ShapeDtypeStruct((bsz * n_lat, GDN_W), F32),
                   jax.ShapeDtypeStruct((bsz * n_ctx, GDN_W), F32)],
        scratch_shapes=[
            pltpu.VMEM((n_lat + 2 * SUBLANES, LANES), F32),
            pltpu.VMEM((n_tot, LANES), F32),
            pltpu.VMEM((n_tot, LANES), F32),
            pltpu.VMEM((n_tot, LANES), F32),
            pltpu.VMEM((n_tot, 4), F32),
            pltpu.VMEM((2, n_tot // CHUNK, GDN_DK, LANES), BF16),
            pltpu.VMEM((2, n_tot // CHUNK, LANES, GDN_DK), F32),
            pltpu.VMEM((2, n_tot, LANES), BF16),
            pltpu.VMEM((2, n_tot // CHUNK, SUBLANES, LANES), F32),
            pltpu.VMEM((2, GDN_DK, LANES), F32),
        ],
        compiler_params=_cparams(("parallel", "arbitrary")),
    )(qkv_l, qkv_l, qkv_l, qkv_c, qkv_c, qkv_c, conv_w, conv_w, conv_w, ab_l, ab_c, gparams)


def _lru_kernel(xl_ref, xc_ref, cw_ref, cb_ref, gw_ref, gb_ref, lam_ref, hl_ref, hc_ref,
                pad_s, x_s, a_s, u_s, *, n_lat, n_ctx):
    n_tot = n_ctx + n_lat
    x_s[0:n_ctx, :] = _dwconv(xc_ref, cw_ref, pad_s, n_ctx) + cb_ref[...]
    x_s[n_ctx:n_tot, :] = _dwconv(xl_ref, cw_ref, pad_s, n_lat) + cb_ref[...]

    sp = [_softplus(-lam_ref[:, d * LANES:(d + 1) * LANES]) for d in range(2)]
    tile = min(256, n_ctx)

    def coeffs(t, carry):
        rows = pl.ds(pl.multiple_of(t * tile, tile), tile)
        x = x_s[rows, :]
        gates = _dot(x.astype(BF16), gw_ref[...]) + gb_ref[...]
        for d in range(2):
            r = _sigmoid(gates[:, (2 * d) * LANES:(2 * d + 1) * LANES])
            i = _sigmoid(gates[:, (2 * d + 1) * LANES:(2 * d + 2) * LANES])
            a = jnp.exp(-LRU_C * r * sp[d])
            a_s[d, rows, :] = a
            u_s[d, rows, :] = jnp.sqrt(1.0 - a * a) * (i * x)
        return carry

    lax.fori_loop(0, n_tot // tile, coeffs, 0)

    sub = lax.broadcasted_iota(jnp.int32, (SUBLANES, LANES), 0)

    def group_scan(d, g0):
        rows = pl.ds(pl.multiple_of(g0, SUBLANES), SUBLANES)
        a = a_s[d, rows, :]
        u = u_s[d, rows, :]
        sh = 1
        while sh < SUBLANES:
            if d == 0:
                a_sh, u_sh, ok = pltpu.roll(a, sh, 0), pltpu.roll(u, sh, 0), sub >= sh
            else:
                a_sh, u_sh = pltpu.roll(a, SUBLANES - sh, 0), pltpu.roll(u, SUBLANES - sh, 0)
                ok = sub < SUBLANES - sh
            u = u + a * jnp.where(ok, u_sh, 0.0)
            a = a * jnp.where(ok, a_sh, 1.0)
            sh *= 2
        return rows, u, a

    def fwd(base, n, h0):
        def body(g, h):
            rows, u, a = group_scan(0, base + g * SUBLANES)
            hh = u + a * h
            u_s[0, rows, :] = hh
            return jnp.broadcast_to(hh[SUBLANES - 1:SUBLANES, :], (SUBLANES, LANES))
        return lax.fori_loop(0, n // SUBLANES, body, h0, unroll=4)

    def bwd(base, n, h0, out_ref):
        def body(g, h):
            lg = n // SUBLANES - 1 - g
            rows, u, a = group_scan(1, base + lg * SUBLANES)
            hh = u + a * h
            orow = pl.ds(pl.multiple_of(lg * SUBLANES, SUBLANES), SUBLANES)
            out_ref[orow, :] = u_s[0, rows, :] + hh
            return jnp.broadcast_to(hh[0:1, :], (SUBLANES, LANES))
        return lax.fori_loop(0, n // SUBLANES, body, h0, unroll=4)

    zero = jnp.zeros((SUBLANES, LANES), F32)
    h = fwd(0, n_ctx, zero)
    fwd(n_ctx, n_lat, h)
    h = bwd(0, n_ctx, zero, hc_ref)
    bwd(n_ctx, n_lat, h, hl_ref)


def _lru_call(xr_l, xr_c, conv_w, conv_b, gate_w, gate_b, lam, bsz, n_lat, n_ctx):
    n_tot = n_lat + n_ctx
    nt = LRU_W // LANES
    kern = functools.partial(_lru_kernel, n_lat=n_lat, n_ctx=n_ctx)
    return pl.pallas_call(
        kern,
        grid=(bsz, nt),
        in_specs=[pl.BlockSpec((n_lat, LANES), lambda b, j: (b, j)),
                  pl.BlockSpec((n_ctx, LANES), lambda b, j: (b, j)),
                  pl.BlockSpec((CONV_K, LANES), lambda b, j: (0, j)),
                  pl.BlockSpec((1, LANES), lambda b, j: (0, j)),
                  pl.BlockSpec((None, LANES, 4 * LANES), lambda b, j: (j, 0, 0)),
                  pl.BlockSpec((None, 1, 4 * LANES), lambda b, j: (j, 0, 0)),
                  pl.BlockSpec((None, 1, 2 * LANES), lambda b, j: (j, 0, 0))],
        out_specs=[pl.BlockSpec((n_lat, LANES), lambda b, j: (b, j)),
                   pl.BlockSpec((n_ctx, LANES), lambda b, j: (b, j))],
        out_shape=[jax.ShapeDtypeStruct((bsz * n_lat, LRU_W), F32),
                   jax.ShapeDtypeStruct((bsz * n_ctx, LRU_W), F32)],
        scratch_shapes=[pltpu.VMEM((n_lat + 2 * SUBLANES, LANES), F32),
                        pltpu.VMEM((n_tot, LANES), F32),
                        pltpu.VMEM((2, n_tot, LANES), F32),
                        pltpu.VMEM((2, n_tot, LANES), F32)],
        compiler_params=_cparams(("parallel", "arbitrary")),
    )(xr_l, xr_c, conv_w, conv_b, gate_w, gate_b, lam)


def _residual_ln(h, y, mod_ref, gate_row, lng_ref, lnb_ref, alpha):
    return _layernorm(alpha * h + mod_ref[gate_row:gate_row + 1, :] * y, lng_ref[...], lnb_ref[...])


FF_TILE = 1024


def _mlp_residual(h, mod_ref, w1_ref, w2_ref, lng_ref, lnb_ref, alpha):
    u = _modulate(h, mod_ref, 3, 4).astype(BF16)
    acc = None
    for j in range(D_FF // FF_TILE):
        sl = slice(j * FF_TILE, (j + 1) * FF_TILE)
        a = jnp.maximum(_dot(u, w1_ref[:, sl]), 0.0)
        part = _dot((a * a).astype(BF16), w2_ref[sl, :])
        acc = part if acc is None else acc + part
    return _residual_ln(h, acc, mod_ref, 5, lng_ref, lnb_ref, alpha)


def _post_even_kernel(o_ref, z_ref, hr_ref, gate_ref, h_ref, mod_ref, gn_ref, w_ref, lng0_ref, lnb0_ref,
                      w1_ref, w2_ref, lng1_ref, lnb1_ref, out_ref, *, alpha):
    parts = []
    for hd in range(GDN_HEADS):
        sl = slice(hd * LANES, (hd + 1) * LANES)
        o = o_ref[:, sl]
        on = o * lax.rsqrt(jnp.mean(o * o, axis=-1, keepdims=True) + NORM_EPS) * gn_ref[...]
        parts.append(on * _silu(z_ref[:, sl]))
    y_gdn = jnp.concatenate(parts, axis=1).astype(BF16)
    y_lru = (hr_ref[...] * _gelu_tanh(gate_ref[...])).astype(BF16)
    y = _dot(y_gdn, w_ref[0:GDN_W, :]) + _dot(y_lru, w_ref[GDN_W:, :])
    h1 = _residual_ln(h_ref[...], y, mod_ref, 2, lng0_ref, lnb0_ref, alpha)
    out_ref[...] = _mlp_residual(h1, mod_ref, w1_ref, w2_ref, lng1_ref, lnb1_ref, alpha)


def _post_odd_kernel(y_ref, h_ref, mod_ref, w_ref, lng0_ref, lnb0_ref, w1_ref, w2_ref, lng1_ref, lnb1_ref,
                     out_ref, *, alpha):
    y = _dot(y_ref[...], w_ref[...])
    h1 = _residual_ln(h_ref[...], y, mod_ref, 2, lng0_ref, lnb0_ref, alpha)
    out_ref[...] = _mlp_residual(h1, mod_ref, w1_ref, w2_ref, lng1_ref, lnb1_ref, alpha)


def _post_call(kern, token_inputs, h, mod, layer, per_batch, mixer_consts, tail_consts, alpha):
    n = h.shape[0]
    tm = _token_tile(n, per_batch)
    rpm = None if per_batch is None else per_batch // tm
    tok = lambda w: pl.BlockSpec((tm, w), lambda t: (t, 0))
    consts = list(mixer_consts) + list(tail_consts)
    return pl.pallas_call(
        functools.partial(kern, alpha=alpha),
        grid=(n // tm,),
        in_specs=[tok(a.shape[1]) for a in token_inputs] + [tok(D_MODEL), _mod_spec(layer, rpm)]
                 + [_const_spec(a.shape) for a in consts],
        out_specs=tok(D_MODEL),
        out_shape=jax.ShapeDtypeStruct((n, D_MODEL), F32),
        compiler_params=_cparams(("parallel",)),
    )(*token_inputs, h, mod, *consts)


def _qkv_rope_kernel(h_ref, mod_ref, wq_ref, wk_ref, wvt_ref, cos_ref, sin_ref, q_ref, k_ref, vt_ref):
    u = _modulate(h_ref[...], mod_ref, 0, 1).astype(BF16)
    cos = cos_ref[...]
    sin_signed = sin_ref[...]
    lane = lax.broadcasted_iota(jnp.int32, cos.shape, 1)
    partner_above = (lane % (2 * ROPE_Q)) < ROPE_Q

    def rope(y):
        parts = []
        for hd in range(DIFF_HEADS):
            t = y[:, hd * LANES:(hd + 1) * LANES]
            partner = jnp.where(partner_above, pltpu.roll(t, LANES - ROPE_Q, 1), pltpu.roll(t, ROPE_Q, 1))
            parts.append(t * cos + partner * sin_signed)
        return jnp.concatenate(parts, axis=1).astype(BF16)

    q_ref[...] = rope(_dot(u, wq_ref[...]))
    k_ref[...] = rope(_dot(u, wk_ref[...]))
    vt_ref[...] = _dot_nt(wvt_ref[...], u).astype(BF16)


def _qkv_plain_kernel(h_ref, mod_ref, wq_ref, wk_ref, wvt_ref, q_ref, k_ref, vt_ref):
    u = _modulate(h_ref[...], mod_ref, 0, 1).astype(BF16)
    q_ref[...] = _dot(u, wq_ref[...]).astype(BF16)
    k_ref[...] = _dot(u, wk_ref[...]).astype(BF16)
    vt_ref[...] = _dot_nt(wvt_ref[...], u).astype(BF16)


def _qkv_call(h, mod, layer, per_batch, weights, rope):
    n = h.shape[0]
    tm = _token_tile(n, per_batch)
    tok = pl.BlockSpec((tm, D_MODEL), lambda t: (t, 0))
    out_specs = [tok, tok, pl.BlockSpec((D_MODEL, tm), lambda t: (0, t))]
    out_shape = [jax.ShapeDtypeStruct((n, D_MODEL), BF16), jax.ShapeDtypeStruct((n, D_MODEL), BF16),
                 jax.ShapeDtypeStruct((D_MODEL, n), BF16)]
    wspec = _const_spec((D_MODEL, D_MODEL))
    if rope is not None:
        cos2, sin2 = rope
        rpm = per_batch // tm
        tab = pl.BlockSpec((tm, LANES), lambda t: (t % rpm, 0))
        wq, wk, wvt = weights
        return pl.pallas_call(
            _qkv_rope_kernel, grid=(n // tm,),
            in_specs=[tok, _mod_spec(layer, rpm), wspec, wspec, wspec, tab, tab],
            out_specs=out_specs, out_shape=out_shape,
            compiler_params=_cparams(("parallel",)),
        )(h, mod, wq, wk, wvt, cos2, sin2)
    wq, wk, wvt = weights
    return pl.pallas_call(
        _qkv_plain_kernel, grid=(n // tm,),
        in_specs=[tok, _mod_spec(layer, None), wspec, wspec, wspec],
        out_specs=out_specs, out_shape=out_shape,
        compiler_params=_cparams(("parallel",)),
    )(h, mod, wq, wk, wvt)


KEY_TILE = 512
SCORE_LOOKAHEAD = 1
Q_TILE = 1024


def _attn_kernel(*refs, key_lens, lam_init):
    nseg = len(key_lens)
    q_ref = refs[0]
    k_refs = refs[1:1 + nseg]
    vt_refs = refs[1 + nseg:1 + 2 * nseg]
    lam_ref, subln_ref, y_ref, acc_s = refs[1 + 2 * nseg:]
    tq = q_ref.shape[0]

    q = q_ref[...]
    lane = lax.broadcasted_iota(jnp.int32, q.shape, 1)
    zero = jnp.zeros_like(q)
    qm = [jnp.where(lane < DIFF_D, q, zero), jnp.where(lane >= DIFF_D, q, zero)]
    acc_s[...] = jnp.zeros_like(acc_s)

    tiles = []
    for k_ref, vt_ref, n_keys in zip(k_refs, vt_refs, key_lens):
        tk = min(KEY_TILE, n_keys)
        tiles += [(k_ref, vt_ref, j * tk, tk) for j in range(n_keys // tk)]

    def tile_scores(t):
        k_ref, _, k0, tk = tiles[t]
        k_t = k_ref[k0:k0 + tk, :]
        return [_dot_nt(k_t, qm[0]), _dot_nt(k_t, qm[1])]

    def absorb(s_pair, t, stats):
        _, vt_ref, k0, tk = tiles[t]
        vt_t = vt_ref[:, k0:k0 + tk]
        new = []
        for m in range(2):
            m_old, l_old = stats[m]
            s_t = s_pair[m]
            m_new = jnp.maximum(m_old, jnp.max(s_t, axis=0, keepdims=True))
            alpha = jnp.exp2(m_old - m_new)
            p = jnp.exp2(s_t - m_new)
            l_new = alpha * l_old + jnp.sum(p, axis=0, keepdims=True)
            acc_s[m] = alpha * acc_s[m] + _dot(vt_t, p.astype(BF16))
            new.append((m_new, l_new))
        return tuple(new)

    init = (jnp.full((1, tq), -1e30, F32), jnp.zeros((1, tq), F32))
    stats = (init, init)
    ahead = [tile_scores(t) for t in range(min(SCORE_LOOKAHEAD, len(tiles)))]
    for t in range(len(tiles)):
        if t + SCORE_LOOKAHEAD < len(tiles):
            ahead.append(tile_scores(t + SCORE_LOOKAHEAD))
        stats = absorb(ahead.pop(0), t, stats)

    lv = lam_ref[...]
    lam = (jnp.exp(jnp.sum(lv[0:1] * lv[1:2], axis=1, keepdims=True))
           - jnp.exp(jnp.sum(lv[2:3] * lv[3:4], axis=1, keepdims=True)) + lam_init)
    o_t = acc_s[0] / stats[0][1] - lam * (acc_s[1] / stats[1][1])
    ms = jnp.mean(o_t * o_t, axis=0, keepdims=True)
    y_t = o_t * lax.rsqrt(ms + NORM_EPS) * subln_ref[...] * (1.0 - lam_init)
    y_ref[...] = y_t.T.astype(BF16)


def _attn_call(q, ks, vts, lam_vec, subln_col, bsz, n_q, key_lens, lam_init):
    tq = min(Q_TILE, n_q)
    nq = n_q // tq
    nseg = len(key_lens)
    in_specs = [pl.BlockSpec((tq, LANES), lambda b, h, i: (b * nq + i, h))]
    in_specs += [pl.BlockSpec((n, LANES), lambda b, h, i: (b, h)) for n in key_lens]
    in_specs += [pl.BlockSpec((LANES, n), lambda b, h, i: (h, b)) for n in key_lens]
    in_specs += [pl.BlockSpec((4, DIFF_D), lambda b, h, i: (0, 0)),
                 pl.BlockSpec((DIFF_DV, 1), lambda b, h, i: (0, 0))]
    kern = functools.partial(_attn_kernel, key_lens=tuple(key_lens), lam_init=lam_init)
    return pl.pallas_call(
        kern,
        grid=(bsz, DIFF_HEADS, nq),
        in_specs=in_specs,
        out_specs=pl.BlockSpec((tq, LANES), lambda b, h, i: (b * nq + i, h)),
        out_shape=jax.ShapeDtypeStruct((bsz * n_q, D_MODEL), BF16),
        scratch_shapes=[pltpu.VMEM((2, DIFF_DV, tq), F32)],
        compiler_params=_cparams(("parallel", "parallel", "arbitrary")),
    )(q, *ks, *vts, lam_vec, subln_col)


def _rope_tables(n_lat):
    rows = n_lat // GRID_W
    row = jnp.repeat(jnp.arange(rows), GRID_W).astype(F32)
    col = jnp.tile(jnp.arange(GRID_W), rows).astype(F32)
    half = DIFF_D // 2
    inv = ROPE_THETA ** (-(jnp.arange(0, half, 2, dtype=F32) / half))
    ang_r = row[:, None] * inv
    ang_c = col[:, None] * inv
    ang = jnp.concatenate([ang_r, ang_r, ang_c, ang_c], axis=-1)
    cos, sin = jnp.cos(ang), jnp.sin(ang)
    sign = jnp.where((jnp.arange(DIFF_D) % (2 * ROPE_Q)) < ROPE_Q, -1.0, 1.0).astype(F32)
    sin_signed = sin * sign
    return jnp.concatenate([cos, cos], axis=-1), jnp.concatenate([sin_signed, sin_signed], axis=-1)


def _lru_gate_weights(gate_w, gate_b, lam):
    nt = LRU_W // LANES
    per = LANES // LRU_BW
    blocks = gate_w.reshape(2, 2, nt, per, LRU_BW, LRU_BW)
    eye = jnp.eye(per, dtype=gate_w.dtype)
    dense = jnp.einsum('dgtpab,pq->dgtpaqb', blocks, eye).reshape(2, 2, nt, LANES, LANES)
    w = jnp.transpose(dense, (2, 3, 0, 1, 4)).reshape(nt, LANES, 4 * LANES)
    b = jnp.transpose(gate_b.reshape(2, 2, nt, LANES), (2, 0, 1, 3)).reshape(nt, 1, 4 * LANES)
    lm = jnp.transpose(lam.reshape(2, nt, LANES), (1, 0, 2)).reshape(nt, 1, 2 * LANES)
    return w.astype(BF16), b, lm


def _per_head_gates(ab, n_rows):
    t = ab.reshape(n_rows, 2, 2, GDN_HEADS)
    return jnp.transpose(t, (3, 0, 1, 2)).reshape(GDN_HEADS, n_rows, 4)


def kernel(x, c, ctx, c_ctx, ada_w, ada_b, ln_g, ln_b, mlp_w1, mlp_w2, mix_w_out, ev_w_in, ev_qkv_conv,
           ev_a_log, ev_dt_bias, ev_gdn_norm, ev_lru_conv_w, ev_lru_conv_b, ev_lru_gate_w, ev_lru_gate_b,
           ev_lru_lambda, od_w_qkv, od_lambda, od_subln):
    bsz, n_lat, d = x.shape
    n_ctx = ctx.shape[1]
    depth = ada_w.shape[0]
    assert d == D_MODEL and bsz + 1 <= MOD_ROWS
    assert n_lat % CHUNK == 0 and n_ctx % CHUNK == 0 and n_lat % GRID_W == 0
    alpha = (2 * depth) ** 0.25

    h_lat = x.reshape(bsz * n_lat, d)
    h_ctx = ctx.reshape(bsz * n_ctx, d)
    cc = jnp.concatenate([c_ctx[None, :], c, jnp.zeros((MOD_ROWS - 1 - bsz, d), F32)], axis=0)
    mod = _ada_call(cc, ada_w, ada_b).reshape(depth, MOD_ROWS, 6, d)
    cos2, sin2 = _rope_tables(n_lat)

    for i in range(depth):
        last = i == depth - 1
        j = i // 2
        w_out = mix_w_out[i].astype(BF16)
        lng0, lnb0 = ln_g[i, 0][None, :], ln_b[i, 0][None, :]
        lng1, lnb1 = ln_g[i, 1][None, :], ln_b[i, 1][None, :]
        tail = (mlp_w1[i].astype(BF16), mlp_w2[i].astype(BF16), lng1, lnb1)
        if i % 2 == 0:
            w_in = ev_w_in[j]
            c0, c1, c2 = 4 * GDN_W, 4 * GDN_W + 4 * GDN_HEADS, 4 * GDN_W + 4 * GDN_HEADS + 2 * LRU_W
            w_main = jnp.concatenate([w_in[:, :c0], w_in[:, c1:c2]], axis=1).astype(BF16)
            w_ab = jnp.pad(w_in[:, c0:c1], ((0, 0), (0, LANES - 4 * GDN_HEADS))).astype(BF16)
            qkv_l, z_l, xr_l, gate_l, ab_l = _inproj_even_call(h_lat, mod, i, n_lat, w_main, w_ab)
            qkv_c, z_c, xr_c, gate_c, ab_c = _inproj_even_call(h_ctx, mod, i, None, w_main, w_ab)
            gparams = jnp.stack([ev_a_log[j, 0], ev_a_log[j, 1], ev_dt_bias[j, 0], ev_dt_bias[j, 1]],
                                axis=-1).reshape(GDN_HEADS, 1, 4)
            o_l, o_c = _gdn_call(qkv_l, qkv_c, _per_head_gates(ab_l, bsz * n_lat),
                                 _per_head_gates(ab_c, bsz * n_ctx), ev_qkv_conv[j], gparams,
                                 bsz, n_lat, n_ctx)
            gw, gb, lm = _lru_gate_weights(ev_lru_gate_w[j], ev_lru_gate_b[j], ev_lru_lambda[j])
            hr_l, hr_c = _lru_call(xr_l, xr_c, ev_lru_conv_w[j], ev_lru_conv_b[j][None, :], gw, gb, lm,
                                   bsz, n_lat, n_ctx)
            gn = ev_gdn_norm[j][None, :]
            mixer = (gn, w_out, lng0, lnb0)
            h_lat = _post_call(_post_even_kernel, (o_l, z_l, hr_l, gate_l), h_lat, mod, i, n_lat, mixer, tail,
                               alpha)
            if not last:
                h_ctx = _post_call(_post_even_kernel, (o_c, z_c, hr_c, gate_c), h_ctx, mod, i, None, mixer,
                                   tail, alpha)
        else:
            lam_init = 0.8 - 0.6 * math.exp(-0.3 * i)
            wq = od_w_qkv[j][:, :D_MODEL] * (DIFF_D ** -0.5 * math.log2(math.e))
            wk = od_w_qkv[j][:, D_MODEL:2 * D_MODEL]
            wvt = od_w_qkv[j][:, 2 * D_MODEL:].T.astype(BF16)
            qkv_w = (wq.astype(BF16), wk.astype(BF16), wvt)
            q_l, k_l, vt_l = _qkv_call(h_lat, mod, i, n_lat, qkv_w, (cos2, sin2))
            q_c, k_c, vt_c = _qkv_call(h_ctx, mod, i, None, qkv_w, None)
            subln_col = od_subln[j][:, None]
            y_l = _attn_call(q_l, [k_c, k_l], [vt_c, vt_l], od_lambda[j], subln_col, bsz, n_lat,
                             [n_ctx, n_lat], lam_init)
            mixer = (w_out, lng0, lnb0)
            h_lat = _post_call(_post_odd_kernel, (y_l,), h_lat, mod, i, n_lat, mixer, tail, alpha)
            if not last:
                y_c = _attn_call(q_c, [k_c], [vt_c], od_lambda[j], subln_col, bsz, n_ctx, [n_ctx], lam_init)
                h_ctx = _post_call(_post_odd_kernel, (y_c,), h_ctx, mod, i, None, mixer, tail, alpha)
    return h_lat.reshape(bsz, n_lat, d)
```

```python
import functools
import math

import jax
import jax.numpy as jnp
from jax import lax
from jax.experimental import pallas as pl
from jax.experimental.pallas import tpu as pltpu

F32 = jnp.float32
BF16 = jnp.bfloat16

D_MODEL = 1024
D_FF = 4 * D_MODEL
CONV_K = 4
CONV_LEFT = 2
GDN_HEADS = 4
GDN_DK = 128
GDN_W = GDN_HEADS * GDN_DK
CHUNK = 64
LRU_W = D_MODEL - GDN_W
LRU_BLOCKS = 8
LRU_BW = LRU_W // LRU_BLOCKS
LRU_C = 8.0
DIFF_HEADS = 8
DIFF_D = 64
DIFF_DV = 128
ROPE_Q = DIFF_D // 4
GRID_W = 64
ROPE_THETA = 10000.0
NORM_EPS = 1e-6
LANES = 128
SUBLANES = 8
MOD_ROWS = 16
VMEM_LIMIT = 56 * 1024 * 1024


def _cparams(sem):
    return pltpu.CompilerParams(dimension_semantics=sem, vmem_limit_bytes=VMEM_LIMIT)


def _sigmoid(x):
    return 0.5 * jnp.tanh(0.5 * x) + 0.5


def _silu(x):
    return x * _sigmoid(x)


def _softplus(x):
    return jnp.maximum(x, 0.0) + jnp.log1p(jnp.exp(-jnp.abs(x)))


def _gelu_tanh(x):
    return 0.5 * x * (1.0 + jnp.tanh(math.sqrt(2.0 / math.pi) * (x + 0.044715 * (x * x * x))))


def _modulate(h, mod_ref, shift_row, scale_row):
    return h * (1.0 + mod_ref[scale_row:scale_row + 1, :]) + mod_ref[shift_row:shift_row + 1, :]


def _layernorm(x, g, b):
    mu = jnp.mean(x, axis=-1, keepdims=True)
    xc = x - mu
    var = jnp.mean(xc * xc, axis=-1, keepdims=True)
    return xc * lax.rsqrt(var + NORM_EPS) * g + b


def _dot(a, b):
    return jnp.dot(a, b, preferred_element_type=F32)


def _dot_nt(a, b):
    return lax.dot_general(a, b, (((1,), (1,)), ((), ())), preferred_element_type=F32)


def _dot_tn(a, b):
    return lax.dot_general(a, b, (((0,), (0,)), ((), ())), preferred_element_type=F32)


def _ada_kernel(c_ref, w_ref, b_ref, o_ref):
    s = _silu(c_ref[...])
    o_ref[...] = jnp.dot(s, w_ref[...], preferred_element_type=F32,
                         precision=lax.Precision.HIGHEST) + b_ref[...]


def _ada_call(cc, ada_w, ada_b):
    depth, d, n = ada_w.shape
    tn = D_MODEL
    return pl.pallas_call(
        _ada_kernel,
        grid=(depth, n // tn),
        in_specs=[pl.BlockSpec((MOD_ROWS, d), lambda i, j: (0, 0)),
                  pl.BlockSpec((None, d, tn), lambda i, j: (i, 0, j)),
                  pl.BlockSpec((None, 1, tn), lambda i, j: (i, 0, j))],
        out_specs=pl.BlockSpec((None, MOD_ROWS, tn), lambda i, j: (i, 0, j)),
        out_shape=jax.ShapeDtypeStruct((depth, MOD_ROWS, n), F32),
        compiler_params=_cparams(("parallel", "parallel")),
    )(cc, ada_w, ada_b.reshape(depth, 1, n))


def _mod_spec(layer, rows_per_mod):
    if rows_per_mod is None:
        return pl.BlockSpec((None, None, 6, D_MODEL), lambda t: (layer, 0, 0, 0))
    return pl.BlockSpec((None, None, 6, D_MODEL), lambda t: (layer, 1 + t // rows_per_mod, 0, 0))


def _const_spec(shape):
    nd = len(shape)
    return pl.BlockSpec(shape, lambda t: (0,) * nd, pipeline_mode=pl.Buffered(1))


def _token_tile(n_rows, per_batch):
    tm = min(512, n_rows if per_batch is None else per_batch)
    assert n_rows % tm == 0 and (per_batch is None or per_batch % tm == 0)
    return tm


def _inproj_even_kernel(h_ref, mod_ref, w_ref, wab_ref, qkv_ref, z_ref, xr_ref, gate_ref, ab_ref):
    u = _modulate(h_ref[...], mod_ref, 0, 1).astype(BF16)
    qkv_ref[...] = _dot(u, w_ref[:, 0:3 * GDN_W])
    z_ref[...] = _dot(u, w_ref[:, 3 * GDN_W:4 * GDN_W])
    xr_ref[...] = _dot(u, w_ref[:, 4 * GDN_W:4 * GDN_W + LRU_W])
    gate_ref[...] = _dot(u, w_ref[:, 4 * GDN_W + LRU_W:])
    ab = _dot(u, wab_ref[...])
    for hd in range(GDN_HEADS):
        ab_ref[hd] = ab[:, 4 * hd:4 * hd + 4]


def _inproj_even_call(h, mod, layer, per_batch, w_main, w_ab):
    n = h.shape[0]
    tm = _token_tile(n, per_batch)
    rpm = None if per_batch is None else per_batch // tm
    tok = lambda w: pl.BlockSpec((tm, w), lambda t: (t, 0))
    return pl.pallas_call(
        _inproj_even_kernel,
        grid=(n // tm,),
        in_specs=[tok(D_MODEL), _mod_spec(layer, rpm), _const_spec(w_main.shape), _const_spec(w_ab.shape)],
        out_specs=[tok(3 * GDN_W), tok(GDN_W), tok(LRU_W), tok(LRU_W),
                   pl.BlockSpec((GDN_HEADS, tm, 4), lambda t: (0, t, 0))],
        out_shape=[jax.ShapeDtypeStruct((n, 3 * GDN_W), F32), jax.ShapeDtypeStruct((n, GDN_W), F32),
                   jax.ShapeDtypeStruct((n, LRU_W), F32), jax.ShapeDtypeStruct((n, LRU_W), F32),
                   jax.ShapeDtypeStruct((GDN_HEADS, n, 4), F32)],
        compiler_params=_cparams(("parallel",)),
    )(h, mod, w_main, w_ab)


def _dwconv(x_ref, w_ref, pad_ref, n):
    zeros = jnp.zeros((SUBLANES, LANES), F32)
    pad_ref[0:SUBLANES, :] = zeros
    pad_ref[SUBLANES + n:2 * SUBLANES + n, :] = zeros
    pad_ref[SUBLANES:SUBLANES + n, :] = x_ref[...]
    y = None
    for j in range(CONV_K):
        off = SUBLANES + j - CONV_LEFT
        term = pad_ref[off:off + n, :] * w_ref[j:j + 1, :]
        y = term if y is None else y + term
    return y


def _hp_parts(x):
    hi = x.astype(BF16).astype(F32)
    return hi, x - hi


def _hp_operands(x):
    hi, lo = _hp_parts(x)
    x2 = hi + pltpu.roll(lo, CHUNK, 1)
    lhs = jnp.concatenate([x2, x2], axis=1).astype(BF16)
    rhs = jnp.concatenate([hi, hi, lo, lo], axis=0).astype(BF16)
    return lhs, rhs


def _gdn_kernel(ql_ref, kl_ref, vl_ref, qc_ref, kc_ref, vc_ref, wq_ref, wk_ref, wv_ref,
                abl_ref, abc_ref, gp_ref, ol_ref, oc_ref,
                pad_s, q_s, k_s, v_s, ab_s, mt_s, nt_s, qe_s, gl_s, st_s, *, n_lat, n_ctx):
    n_tot = n_ctx + n_lat

    def prep(x_ref, w_ref, n, dst, base, kind):
        y = _silu(_dwconv(x_ref, w_ref, pad_s, n))
        if kind != "v":
            y = y * lax.rsqrt(jnp.sum(y * y, axis=-1, keepdims=True) + NORM_EPS)
        if kind == "q":
            y = y * (GDN_DK ** -0.5)
        dst[base:base + n, :] = y

    prep(qc_ref, wq_ref, n_ctx, q_s, 0, "q")
    prep(kc_ref, wk_ref, n_ctx, k_s, 0, "k")
    prep(vc_ref, wv_ref, n_ctx, v_s, 0, "v")
    prep(ql_ref, wq_ref, n_lat, q_s, n_ctx, "q")
    prep(kl_ref, wk_ref, n_lat, k_s, n_ctx, "k")
    prep(vl_ref, wv_ref, n_lat, v_s, n_ctx, "v")
    ab_s[0:n_ctx, :] = abc_ref[...]
    ab_s[n_ctx:n_tot, :] = abl_ref[...]

    row = lax.broadcasted_iota(jnp.int32, (CHUNK, LANES), 0)
    col = lax.broadcasted_iota(jnp.int32, (CHUNK, LANES), 1)
    left = col < CHUNK
    eye = jnp.where(row == col, 1.0, 0.0).astype(F32)
    incl = [jnp.logical_and(row >= col, left), jnp.logical_and(row <= col, left)]
    strict = [jnp.logical_and(row > col, left), jnp.logical_and(row < col, left)]
    ones_left = jnp.where(left, 1.0, 0.0).astype(F32)
    gp = gp_ref[...]

    def chunk_group(chunk0, out_ref, group, gi):
        chains = []
        for g in range(group):
            lc = gi * group + g
            c = chunk0 + lc
            rows = pl.ds(pl.multiple_of(c * CHUNK, CHUNK), CHUNK)
            orow = pl.ds(pl.multiple_of(lc * CHUNK, CHUNK), CHUNK)
            q_c = q_s[rows, :]
            k_c = k_s[rows, :]
            v_c = v_s[rows, :]
            ab_c = ab_s[rows, :]
            kb = k_c.astype(BF16)
            k_pad = jnp.concatenate([kb, jnp.zeros_like(kb)], axis=0)
            qk_kk = _dot_nt(jnp.concatenate([q_c.astype(BF16), kb], axis=0), k_pad)
            for d in range(2):
                chains.append(dict(c=c, rows=rows, orow=orow, d=d, q=q_c, k=k_c, v=v_c, ab=ab_c,
                                   qk_raw=qk_kk[0:CHUNK], kk=qk_kk[CHUNK:2 * CHUNK]))
        for ch in chains:
            d = ch["d"]
            a_col = ch["ab"][:, d:d + 1]
            b_col = ch["ab"][:, 2 + d:3 + d]
            g_col = -jnp.exp(gp[:, d:d + 1]) * _softplus(a_col + gp[:, 2 + d:3 + d])
            ch["beta"] = _sigmoid(b_col)
            g_b = jnp.broadcast_to(g_col, (CHUNK, LANES))
            m_t = incl[1 - d]
            gh, gl = _hp_parts(g_b)
            cum = jnp.where(incl[d], 1.0, 0.0).astype(F32)
            lhs1 = cum + pltpu.roll(ones_left, CHUNK, 1)
            lhs = jnp.concatenate([lhs1, lhs1], axis=1).astype(BF16)
            rhs = jnp.concatenate([gh, jnp.where(m_t, -gh, 0.0), gl, jnp.where(m_t, -gl, 0.0)],
                                  axis=0).astype(BF16)
            ch["diff"] = _dot(lhs, rhs)
            ch["tot"] = jnp.sum(g_b, axis=0, keepdims=True)
        for ch in chains:
            d = ch["d"]
            ch["decay"] = jnp.exp(jnp.where(incl[d], ch["diff"], -1e30))
            a_mat = jnp.where(strict[d], ch["beta"] * ch["kk"] * ch["decay"], 0.0)
            ch["p"] = eye - a_mat
            ch["a"] = a_mat
        for ch in chains:
            a_lhs, a_rhs = _hp_operands(ch["a"])
            ch["q2"] = _dot(a_lhs, a_rhs)
        sq = 2
        while sq * 2 < CHUNK:
            for ch in chains:
                q_lhs, q_rhs = _hp_operands(ch["q2"])
                p_lhs, _ = _hp_operands(ch["p"])
                r = _dot(jnp.concatenate([p_lhs, q_lhs], axis=0), q_rhs)
                ch["p"] = ch["p"] + r[0:CHUNK]
                ch["q2"] = r[CHUNK:2 * CHUNK]
            sq *= 2
        for ch in chains:
            ch["t"] = ch["p"] + _dot(_hp_operands(ch["p"])[0], _hp_operands(ch["q2"])[1])
        for ch in chains:
            gam = jnp.where(left, pltpu.roll(ch["diff"], CHUNK, 1), ch["diff"])
            egam = jnp.exp(gam)
            beta = ch["beta"]
            rhs_uw = jnp.concatenate([ch["k"] * (beta * egam), ch["v"] * beta], axis=1).astype(BF16)
            ch["wu"] = _dot(ch["t"][:, 0:CHUNK].astype(BF16), rhs_uw).astype(BF16)
            ch["qg"] = ch["q"] * egam
            ch["kd"] = (ch["k"] * jnp.exp(ch["tot"] - gam)).astype(BF16)
        for ch in chains:
            d, c = ch["d"], ch["c"]
            qk_m = (ch["qk_raw"] * ch["decay"])[:, 0:CHUNK].astype(BF16)
            x = _dot(qk_m, ch["wu"])
            mn = _dot_tn(ch["wu"], ch["kd"])
            qe_s[d, ch["rows"], :] = (ch["qg"] - x[:, 0:LANES]).astype(BF16)
            out_ref[ch["orow"], :] = out_ref[ch["orow"], :] + x[:, LANES:2 * LANES]
            mt_s[d, c] = mn[0:LANES].astype(BF16)
            nt_s[d, c] = mn[LANES:2 * LANES]
            gl_s[d, c] = jnp.broadcast_to(jnp.exp(ch["tot"]), (SUBLANES, LANES))

    def prep_segment(chunk0, n_chunks, out_ref):
        group = max(g for g in (8, 4, 2, 1) if n_chunks % g == 0)

        def prep(gi, carry):
            chunk_group(chunk0, out_ref, group, gi)
            return carry
        lax.fori_loop(0, n_chunks // group, prep, 0)

    def rec_segment(chunk0, n_chunks, out_ref):
        def step(s, carry):
            for d in range(2):
                lc = s if d == 0 else n_chunks - 1 - s
                c = chunk0 + lc
                rows = pl.ds(pl.multiple_of(c * CHUNK, CHUNK), CHUNK)
                orow = pl.ds(pl.multiple_of(lc * CHUNK, CHUNK), CHUNK)
                st = st_s[d]
                stb = st.astype(BF16)
                r = _dot(stb, mt_s[d, c])
                out_ref[orow, :] = out_ref[orow, :] + _dot_nt(qe_s[d, rows, :], stb)
                st_s[d] = st * gl_s[d, c][0:1, :] + nt_s[d, c] - r
            return carry
        lax.fori_loop(0, n_chunks, step, 0)

    st_s[...] = jnp.zeros_like(st_s)
    ol_ref[...] = jnp.zeros_like(ol_ref)
    oc_ref[...] = jnp.zeros_like(oc_ref)
    prep_segment(0, n_ctx // CHUNK, oc_ref)
    prep_segment(n_ctx // CHUNK, n_lat // CHUNK, ol_ref)
    rec_segment(0, n_ctx // CHUNK, oc_ref)
    rec_segment(n_ctx // CHUNK, n_lat // CHUNK, ol_ref)


def _gdn_call(qkv_l, qkv_c, ab_l, ab_c, conv_w, gparams, bsz, n_lat, n_ctx):
    n_tot = n_lat + n_ctx
    nh = GDN_HEADS
    lat = lambda off: pl.BlockSpec((n_lat, LANES), lambda b, h: (b, off + h))
    ctx = lambda off: pl.BlockSpec((n_ctx, LANES), lambda b, h: (b, off + h))
    cw = lambda off: pl.BlockSpec((CONV_K, LANES), lambda b, h: (0, off + h))
    kern = functools.partial(_gdn_kernel, n_lat=n_lat, n_ctx=n_ctx)
    return pl.pallas_call(
        kern,
        grid=(bsz, nh),
        in_specs=[lat(0), lat(nh), lat(2 * nh), ctx(0), ctx(nh), ctx(2 * nh), cw(0), cw(nh), cw(2 * nh),
                  pl.BlockSpec((None, n_lat, 4), lambda b, h: (h, b, 0)),
                  pl.BlockSpec((None, n_ctx, 4), lambda b, h: (h, b, 0)),
                  pl.BlockSpec((None, 1, 4), lambda b, h: (h, 0, 0))],
        out_specs=[pl.BlockSpec((n_lat, LANES), lambda b, h: (b, h)),
                   pl.BlockSpec((n_ctx, LANES), lambda b, h: (b, h))],
        out_shape=[jax.ShapeDtypeStruct((bsz * n_lat, GDN_W), F32),
                   jax.ShapeDtypeStruct((bsz * n_ctx, GDN_W), F32)],
        scratch_shapes=[
            pltpu.VMEM((n_lat + 2 * SUBLANES, LANES), F32),
            pltpu.VMEM((n_tot, LANES), F32),
            pltpu.VMEM((n_tot, LANES), F32),
            pltpu.VMEM((n_tot, LANES), F32),
            pltpu.VMEM((n_tot, 4), F32),
            pltpu.VMEM((2, n_tot // CHUNK, GDN_DK, LANES), BF16),
            pltpu.VMEM((2, n_tot // CHUNK, LANES, GDN_DK), F32),
            pltpu.VMEM((2, n_tot, LANES), BF16),
            pltpu.VMEM((2, n_tot // CHUNK, SUBLANES, LANES), F32),
            pltpu.VMEM((2, GDN_DK, LANES), F32),
        ],
        compiler_params=_cparams(("parallel", "arbitrary")),
    )(qkv_l, qkv_l, qkv_l, qkv_c, qkv_c, qkv_c, conv_w, conv_w, conv_w, ab_l, ab_c, gparams)


def _lru_kernel(xl_ref, xc_ref, cw_ref, cb_ref, gw_ref, gb_ref, lam_ref, hl_ref, hc_ref,
                pad_s, x_s, a_s, u_s, *, n_lat, n_ctx):
    n_tot = n_ctx + n_lat
    x_s[0:n_ctx, :] = _dwconv(xc_ref, cw_ref, pad_s, n_ctx) + cb_ref[...]
    x_s[n_ctx:n_tot, :] = _dwconv(xl_ref, cw_ref, pad_s, n_lat) + cb_ref[...]

    sp = [_softplus(-lam_ref[:, d * LANES:(d + 1) * LANES]) for d in range(2)]
    tile = min(256, n_ctx)

    def coeffs(t, carry):
        rows = pl.ds(pl.multiple_of(t * tile, tile), tile)
        x = x_s[rows, :]
        gates = _dot(x.astype(BF16), gw_ref[...]) + gb_ref[...]
        for d in range(2):
            r = _sigmoid(gates[:, (2 * d) * LANES:(2 * d + 1) * LANES])
            i = _sigmoid(gates[:, (2 * d + 1) * LANES:(2 * d + 2) * LANES])
            a = jnp.exp(-LRU_C * r * sp[d])
            a_s[d, rows, :] = a
            u_s[d, rows, :] = jnp.sqrt(1.0 - a * a) * (i * x)
        return carry

    lax.fori_loop(0, n_tot // tile, coeffs, 0)

    sub = lax.broadcasted_iota(jnp.int32, (SUBLANES, LANES), 0)

    def group_scan(d, g0):
        rows = pl.ds(pl.multiple_of(g0, SUBLANES), SUBLANES)
        a = a_s[d, rows, :]
        u = u_s[d, rows, :]
        sh = 1
        while sh < SUBLANES:
            if d == 0:
                a_sh, u_sh, ok = pltpu.roll(a, sh, 0), pltpu.roll(u, sh, 0), sub >= sh
            else:
                a_sh, u_sh = pltpu.roll(a, SUBLANES - sh, 0), pltpu.roll(u, SUBLANES - sh, 0)
                ok = sub < SUBLANES - sh
            u = u + a * jnp.where(ok, u_sh, 0.0)
            a = a * jnp.where(ok, a_sh, 1.0)
            sh *= 2
        return rows, u, a

    def fwd(base, n, h0):
        def body(g, h):
            rows, u, a = group_scan(0, base + g * SUBLANES)
            hh = u + a * h
            u_s[0, rows, :] = hh
            return jnp.broadcast_to(hh[SUBLANES - 1:SUBLANES, :], (SUBLANES, LANES))
        return lax.fori_loop(0, n // SUBLANES, body, h0, unroll=8)

    def bwd(base, n, h0, out_ref):
        def body(g, h):
            lg = n // SUBLANES - 1 - g
            rows, u, a = group_scan(1, base + lg * SUBLANES)
            hh = u + a * h
            orow = pl.ds(pl.multiple_of(lg * SUBLANES, SUBLANES), SUBLANES)
            out_ref[orow, :] = u_s[0, rows, :] + hh
            return jnp.broadcast_to(hh[0:1, :], (SUBLANES, LANES))
        return lax.fori_loop(0, n // SUBLANES, body, h0, unroll=8)

    zero = jnp.zeros((SUBLANES, LANES), F32)
    h = fwd(0, n_ctx, zero)
    fwd(n_ctx, n_lat, h)
    h = bwd(0, n_ctx, zero, hc_ref)
    bwd(n_ctx, n_lat, h, hl_ref)


def _lru_call(xr_l, xr_c, conv_w, conv_b, gate_w, gate_b, lam, bsz, n_lat, n_ctx):
    n_tot = n_lat + n_ctx
    nt = LRU_W // LANES
    kern = functools.partial(_lru_kernel, n_lat=n_lat, n_ctx=n_ctx)
    return pl.pallas_call(
        kern,
        grid=(bsz, nt),
        in_specs=[pl.BlockSpec((n_lat, LANES), lambda b, j: (b, j)),
                  pl.BlockSpec((n_ctx, LANES), lambda b, j: (b, j)),
                  pl.BlockSpec((CONV_K, LANES), lambda b, j: (0, j)),
                  pl.BlockSpec((1, LANES), lambda b, j: (0, j)),
                  pl.BlockSpec((None, LANES, 4 * LANES), lambda b, j: (j, 0, 0)),
                  pl.BlockSpec((None, 1, 4 * LANES), lambda b, j: (j, 0, 0)),
                  pl.BlockSpec((None, 1, 2 * LANES), lambda b, j: (j, 0, 0))],
        out_specs=[pl.BlockSpec((n_lat, LANES), lambda b, j: (b, j)),
                   pl.BlockSpec((n_ctx, LANES), lambda b, j: (b, j))],
        out_shape=[jax.ShapeDtypeStruct((bsz * n_lat, LRU_W), F32),
                   jax.ShapeDtypeStruct((bsz * n_ctx, LRU_W), F32)],
        scratch_shapes=[pltpu.VMEM((n_lat + 2 * SUBLANES, LANES), F32),
                        pltpu.VMEM((n_tot, LANES), F32),
                        pltpu.VMEM((2, n_tot, LANES), F32),
                        pltpu.VMEM((2, n_tot, LANES), F32)],
        compiler_params=_cparams(("parallel", "arbitrary")),
    )(xr_l, xr_c, conv_w, conv_b, gate_w, gate_b, lam)


def _residual_ln(h, y, mod_ref, gate_row, lng_ref, lnb_ref, alpha):
    return _layernorm(alpha * h + mod_ref[gate_row:gate_row + 1, :] * y, lng_ref[...], lnb_ref[...])


FF_TILE = 1024


def _mlp_residual(h, mod_ref, w1_ref, w2_ref, lng_ref, lnb_ref, alpha):
    u = _modulate(h, mod_ref, 3, 4).astype(BF16)
    acc = None
    for j in range(D_FF // FF_TILE):
        sl = slice(j * FF_TILE, (j + 1) * FF_TILE)
        a = jnp.maximum(_dot(u, w1_ref[:, sl]), 0.0)
        part = _dot((a * a).astype(BF16), w2_ref[sl, :])
        acc = part if acc is None else acc + part
    return _residual_ln(h, acc, mod_ref, 5, lng_ref, lnb_ref, alpha)


def _post_even_kernel(o_ref, z_ref, hr_ref, gate_ref, h_ref, mod_ref, gn_ref, w_ref, lng0_ref, lnb0_ref,
                      w1_ref, w2_ref, lng1_ref, lnb1_ref, out_ref, *, alpha):
    parts = []
    for hd in range(GDN_HEADS):
        sl = slice(hd * LANES, (hd + 1) * LANES)
        o = o_ref[:, sl]
        on = o * lax.rsqrt(jnp.mean(o * o, axis=-1, keepdims=True) + NORM_EPS) * gn_ref[...]
        parts.append(on * _silu(z_ref[:, sl]))
    y_gdn = jnp.concatenate(parts, axis=1).astype(BF16)
    y_lru = (hr_ref[...] * _gelu_tanh(gate_ref[...])).astype(BF16)
    y = _dot(y_gdn, w_ref[0:GDN_W, :]) + _dot(y_lru, w_ref[GDN_W:, :])
    h1 = _residual_ln(h_ref[...], y, mod_ref, 2, lng0_ref, lnb0_ref, alpha)
    out_ref[...] = _mlp_residual(h1, mod_ref, w1_ref, w2_ref, lng1_ref, lnb1_ref, alpha)


def _post_odd_kernel(y_ref, h_ref, mod_ref, w_ref, lng0_ref, lnb0_ref, w1_ref, w2_ref, lng1_ref, lnb1_ref,
                     out_ref, *, alpha):
    y = _dot(y_ref[...], w_ref[...])
    h1 = _residual_ln(h_ref[...], y, mod_ref, 2, lng0_ref, lnb0_ref, alpha)
    out_ref[...] = _mlp_residual(h1, mod_ref, w1_ref, w2_ref, lng1_ref, lnb1_ref, alpha)


def _post_call(kern, token_inputs, h, mod, layer, per_batch, mixer_consts, tail_consts, alpha):
    n = h.shape[0]
    tm = _token_tile(n, per_batch)
    rpm = None if per_batch is None else per_batch // tm
    tok = lambda w: pl.BlockSpec((tm, w), lambda t: (t, 0))
    consts = list(mixer_consts) + list(tail_consts)
    const_specs = [_const_spec(a.shape) if a.ndim == 2 else
                   pl.BlockSpec((None,) + a.shape[1:], lambda t: (layer, 0, 0), pipeline_mode=pl.Buffered(1))
                   for a in consts]
    return pl.pallas_call(
        functools.partial(kern, alpha=alpha),
        grid=(n // tm,),
        in_specs=[tok(a.shape[1]) for a in token_inputs] + [tok(D_MODEL), _mod_spec(layer, rpm)]
                 + const_specs,
        out_specs=tok(D_MODEL),
        out_shape=jax.ShapeDtypeStruct((n, D_MODEL), F32),
        compiler_params=_cparams(("parallel",)),
    )(*token_inputs, h, mod, *consts)


def _qkv_rope_kernel(h_ref, mod_ref, wq_ref, wk_ref, wvt_ref, cos_ref, sin_ref, q_ref, k_ref, vt_ref):
    u = _modulate(h_ref[...], mod_ref, 0, 1).astype(BF16)
    cos = cos_ref[...]
    sin_signed = sin_ref[...]
    lane = lax.broadcasted_iota(jnp.int32, cos.shape, 1)
    partner_above = (lane % (2 * ROPE_Q)) < ROPE_Q

    def rope(y):
        parts = []
        for hd in range(DIFF_HEADS):
            t = y[:, hd * LANES:(hd + 1) * LANES]
            partner = jnp.where(partner_above, pltpu.roll(t, LANES - ROPE_Q, 1), pltpu.roll(t, ROPE_Q, 1))
            parts.append(t * cos + partner * sin_signed)
        return jnp.concatenate(parts, axis=1).astype(BF16)

    q_ref[...] = rope(_dot(u, wq_ref[...]))
    k_ref[...] = rope(_dot(u, wk_ref[...]))
    vt_ref[...] = _dot_nt(wvt_ref[...], u).astype(BF16)


def _qkv_plain_kernel(h_ref, mod_ref, wq_ref, wk_ref, wvt_ref, q_ref, k_ref, vt_ref):
    u = _modulate(h_ref[...], mod_ref, 0, 1).astype(BF16)
    q_ref[...] = _dot(u, wq_ref[...]).astype(BF16)
    k_ref[...] = _dot(u, wk_ref[...]).astype(BF16)
    vt_ref[...] = _dot_nt(wvt_ref[...], u).astype(BF16)


def _qkv_call(h, mod, layer, per_batch, weights, rope):
    n = h.shape[0]
    tm = _token_tile(n, per_batch)
    tok = pl.BlockSpec((tm, D_MODEL), lambda t: (t, 0))
    out_specs = [tok, tok, pl.BlockSpec((D_MODEL, tm), lambda t: (0, t))]
    out_shape = [jax.ShapeDtypeStruct((n, D_MODEL), BF16), jax.ShapeDtypeStruct((n, D_MODEL), BF16),
                 jax.ShapeDtypeStruct((D_MODEL, n), BF16)]
    wspec = _const_spec((D_MODEL, D_MODEL))
    if rope is not None:
        cos2, sin2 = rope
        rpm = per_batch // tm
        tab = pl.BlockSpec((tm, LANES), lambda t: (t % rpm, 0))
        wq, wk, wvt = weights
        return pl.pallas_call(
            _qkv_rope_kernel, grid=(n // tm,),
            in_specs=[tok, _mod_spec(layer, rpm), wspec, wspec, wspec, tab, tab],
            out_specs=out_specs, out_shape=out_shape,
            compiler_params=_cparams(("parallel",)),
        )(h, mod, wq, wk, wvt, cos2, sin2)
    wq, wk, wvt = weights
    return pl.pallas_call(
        _qkv_plain_kernel, grid=(n // tm,),
        in_specs=[tok, _mod_spec(layer, None), wspec, wspec, wspec],
        out_specs=out_specs, out_shape=out_shape,
        compiler_params=_cparams(("parallel",)),
    )(h, mod, wq, wk, wvt)


KEY_TILE = 512
SCORE_LOOKAHEAD = 1
Q_TILE = 1024


def _attn_kernel(*refs, key_lens, lam_init):
    nseg = len(key_lens)
    q_ref = refs[0]
    k_refs = refs[1:1 + nseg]
    vt_refs = refs[1 + nseg:1 + 2 * nseg]
    lam_ref, subln_ref, y_ref, acc_s = refs[1 + 2 * nseg:]
    tq = q_ref.shape[0]

    q = q_ref[...]
    lane = lax.broadcasted_iota(jnp.int32, q.shape, 1)
    zero = jnp.zeros_like(q)
    qm = [jnp.where(lane < DIFF_D, q, zero), jnp.where(lane >= DIFF_D, q, zero)]
    acc_s[...] = jnp.zeros_like(acc_s)

    tiles = []
    for k_ref, vt_ref, n_keys in zip(k_refs, vt_refs, key_lens):
        tk = min(KEY_TILE, n_keys)
        tiles += [(k_ref, vt_ref, j * tk, tk) for j in range(n_keys // tk)]

    def tile_scores(t):
        k_ref, _, k0, tk = tiles[t]
        k_t = k_ref[k0:k0 + tk, :]
        return [_dot_nt(k_t, qm[0]), _dot_nt(k_t, qm[1])]

    def absorb(s_pair, t, stats):
        _, vt_ref, k0, tk = tiles[t]
        vt_t = vt_ref[:, k0:k0 + tk]
        new = []
        for m in range(2):
            m_old, l_old = stats[m]
            s_t = s_pair[m]
            m_new = jnp.maximum(m_old, jnp.max(s_t, axis=0, keepdims=True))
            alpha = jnp.exp2(m_old - m_new)
            p = jnp.exp2(s_t - m_new)
            l_new = alpha * l_old + jnp.sum(p, axis=0, keepdims=True)
            acc_s[m] = alpha * acc_s[m] + _dot(vt_t, p.astype(BF16))
            new.append((m_new, l_new))
        return tuple(new)

    init = (jnp.full((1, tq), -1e30, F32), jnp.zeros((1, tq), F32))
    stats = (init, init)
    ahead = [tile_scores(t) for t in range(min(SCORE_LOOKAHEAD, len(tiles)))]
    for t in range(len(tiles)):
        if t + SCORE_LOOKAHEAD < len(tiles):
            ahead.append(tile_scores(t + SCORE_LOOKAHEAD))
        stats = absorb(ahead.pop(0), t, stats)

    lv = lam_ref[...]
    lam = (jnp.exp(jnp.sum(lv[0:1] * lv[1:2], axis=1, keepdims=True))
           - jnp.exp(jnp.sum(lv[2:3] * lv[3:4], axis=1, keepdims=True)) + lam_init)
    o_t = acc_s[0] / stats[0][1] - lam * (acc_s[1] / stats[1][1])
    ms = jnp.mean(o_t * o_t, axis=0, keepdims=True)
    y_t = o_t * lax.rsqrt(ms + NORM_EPS) * subln_ref[...] * (1.0 - lam_init)
    y_ref[...] = y_t.T.astype(BF16)


def _attn_call(q, ks, vts, lam_vec, subln_col, bsz, n_q, key_lens, lam_init):
    tq = min(Q_TILE, n_q)
    nq = n_q // tq
    nseg = len(key_lens)
    in_specs = [pl.BlockSpec((tq, LANES), lambda b, h, i: (b * nq + i, h))]
    in_specs += [pl.BlockSpec((n, LANES), lambda b, h, i: (b, h)) for n in key_lens]
    in_specs += [pl.BlockSpec((LANES, n), lambda b, h, i: (h, b)) for n in key_lens]
    in_specs += [pl.BlockSpec((4, DIFF_D), lambda b, h, i: (0, 0)),
                 pl.BlockSpec((DIFF_DV, 1), lambda b, h, i: (0, 0))]
    kern = functools.partial(_attn_kernel, key_lens=tuple(key_lens), lam_init=lam_init)
    return pl.pallas_call(
        kern,
        grid=(bsz, DIFF_HEADS, nq),
        in_specs=in_specs,
        out_specs=pl.BlockSpec((tq, LANES), lambda b, h, i: (b * nq + i, h)),
        out_shape=jax.ShapeDtypeStruct((bsz * n_q, D_MODEL), BF16),
        scratch_shapes=[pltpu.VMEM((2, DIFF_DV, tq), F32)],
        compiler_params=_cparams(("parallel", "parallel", "arbitrary")),
    )(q, *ks, *vts, lam_vec, subln_col)


def _rope_tables(n_lat):
    rows = n_lat // GRID_W
    row = jnp.repeat(jnp.arange(rows), GRID_W).astype(F32)
    col = jnp.tile(jnp.arange(GRID_W), rows).astype(F32)
    half = DIFF_D // 2
    inv = ROPE_THETA ** (-(jnp.arange(0, half, 2, dtype=F32) / half))
    ang_r = row[:, None] * inv
    ang_c = col[:, None] * inv
    ang = jnp.concatenate([ang_r, ang_r, ang_c, ang_c], axis=-1)
    cos, sin = jnp.cos(ang), jnp.sin(ang)
    sign = jnp.where((jnp.arange(DIFF_D) % (2 * ROPE_Q)) < ROPE_Q, -1.0, 1.0).astype(F32)
    sin_signed = sin * sign
    return jnp.concatenate([cos, cos], axis=-1), jnp.concatenate([sin_signed, sin_signed], axis=-1)


def _lru_gate_weights(gate_w, gate_b, lam):
    nt = LRU_W // LANES
    per = LANES // LRU_BW
    blocks = gate_w.reshape(2, 2, nt, per, LRU_BW, LRU_BW)
    eye = jnp.eye(per, dtype=gate_w.dtype)
    dense = jnp.einsum('dgtpab,pq->dgtpaqb', blocks, eye).reshape(2, 2, nt, LANES, LANES)
    w = jnp.transpose(dense, (2, 3, 0, 1, 4)).reshape(nt, LANES, 4 * LANES)
    b = jnp.transpose(gate_b.reshape(2, 2, nt, LANES), (2, 0, 1, 3)).reshape(nt, 1, 4 * LANES)
    lm = jnp.transpose(lam.reshape(2, nt, LANES), (1, 0, 2)).reshape(nt, 1, 2 * LANES)
    return w.astype(BF16), b, lm


def _gate_columns_per_head(w_ab):
    d = w_ab.shape[0]
    per_head = jnp.transpose(w_ab.reshape(d, 4, GDN_HEADS), (0, 2, 1)).reshape(d, 4 * GDN_HEADS)
    return jnp.pad(per_head, ((0, 0), (0, LANES - 4 * GDN_HEADS)))


def _cast_kernel(x_ref, o_ref):
    o_ref[...] = x_ref[...].astype(BF16)


def _cast_bf16(w):
    n, r, c = w.shape
    tr = max(t for t in (512, 256, 128, 64, 32, 16) if r % t == 0 and t * c * 4 <= 4 * 1024 * 1024)
    return pl.pallas_call(
        _cast_kernel,
        grid=(n, r // tr),
        in_specs=[pl.BlockSpec((None, tr, c), lambda i, j: (i, j, 0))],
        out_specs=pl.BlockSpec((None, tr, c), lambda i, j: (i, j, 0)),
        out_shape=jax.ShapeDtypeStruct(w.shape, BF16),
        compiler_params=_cparams(("parallel", "parallel")),
    )(w)


def kernel(x, c, ctx, c_ctx, ada_w, ada_b, ln_g, ln_b, mlp_w1, mlp_w2, mix_w_out, ev_w_in, ev_qkv_conv,
           ev_a_log, ev_dt_bias, ev_gdn_norm, ev_lru_conv_w, ev_lru_conv_b, ev_lru_gate_w, ev_lru_gate_b,
           ev_lru_lambda, od_w_qkv, od_lambda, od_subln):
    bsz, n_lat, d = x.shape
    n_ctx = ctx.shape[1]
    depth = ada_w.shape[0]
    assert d == D_MODEL and bsz + 1 <= MOD_ROWS
    assert n_lat % CHUNK == 0 and n_ctx % CHUNK == 0 and n_lat % GRID_W == 0
    alpha = (2 * depth) ** 0.25

    h_lat = x.reshape(bsz * n_lat, d)
    h_ctx = ctx.reshape(bsz * n_ctx, d)
    cc = jnp.concatenate([c_ctx[None, :], c, jnp.zeros((MOD_ROWS - 1 - bsz, d), F32)], axis=0)
    mod = _ada_call(cc, ada_w, ada_b).reshape(depth, MOD_ROWS, 6, d)
    cos2, sin2 = _rope_tables(n_lat)
    w1_all, w2_all, w_out_all = _cast_bf16(mlp_w1), _cast_bf16(mlp_w2), _cast_bf16(mix_w_out)

    for i in range(depth):
        last = i == depth - 1
        j = i // 2
        lng0, lnb0 = ln_g[i, 0][None, :], ln_b[i, 0][None, :]
        lng1, lnb1 = ln_g[i, 1][None, :], ln_b[i, 1][None, :]
        tail = (w1_all, w2_all, lng1, lnb1)
        if i % 2 == 0:
            w_in = ev_w_in[j]
            c0, c1, c2 = 4 * GDN_W, 4 * GDN_W + 4 * GDN_HEADS, 4 * GDN_W + 4 * GDN_HEADS + 2 * LRU_W
            w_main = jnp.concatenate([w_in[:, :c0], w_in[:, c1:c2]], axis=1).astype(BF16)
            w_ab = _gate_columns_per_head(w_in[:, c0:c1]).astype(BF16)
            qkv_l, z_l, xr_l, gate_l, ab_l = _inproj_even_call(h_lat, mod, i, n_lat, w_main, w_ab)
            qkv_c, z_c, xr_c, gate_c, ab_c = _inproj_even_call(h_ctx, mod, i, None, w_main, w_ab)
            gparams = jnp.stack([ev_a_log[j, 0], ev_a_log[j, 1], ev_dt_bias[j, 0], ev_dt_bias[j, 1]],
                                axis=-1).reshape(GDN_HEADS, 1, 4)
            o_l, o_c = _gdn_call(qkv_l, qkv_c, ab_l, ab_c, ev_qkv_conv[j], gparams, bsz, n_lat, n_ctx)
            gw, gb, lm = _lru_gate_weights(ev_lru_gate_w[j], ev_lru_gate_b[j], ev_lru_lambda[j])
            hr_l, hr_c = _lru_call(xr_l, xr_c, ev_lru_conv_w[j], ev_lru_conv_b[j][None, :], gw, gb, lm,
                                   bsz, n_lat, n_ctx)
            gn = ev_gdn_norm[j][None, :]
            mixer = (gn, w_out_all, lng0, lnb0)
            h_lat = _post_call(_post_even_kernel, (o_l, z_l, hr_l, gate_l), h_lat, mod, i, n_lat, mixer, tail,
                               alpha)
            if not last:
                h_ctx = _post_call(_post_even_kernel, (o_c, z_c, hr_c, gate_c), h_ctx, mod, i, None, mixer,
                                   tail, alpha)
        else:
            lam_init = 0.8 - 0.6 * math.exp(-0.3 * i)
            wq = od_w_qkv[j][:, :D_MODEL] * (DIFF_D ** -0.5 * math.log2(math.e))
            wk = od_w_qkv[j][:, D_MODEL:2 * D_MODEL]
            wvt = od_w_qkv[j][:, 2 * D_MODEL:].T.astype(BF16)
            qkv_w = (wq.astype(BF16), wk.astype(BF16), wvt)
            q_l, k_l, vt_l = _qkv_call(h_lat, mod, i, n_lat, qkv_w, (cos2, sin2))
            q_c, k_c, vt_c = _qkv_call(h_ctx, mod, i, None, qkv_w, None)
            subln_col = od_subln[j][:, None]
            y_l = _attn_call(q_l, [k_c, k_l], [vt_c, vt_l], od_lambda[j], subln_col, bsz, n_lat,
                             [n_ctx, n_lat], lam_init)
            mixer = (w_out_all, lng0, lnb0)
            h_lat = _post_call(_post_odd_kernel, (y_l,), h_lat, mod, i, n_lat, mixer, tail, alpha)
            if not last:
                y_c = _attn_call(q_c, [k_c], [vt_c], od_lambda[j], subln_col, bsz, n_ctx, [n_ctx], lam_init)
                h_ctx = _post_call(_post_odd_kernel, (y_c,), h_ctx, mod, i, None, mixer, tail, alpha)
    return h_lat.reshape(bsz, n_lat, d)
```

```python
import functools
import math

import jax
import jax.numpy as jnp
from jax import lax
from jax.experimental import pallas as pl
from jax.experimental.pallas import tpu as pltpu

F32 = jnp.float32
BF16 = jnp.bfloat16

D_MODEL = 1024
D_FF = 4 * D_MODEL
CONV_K = 4
CONV_LEFT = 2
GDN_HEADS = 4
GDN_DK = 128
GDN_W = GDN_HEADS * GDN_DK
CHUNK = 64
LRU_W = D_MODEL - GDN_W
LRU_BLOCKS = 8
LRU_BW = LRU_W // LRU_BLOCKS
LRU_C = 8.0
DIFF_HEADS = 8
DIFF_D = 64
DIFF_DV = 128
ROPE_Q = DIFF_D // 4
GRID_W = 64
ROPE_THETA = 10000.0
NORM_EPS = 1e-6
LANES = 128
SUBLANES = 8
MOD_ROWS = 16
VMEM_LIMIT = 56 * 1024 * 1024


def _cparams(sem):
    return pltpu.CompilerParams(dimension_semantics=sem, vmem_limit_bytes=VMEM_LIMIT)


def _sigmoid(x):
    return 0.5 * jnp.tanh(0.5 * x) + 0.5


def _silu(x):
    return x * _sigmoid(x)


def _softplus(x):
    return jnp.maximum(x, 0.0) + jnp.log1p(jnp.exp(-jnp.abs(x)))


def _gelu_tanh(x):
    return 0.5 * x * (1.0 + jnp.tanh(math.sqrt(2.0 / math.pi) * (x + 0.044715 * (x * x * x))))


def _modulate(h, mod_ref, shift_row, scale_row):
    return h * (1.0 + mod_ref[scale_row:scale_row + 1, :]) + mod_ref[shift_row:shift_row + 1, :]


def _layernorm(x, g, b):
    mu = jnp.mean(x, axis=-1, keepdims=True)
    xc = x - mu
    var = jnp.mean(xc * xc, axis=-1, keepdims=True)
    return xc * lax.rsqrt(var + NORM_EPS) * g + b


def _dot(a, b):
    return jnp.dot(a, b, preferred_element_type=F32)


def _dot_nt(a, b):
    return lax.dot_general(a, b, (((1,), (1,)), ((), ())), preferred_element_type=F32)


def _dot_tn(a, b):
    return lax.dot_general(a, b, (((0,), (0,)), ((), ())), preferred_element_type=F32)


def _ada_kernel(c_ref, w_ref, b_ref, o_ref):
    s = _silu(c_ref[...])
    o_ref[...] = jnp.dot(s, w_ref[...], preferred_element_type=F32,
                         precision=lax.Precision.HIGHEST) + b_ref[...]


def _ada_call(cc, ada_w, ada_b):
    depth, d, n = ada_w.shape
    tn = D_MODEL
    return pl.pallas_call(
        _ada_kernel,
        grid=(depth, n // tn),
        in_specs=[pl.BlockSpec((MOD_ROWS, d), lambda i, j: (0, 0)),
                  pl.BlockSpec((None, d, tn), lambda i, j: (i, 0, j)),
                  pl.BlockSpec((None, 1, tn), lambda i, j: (i, 0, j))],
        out_specs=pl.BlockSpec((None, MOD_ROWS, tn), lambda i, j: (i, 0, j)),
        out_shape=jax.ShapeDtypeStruct((depth, MOD_ROWS, n), F32),
        compiler_params=_cparams(("parallel", "parallel")),
    )(cc, ada_w, ada_b.reshape(depth, 1, n))


def _mod_spec(layer, rows_per_mod):
    if rows_per_mod is None:
        return pl.BlockSpec((None, None, 6, D_MODEL), lambda t: (layer, 0, 0, 0))
    return pl.BlockSpec((None, None, 6, D_MODEL), lambda t: (layer, 1 + t // rows_per_mod, 0, 0))


def _const_spec(shape):
    nd = len(shape)
    return pl.BlockSpec(shape, lambda t: (0,) * nd, pipeline_mode=pl.Buffered(1))


def _token_tile(n_rows, per_batch):
    tm = min(512, n_rows if per_batch is None else per_batch)
    assert n_rows % tm == 0 and (per_batch is None or per_batch % tm == 0)
    return tm


def _inproj_even_kernel(h_ref, mod_ref, w_ref, wab_ref, qkv_ref, z_ref, xr_ref, gate_ref, ab_ref):
    u = _modulate(h_ref[...], mod_ref, 0, 1).astype(BF16)
    qkv_ref[...] = _dot(u, w_ref[:, 0:3 * GDN_W])
    z_ref[...] = _dot(u, w_ref[:, 3 * GDN_W:4 * GDN_W])
    xr_ref[...] = _dot(u, w_ref[:, 4 * GDN_W:4 * GDN_W + LRU_W])
    gate_ref[...] = _dot(u, w_ref[:, 4 * GDN_W + LRU_W:])
    ab = _dot(u, wab_ref[...])
    for hd in range(GDN_HEADS):
        ab_ref[hd] = ab[:, 4 * hd:4 * hd + 4]


def _inproj_even_call(h, mod, layer, per_batch, w_main, w_ab):
    n = h.shape[0]
    tm = _token_tile(n, per_batch)
    rpm = None if per_batch is None else per_batch // tm
    tok = lambda w: pl.BlockSpec((tm, w), lambda t: (t, 0))
    return pl.pallas_call(
        _inproj_even_kernel,
        grid=(n // tm,),
        in_specs=[tok(D_MODEL), _mod_spec(layer, rpm), _const_spec(w_main.shape), _const_spec(w_ab.shape)],
        out_specs=[tok(3 * GDN_W), tok(GDN_W), tok(LRU_W), tok(LRU_W),
                   pl.BlockSpec((GDN_HEADS, tm, 4), lambda t: (0, t, 0))],
        out_shape=[jax.ShapeDtypeStruct((n, 3 * GDN_W), F32), jax.ShapeDtypeStruct((n, GDN_W), F32),
                   jax.ShapeDtypeStruct((n, LRU_W), F32), jax.ShapeDtypeStruct((n, LRU_W), F32),
                   jax.ShapeDtypeStruct((GDN_HEADS, n, 4), F32)],
        compiler_params=_cparams(("parallel",)),
    )(h, mod, w_main, w_ab)


def _dwconv(x_ref, w_ref, pad_ref, n):
    zeros = jnp.zeros((SUBLANES, LANES), F32)
    pad_ref[0:SUBLANES, :] = zeros
    pad_ref[SUBLANES + n:2 * SUBLANES + n, :] = zeros
    pad_ref[SUBLANES:SUBLANES + n, :] = x_ref[...]
    y = None
    for j in range(CONV_K):
        off = SUBLANES + j - CONV_LEFT
        term = pad_ref[off:off + n, :] * w_ref[j:j + 1, :]
        y = term if y is None else y + term
    return y


def _hp_parts(x):
    hi = x.astype(BF16).astype(F32)
    return hi, x - hi


def _hp_operands(x):
    hi, lo = _hp_parts(x)
    x2 = hi + pltpu.roll(lo, CHUNK, 1)
    lhs = jnp.concatenate([x2, x2], axis=1).astype(BF16)
    rhs = jnp.concatenate([hi, hi, lo, lo], axis=0).astype(BF16)
    return lhs, rhs


def _gdn_kernel(ql_ref, kl_ref, vl_ref, qc_ref, kc_ref, vc_ref, wq_ref, wk_ref, wv_ref,
                abl_ref, abc_ref, gp_ref, ol_ref, oc_ref,
                pad_s, q_s, k_s, v_s, ab_s, mt_s, nt_s, qe_s, gl_s, st_s, *, n_lat, n_ctx):
    n_tot = n_ctx + n_lat

    def prep(x_ref, w_ref, n, dst, base, kind):
        y = _silu(_dwconv(x_ref, w_ref, pad_s, n))
        if kind != "v":
            y = y * lax.rsqrt(jnp.sum(y * y, axis=-1, keepdims=True) + NORM_EPS)
        if kind == "q":
            y = y * (GDN_DK ** -0.5)
        dst[base:base + n, :] = y

    prep(qc_ref, wq_ref, n_ctx, q_s, 0, "q")
    prep(kc_ref, wk_ref, n_ctx, k_s, 0, "k")
    prep(vc_ref, wv_ref, n_ctx, v_s, 0, "v")
    prep(ql_ref, wq_ref, n_lat, q_s, n_ctx, "q")
    prep(kl_ref, wk_ref, n_lat, k_s, n_ctx, "k")
    prep(vl_ref, wv_ref, n_lat, v_s, n_ctx, "v")
    ab_s[0:n_ctx, :] = abc_ref[...]
    ab_s[n_ctx:n_tot, :] = abl_ref[...]

    row = lax.broadcasted_iota(jnp.int32, (CHUNK, LANES), 0)
    col = lax.broadcasted_iota(jnp.int32, (CHUNK, LANES), 1)
    left = col < CHUNK
    eye = jnp.where(row == col, 1.0, 0.0).astype(F32)
    incl = [jnp.logical_and(row >= col, left), jnp.logical_and(row <= col, left)]
    strict = [jnp.logical_and(row > col, left), jnp.logical_and(row < col, left)]
    ones_left = jnp.where(left, 1.0, 0.0).astype(F32)
    gp = gp_ref[...]

    def chunk_group(chunk0, out_ref, group, gi):
        chains = []
        for g in range(group):
            lc = gi * group + g
            c = chunk0 + lc
            rows = pl.ds(pl.multiple_of(c * CHUNK, CHUNK), CHUNK)
            orow = pl.ds(pl.multiple_of(lc * CHUNK, CHUNK), CHUNK)
            q_c = q_s[rows, :]
            k_c = k_s[rows, :]
            v_c = v_s[rows, :]
            ab_c = ab_s[rows, :]
            kb = k_c.astype(BF16)
            k_pad = jnp.concatenate([kb, jnp.zeros_like(kb)], axis=0)
            qk_kk = _dot_nt(jnp.concatenate([q_c.astype(BF16), kb], axis=0), k_pad)
            for d in range(2):
                chains.append(dict(c=c, rows=rows, orow=orow, d=d, q=q_c, k=k_c, v=v_c, ab=ab_c,
                                   qk_raw=qk_kk[0:CHUNK], kk=qk_kk[CHUNK:2 * CHUNK]))
        for ch in chains:
            d = ch["d"]
            a_col = ch["ab"][:, d:d + 1]
            b_col = ch["ab"][:, 2 + d:3 + d]
            g_col = -jnp.exp(gp[:, d:d + 1]) * _softplus(a_col + gp[:, 2 + d:3 + d])
            ch["beta"] = _sigmoid(b_col)
            g_b = jnp.broadcast_to(g_col, (CHUNK, LANES))
            m_t = incl[1 - d]
            gh, gl = _hp_parts(g_b)
            cum = jnp.where(incl[d], 1.0, 0.0).astype(F32)
            lhs1 = cum + pltpu.roll(ones_left, CHUNK, 1)
            lhs = jnp.concatenate([lhs1, lhs1], axis=1).astype(BF16)
            rhs = jnp.concatenate([gh, jnp.where(m_t, -gh, 0.0), gl, jnp.where(m_t, -gl, 0.0)],
                                  axis=0).astype(BF16)
            ch["diff"] = _dot(lhs, rhs)
            ch["tot"] = jnp.sum(g_b, axis=0, keepdims=True)
        for ch in chains:
            d = ch["d"]
            ch["decay"] = jnp.exp(jnp.where(incl[d], ch["diff"], -1e30))
            a_mat = jnp.where(strict[d], ch["beta"] * ch["kk"] * ch["decay"], 0.0)
            ch["p"] = eye - a_mat
            ch["a"] = a_mat
        for ch in chains:
            a_lhs, a_rhs = _hp_operands(ch["a"])
            ch["q2"] = _dot(a_lhs, a_rhs)
        sq = 2
        while sq * 2 < CHUNK:
            for ch in chains:
                q_lhs, q_rhs = _hp_operands(ch["q2"])
                p_lhs, _ = _hp_operands(ch["p"])
                r = _dot(jnp.concatenate([p_lhs, q_lhs], axis=0), q_rhs)
                ch["p"] = ch["p"] + r[0:CHUNK]
                ch["q2"] = r[CHUNK:2 * CHUNK]
            sq *= 2
        for ch in chains:
            ch["t"] = ch["p"] + _dot(_hp_operands(ch["p"])[0], _hp_operands(ch["q2"])[1])
        for ch in chains:
            gam = jnp.where(left, pltpu.roll(ch["diff"], CHUNK, 1), ch["diff"])
            egam = jnp.exp(gam)
            beta = ch["beta"]
            rhs_uw = jnp.concatenate([ch["k"] * (beta * egam), ch["v"] * beta], axis=1).astype(BF16)
            ch["wu"] = _dot(ch["t"][:, 0:CHUNK].astype(BF16), rhs_uw).astype(BF16)
            ch["qg"] = ch["q"] * egam
            ch["kd"] = (ch["k"] * jnp.exp(ch["tot"] - gam)).astype(BF16)
        for ch in chains:
            d, c = ch["d"], ch["c"]
            qk_m = (ch["qk_raw"] * ch["decay"])[:, 0:CHUNK].astype(BF16)
            x = _dot(qk_m, ch["wu"])
            mn = _dot_tn(ch["wu"], ch["kd"])
            qe_s[d, ch["rows"], :] = (ch["qg"] - x[:, 0:LANES]).astype(BF16)
            out_ref[ch["orow"], :] = out_ref[ch["orow"], :] + x[:, LANES:2 * LANES]
            mt_s[d, c] = mn[0:LANES].astype(BF16)
            nt_s[d, c] = mn[LANES:2 * LANES]
            gl_s[d, c] = jnp.broadcast_to(jnp.exp(ch["tot"]), (SUBLANES, LANES))

    def prep_segment(chunk0, n_chunks, out_ref):
        group = max(g for g in (8, 4, 2, 1) if n_chunks % g == 0)

        def prep(gi, carry):
            chunk_group(chunk0, out_ref, group, gi)
            return carry
        lax.fori_loop(0, n_chunks // group, prep, 0)

    def rec_segment(chunk0, n_chunks, out_ref):
        def step(s, carry):
            for d in range(2):
                lc = s if d == 0 else n_chunks - 1 - s
                c = chunk0 + lc
                rows = pl.ds(pl.multiple_of(c * CHUNK, CHUNK), CHUNK)
                orow = pl.ds(pl.multiple_of(lc * CHUNK, CHUNK), CHUNK)
                st = st_s[d]
                stb = st.astype(BF16)
                r = _dot(stb, mt_s[d, c])
                out_ref[orow, :] = out_ref[orow, :] + _dot_nt(qe_s[d, rows, :], stb)
                st_s[d] = st * gl_s[d, c][0:1, :] + nt_s[d, c] - r
            return carry
        lax.fori_loop(0, n_chunks, step, 0)

    st_s[...] = jnp.zeros_like(st_s)
    ol_ref[...] = jnp.zeros_like(ol_ref)
    oc_ref[...] = jnp.zeros_like(oc_ref)
    prep_segment(0, n_ctx // CHUNK, oc_ref)
    prep_segment(n_ctx // CHUNK, n_lat // CHUNK, ol_ref)
    rec_segment(0, n_ctx // CHUNK, oc_ref)
    rec_segment(n_ctx // CHUNK, n_lat // CHUNK, ol_ref)


def _gdn_call(qkv_l, qkv_c, ab_l, ab_c, conv_w, gparams, bsz, n_lat, n_ctx):
    n_tot = n_lat + n_ctx
    nh = GDN_HEADS
    lat = lambda off: pl.BlockSpec((n_lat, LANES), lambda b, h: (b, off + h))
    ctx = lambda off: pl.BlockSpec((n_ctx, LANES), lambda b, h: (b, off + h))
    cw = lambda off: pl.BlockSpec((CONV_K, LANES), lambda b, h: (0, off + h))
    kern = functools.partial(_gdn_kernel, n_lat=n_lat, n_ctx=n_ctx)
    return pl.pallas_call(
        kern,
        grid=(bsz, nh),
        in_specs=[lat(0), lat(nh), lat(2 * nh), ctx(0), ctx(nh), ctx(2 * nh), cw(0), cw(nh), cw(2 * nh),
                  pl.BlockSpec((None, n_lat, 4), lambda b, h: (h, b, 0)),
                  pl.BlockSpec((None, n_ctx, 4), lambda b, h: (h, b, 0)),
                  pl.BlockSpec((None, 1, 4), lambda b, h: (h, 0, 0))],
        out_specs=[pl.BlockSpec((n_lat, LANES), lambda b, h: (b, h)),
                   pl.BlockSpec((n_ctx, LANES), lambda b, h: (b, h))],
        out_shape=[jax.ShapeDtypeStruct((bsz * n_lat, GDN_W), F32),
                   jax.ShapeDtypeStruct((bsz * n_ctx, GDN_W), F32)],
        scratch_shapes=[
            pltpu.VMEM((n_lat + 2 * SUBLANES, LANES), F32),
            pltpu.VMEM((n_tot, LANES), F32),
            pltpu.VMEM((n_tot, LANES), F32),
            pltpu.VMEM((n_tot, LANES), F32),
            pltpu.VMEM((n_tot, 4), F32),
            pltpu.VMEM((2, n_tot // CHUNK, GDN_DK, LANES), BF16),
            pltpu.VMEM((2, n_tot // CHUNK, LANES, GDN_DK), F32),
            pltpu.VMEM((2, n_tot, LANES), BF16),
            pltpu.VMEM((2, n_tot // CHUNK, SUBLANES, LANES), F32),
            pltpu.VMEM((2, GDN_DK, LANES), F32),
        ],
        compiler_params=_cparams(("parallel", "arbitrary")),
    )(qkv_l, qkv_l, qkv_l, qkv_c, qkv_c, qkv_c, conv_w, conv_w, conv_w, ab_l, ab_c, gparams)


def _lru_kernel(xl_ref, xc_ref, cw_ref, cb_ref, gw_ref, gb_ref, lam_ref, hl_ref, hc_ref,
                pad_s, x_s, a_s, u_s, *, n_lat, n_ctx):
    n_tot = n_ctx + n_lat
    x_s[0:n_ctx, :] = _dwconv(xc_ref, cw_ref, pad_s, n_ctx) + cb_ref[...]
    x_s[n_ctx:n_tot, :] = _dwconv(xl_ref, cw_ref, pad_s, n_lat) + cb_ref[...]

    sp = [_softplus(-lam_ref[:, d * LANES:(d + 1) * LANES]) for d in range(2)]
    tile = min(256, n_ctx)

    def coeffs(t, carry):
        rows = pl.ds(pl.multiple_of(t * tile, tile), tile)
        x = x_s[rows, :]
        gates = _dot(x.astype(BF16), gw_ref[...]) + gb_ref[...]
        for d in range(2):
            r = _sigmoid(gates[:, (2 * d) * LANES:(2 * d + 1) * LANES])
            i = _sigmoid(gates[:, (2 * d + 1) * LANES:(2 * d + 2) * LANES])
            a = jnp.exp(-LRU_C * r * sp[d])
            a_s[d, rows, :] = a
            u_s[d, rows, :] = jnp.sqrt(1.0 - a * a) * (i * x)
        return carry

    lax.fori_loop(0, n_tot // tile, coeffs, 0)

    sub = lax.broadcasted_iota(jnp.int32, (SUBLANES, LANES), 0)

    def group_scan(d, g0):
        rows = pl.ds(pl.multiple_of(g0, SUBLANES), SUBLANES)
        a = a_s[d, rows, :]
        u = u_s[d, rows, :]
        sh = 1
        while sh < SUBLANES:
            if d == 0:
                a_sh, u_sh, ok = pltpu.roll(a, sh, 0), pltpu.roll(u, sh, 0), sub >= sh
            else:
                a_sh, u_sh = pltpu.roll(a, SUBLANES - sh, 0), pltpu.roll(u, SUBLANES - sh, 0)
                ok = sub < SUBLANES - sh
            u = u + a * jnp.where(ok, u_sh, 0.0)
            a = a * jnp.where(ok, a_sh, 1.0)
            sh *= 2
        return rows, u, a

    def fwd(base, n, h0):
        def body(g, h):
            rows, u, a = group_scan(0, base + g * SUBLANES)
            hh = u + a * h
            u_s[0, rows, :] = hh
            return jnp.broadcast_to(hh[SUBLANES - 1:SUBLANES, :], (SUBLANES, LANES))
        return lax.fori_loop(0, n // SUBLANES, body, h0, unroll=8)

    def bwd(base, n, h0, out_ref):
        def body(g, h):
            lg = n // SUBLANES - 1 - g
            rows, u, a = group_scan(1, base + lg * SUBLANES)
            hh = u + a * h
            orow = pl.ds(pl.multiple_of(lg * SUBLANES, SUBLANES), SUBLANES)
            out_ref[orow, :] = u_s[0, rows, :] + hh
            return jnp.broadcast_to(hh[0:1, :], (SUBLANES, LANES))
        return lax.fori_loop(0, n // SUBLANES, body, h0, unroll=8)

    zero = jnp.zeros((SUBLANES, LANES), F32)
    h = fwd(0, n_ctx, zero)
    fwd(n_ctx, n_lat, h)
    h = bwd(0, n_ctx, zero, hc_ref)
    bwd(n_ctx, n_lat, h, hl_ref)


def _lru_call(xr_l, xr_c, conv_w, conv_b, gate_w, gate_b, lam, bsz, n_lat, n_ctx):
    n_tot = n_lat + n_ctx
    nt = LRU_W // LANES
    kern = functools.partial(_lru_kernel, n_lat=n_lat, n_ctx=n_ctx)
    return pl.pallas_call(
        kern,
        grid=(bsz, nt),
        in_specs=[pl.BlockSpec((n_lat, LANES), lambda b, j: (b, j)),
                  pl.BlockSpec((n_ctx, LANES), lambda b, j: (b, j)),
                  pl.BlockSpec((CONV_K, LANES), lambda b, j: (0, j)),
                  pl.BlockSpec((1, LANES), lambda b, j: (0, j)),
                  pl.BlockSpec((None, LANES, 4 * LANES), lambda b, j: (j, 0, 0)),
                  pl.BlockSpec((None, 1, 4 * LANES), lambda b, j: (j, 0, 0)),
                  pl.BlockSpec((None, 1, 2 * LANES), lambda b, j: (j, 0, 0))],
        out_specs=[pl.BlockSpec((n_lat, LANES), lambda b, j: (b, j)),
                   pl.BlockSpec((n_ctx, LANES), lambda b, j: (b, j))],
        out_shape=[jax.ShapeDtypeStruct((bsz * n_lat, LRU_W), F32),
                   jax.ShapeDtypeStruct((bsz * n_ctx, LRU_W), F32)],
        scratch_shapes=[pltpu.VMEM((n_lat + 2 * SUBLANES, LANES), F32),
                        pltpu.VMEM((n_tot, LANES), F32),
                        pltpu.VMEM((2, n_tot, LANES), F32),
                        pltpu.VMEM((2, n_tot, LANES), F32)],
        compiler_params=_cparams(("parallel", "arbitrary")),
    )(xr_l, xr_c, conv_w, conv_b, gate_w, gate_b, lam)


def _residual_ln(h, y, mod_ref, gate_row, lng_ref, lnb_ref, alpha):
    return _layernorm(alpha * h + mod_ref[gate_row:gate_row + 1, :] * y, lng_ref[...], lnb_ref[...])


FF_TILE = 1024


def _mlp_residual(h, mod_ref, w1_ref, w2_ref, lng_ref, lnb_ref, alpha):
    u = _modulate(h, mod_ref, 3, 4).astype(BF16)
    acc = None
    for j in range(D_FF // FF_TILE):
        sl = slice(j * FF_TILE, (j + 1) * FF_TILE)
        a = jnp.maximum(_dot(u, w1_ref[:, sl]), 0.0)
        part = _dot((a * a).astype(BF16), w2_ref[sl, :])
        acc = part if acc is None else acc + part
    return _residual_ln(h, acc, mod_ref, 5, lng_ref, lnb_ref, alpha)


def _post_even_kernel(o_ref, z_ref, hr_ref, gate_ref, h_ref, mod_ref, gn_ref, w_ref, lng0_ref, lnb0_ref,
                      w1_ref, w2_ref, lng1_ref, lnb1_ref, out_ref, *, alpha):
    parts = []
    for hd in range(GDN_HEADS):
        sl = slice(hd * LANES, (hd + 1) * LANES)
        o = o_ref[:, sl]
        on = o * lax.rsqrt(jnp.mean(o * o, axis=-1, keepdims=True) + NORM_EPS) * gn_ref[...]
        parts.append(on * _silu(z_ref[:, sl]))
    y_gdn = jnp.concatenate(parts, axis=1).astype(BF16)
    y_lru = (hr_ref[...] * _gelu_tanh(gate_ref[...])).astype(BF16)
    y = _dot(y_gdn, w_ref[0:GDN_W, :]) + _dot(y_lru, w_ref[GDN_W:, :])
    h1 = _residual_ln(h_ref[...], y, mod_ref, 2, lng0_ref, lnb0_ref, alpha)
    out_ref[...] = _mlp_residual(h1, mod_ref, w1_ref, w2_ref, lng1_ref, lnb1_ref, alpha)


def _post_odd_kernel(y_ref, h_ref, mod_ref, w_ref, lng0_ref, lnb0_ref, w1_ref, w2_ref, lng1_ref, lnb1_ref,
                     out_ref, *, alpha):
    y = _dot(y_ref[...], w_ref[...])
    h1 = _residual_ln(h_ref[...], y, mod_ref, 2, lng0_ref, lnb0_ref, alpha)
    out_ref[...] = _mlp_residual(h1, mod_ref, w1_ref, w2_ref, lng1_ref, lnb1_ref, alpha)


def _post_call(kern, token_inputs, h, mod, layer, per_batch, mixer_consts, tail_consts, alpha):
    n = h.shape[0]
    tm = _token_tile(n, per_batch)
    rpm = None if per_batch is None else per_batch // tm
    tok = lambda w: pl.BlockSpec((tm, w), lambda t: (t, 0))
    consts = list(mixer_consts) + list(tail_consts)
    const_specs = [_const_spec(a.shape) if a.ndim == 2 else
                   pl.BlockSpec((None,) + a.shape[1:], lambda t: (layer, 0, 0), pipeline_mode=pl.Buffered(1))
                   for a in consts]
    return pl.pallas_call(
        functools.partial(kern, alpha=alpha),
        grid=(n // tm,),
        in_specs=[tok(a.shape[1]) for a in token_inputs] + [tok(D_MODEL), _mod_spec(layer, rpm)]
                 + const_specs,
        out_specs=tok(D_MODEL),
        out_shape=jax.ShapeDtypeStruct((n, D_MODEL), F32),
        compiler_params=_cparams(("parallel",)),
    )(*token_inputs, h, mod, *consts)


def _qkv_rope_kernel(h_ref, mod_ref, wq_ref, wk_ref, wvt_ref, cos_ref, sin_ref, q_ref, k_ref, vt_ref):
    u = _modulate(h_ref[...], mod_ref, 0, 1).astype(BF16)
    cos = cos_ref[...]
    sin_signed = sin_ref[...]
    lane = lax.broadcasted_iota(jnp.int32, cos.shape, 1)
    partner_above = (lane % (2 * ROPE_Q)) < ROPE_Q

    def rope(y):
        parts = []
        for hd in range(DIFF_HEADS):
            t = y[:, hd * LANES:(hd + 1) * LANES]
            partner = jnp.where(partner_above, pltpu.roll(t, LANES - ROPE_Q, 1), pltpu.roll(t, ROPE_Q, 1))
            parts.append(t * cos + partner * sin_signed)
        return jnp.concatenate(parts, axis=1).astype(BF16)

    q_ref[...] = rope(_dot(u, wq_ref[...]))
    k_ref[...] = rope(_dot(u, wk_ref[...]))
    vt_ref[...] = _dot_nt(wvt_ref[...], u).astype(BF16)


def _qkv_plain_kernel(h_ref, mod_ref, wq_ref, wk_ref, wvt_ref, q_ref, k_ref, vt_ref):
    u = _modulate(h_ref[...], mod_ref, 0, 1).astype(BF16)
    q_ref[...] = _dot(u, wq_ref[...]).astype(BF16)
    k_ref[...] = _dot(u, wk_ref[...]).astype(BF16)
    vt_ref[...] = _dot_nt(wvt_ref[...], u).astype(BF16)


def _qkv_call(h, mod, layer, per_batch, weights, rope):
    n = h.shape[0]
    tm = _token_tile(n, per_batch)
    tok = pl.BlockSpec((tm, D_MODEL), lambda t: (t, 0))
    out_specs = [tok, tok, pl.BlockSpec((D_MODEL, tm), lambda t: (0, t))]
    out_shape = [jax.ShapeDtypeStruct((n, D_MODEL), BF16), jax.ShapeDtypeStruct((n, D_MODEL), BF16),
                 jax.ShapeDtypeStruct((D_MODEL, n), BF16)]
    wspec = _const_spec((D_MODEL, D_MODEL))
    if rope is not None:
        cos2, sin2 = rope
        rpm = per_batch // tm
        tab = pl.BlockSpec((tm, LANES), lambda t: (t % rpm, 0))
        wq, wk, wvt = weights
        return pl.pallas_call(
            _qkv_rope_kernel, grid=(n // tm,),
            in_specs=[tok, _mod_spec(layer, rpm), wspec, wspec, wspec, tab, tab],
            out_specs=out_specs, out_shape=out_shape,
            compiler_params=_cparams(("parallel",)),
        )(h, mod, wq, wk, wvt, cos2, sin2)
    wq, wk, wvt = weights
    return pl.pallas_call(
        _qkv_plain_kernel, grid=(n // tm,),
        in_specs=[tok, _mod_spec(layer, None), wspec, wspec, wspec],
        out_specs=out_specs, out_shape=out_shape,
        compiler_params=_cparams(("parallel",)),
    )(h, mod, wq, wk, wvt)


KEY_TILE = 512
SCORE_LOOKAHEAD = 1
Q_TILE = 1024
ONES_ROWS = 2 * SUBLANES


def _attn_kernel(*refs, key_lens, lam_init):
    nseg = len(key_lens)
    q_ref = refs[0]
    k_refs = refs[1:1 + nseg]
    vt_refs = refs[1 + nseg:1 + 2 * nseg]
    lam_ref, subln_ref, y_ref, acc_s = refs[1 + 2 * nseg:]
    tq = q_ref.shape[0]

    q = q_ref[...]
    lane = lax.broadcasted_iota(jnp.int32, q.shape, 1)
    zero = jnp.zeros_like(q)
    qm = [jnp.where(lane < DIFF_D, q, zero), jnp.where(lane >= DIFF_D, q, zero)]
    acc_s[...] = jnp.zeros_like(acc_s)

    tiles = []
    for k_ref, vt_ref, n_keys in zip(k_refs, vt_refs, key_lens):
        tk = min(KEY_TILE, n_keys)
        tiles += [(k_ref, vt_ref, j * tk, tk) for j in range(n_keys // tk)]

    def tile_scores(t):
        k_ref, _, k0, tk = tiles[t]
        k_t = k_ref[k0:k0 + tk, :]
        return [_dot_nt(k_t, qm[0]), _dot_nt(k_t, qm[1])]

    def absorb(s_pair, t, m_run):
        _, vt_ref, k0, tk = tiles[t]
        vt_ext = jnp.concatenate([vt_ref[:, k0:k0 + tk], jnp.ones((ONES_ROWS, tk), BF16)], axis=0)
        new = []
        for m in range(2):
            m_new = jnp.maximum(m_run[m], jnp.max(s_pair[m], axis=0, keepdims=True))
            alpha = jnp.exp2(m_run[m] - m_new)
            p = jnp.exp2(s_pair[m] - m_new).astype(BF16)
            acc_s[m] = alpha * acc_s[m] + _dot(vt_ext, p)
            new.append(m_new)
        return new

    stats = [jnp.full((1, tq), -1e30, F32)] * 2
    ahead = [tile_scores(t) for t in range(min(SCORE_LOOKAHEAD, len(tiles)))]
    for t in range(len(tiles)):
        if t + SCORE_LOOKAHEAD < len(tiles):
            ahead.append(tile_scores(t + SCORE_LOOKAHEAD))
        stats = absorb(ahead.pop(0), t, stats)

    lv = lam_ref[...]
    lam = (jnp.exp(jnp.sum(lv[0:1] * lv[1:2], axis=1, keepdims=True))
           - jnp.exp(jnp.sum(lv[2:3] * lv[3:4], axis=1, keepdims=True)) + lam_init)
    num = [acc_s[m, 0:DIFF_DV, :] for m in range(2)]
    den = [acc_s[m, DIFF_DV:DIFF_DV + 1, :] for m in range(2)]
    o_t = num[0] / den[0] - lam * (num[1] / den[1])
    ms = jnp.mean(o_t * o_t, axis=0, keepdims=True)
    y_t = o_t * lax.rsqrt(ms + NORM_EPS) * subln_ref[...] * (1.0 - lam_init)
    y_ref[...] = y_t.T.astype(BF16)


def _attn_call(q, ks, vts, lam_vec, subln_col, bsz, n_q, key_lens, lam_init):
    tq = min(Q_TILE, n_q)
    nq = n_q // tq
    nseg = len(key_lens)
    in_specs = [pl.BlockSpec((tq, LANES), lambda b, h, i: (b * nq + i, h))]
    in_specs += [pl.BlockSpec((n, LANES), lambda b, h, i: (b, h)) for n in key_lens]
    in_specs += [pl.BlockSpec((LANES, n), lambda b, h, i: (h, b)) for n in key_lens]
    in_specs += [pl.BlockSpec((4, DIFF_D), lambda b, h, i: (0, 0)),
                 pl.BlockSpec((DIFF_DV, 1), lambda b, h, i: (0, 0))]
    kern = functools.partial(_attn_kernel, key_lens=tuple(key_lens), lam_init=lam_init)
    return pl.pallas_call(
        kern,
        grid=(bsz, DIFF_HEADS, nq),
        in_specs=in_specs,
        out_specs=pl.BlockSpec((tq, LANES), lambda b, h, i: (b * nq + i, h)),
        out_shape=jax.ShapeDtypeStruct((bsz * n_q, D_MODEL), BF16),
        scratch_shapes=[pltpu.VMEM((2, DIFF_DV + ONES_ROWS, tq), F32)],
        compiler_params=_cparams(("parallel", "parallel", "arbitrary")),
    )(q, *ks, *vts, lam_vec, subln_col)


def _rope_tables(n_lat):
    rows = n_lat // GRID_W
    row = jnp.repeat(jnp.arange(rows), GRID_W).astype(F32)
    col = jnp.tile(jnp.arange(GRID_W), rows).astype(F32)
    half = DIFF_D // 2
    inv = ROPE_THETA ** (-(jnp.arange(0, half, 2, dtype=F32) / half))
    ang_r = row[:, None] * inv
    ang_c = col[:, None] * inv
    ang = jnp.concatenate([ang_r, ang_r, ang_c, ang_c], axis=-1)
    cos, sin = jnp.cos(ang), jnp.sin(ang)
    sign = jnp.where((jnp.arange(DIFF_D) % (2 * ROPE_Q)) < ROPE_Q, -1.0, 1.0).astype(F32)
    sin_signed = sin * sign
    return jnp.concatenate([cos, cos], axis=-1), jnp.concatenate([sin_signed, sin_signed], axis=-1)


def _lru_gate_weights(gate_w, gate_b, lam):
    nt = LRU_W // LANES
    per = LANES // LRU_BW
    blocks = gate_w.reshape(2, 2, nt, per, LRU_BW, LRU_BW)
    eye = jnp.eye(per, dtype=gate_w.dtype)
    dense = jnp.einsum('dgtpab,pq->dgtpaqb', blocks, eye).reshape(2, 2, nt, LANES, LANES)
    w = jnp.transpose(dense, (2, 3, 0, 1, 4)).reshape(nt, LANES, 4 * LANES)
    b = jnp.transpose(gate_b.reshape(2, 2, nt, LANES), (2, 0, 1, 3)).reshape(nt, 1, 4 * LANES)
    lm = jnp.transpose(lam.reshape(2, nt, LANES), (1, 0, 2)).reshape(nt, 1, 2 * LANES)
    return w.astype(BF16), b, lm


def _gate_columns_per_head(w_ab):
    d = w_ab.shape[0]
    per_head = jnp.transpose(w_ab.reshape(d, 4, GDN_HEADS), (0, 2, 1)).reshape(d, 4 * GDN_HEADS)
    return jnp.pad(per_head, ((0, 0), (0, LANES - 4 * GDN_HEADS)))


def _cast_kernel(x_ref, o_ref):
    o_ref[...] = x_ref[...].astype(BF16)


def _cast_bf16(w):
    n, r, c = w.shape
    tr = max(t for t in (512, 256, 128, 64, 32, 16) if r % t == 0 and t * c * 4 <= 4 * 1024 * 1024)
    return pl.pallas_call(
        _cast_kernel,
        grid=(n, r // tr),
        in_specs=[pl.BlockSpec((None, tr, c), lambda i, j: (i, j, 0))],
        out_specs=pl.BlockSpec((None, tr, c), lambda i, j: (i, j, 0)),
        out_shape=jax.ShapeDtypeStruct(w.shape, BF16),
        compiler_params=_cparams(("parallel", "parallel")),
    )(w)


def kernel(x, c, ctx, c_ctx, ada_w, ada_b, ln_g, ln_b, mlp_w1, mlp_w2, mix_w_out, ev_w_in, ev_qkv_conv,
           ev_a_log, ev_dt_bias, ev_gdn_norm, ev_lru_conv_w, ev_lru_conv_b, ev_lru_gate_w, ev_lru_gate_b,
           ev_lru_lambda, od_w_qkv, od_lambda, od_subln):
    bsz, n_lat, d = x.shape
    n_ctx = ctx.shape[1]
    depth = ada_w.shape[0]
    assert d == D_MODEL and bsz + 1 <= MOD_ROWS
    assert n_lat % CHUNK == 0 and n_ctx % CHUNK == 0 and n_lat % GRID_W == 0
    alpha = (2 * depth) ** 0.25

    h_lat = x.reshape(bsz * n_lat, d)
    h_ctx = ctx.reshape(bsz * n_ctx, d)
    cc = jnp.concatenate([c_ctx[None, :], c, jnp.zeros((MOD_ROWS - 1 - bsz, d), F32)], axis=0)
    mod = _ada_call(cc, ada_w, ada_b).reshape(depth, MOD_ROWS, 6, d)
    cos2, sin2 = _rope_tables(n_lat)
    w1_all, w2_all, w_out_all = _cast_bf16(mlp_w1), _cast_bf16(mlp_w2), _cast_bf16(mix_w_out)

    for i in range(depth):
        last = i == depth - 1
        j = i // 2
        lng0, lnb0 = ln_g[i, 0][None, :], ln_b[i, 0][None, :]
        lng1, lnb1 = ln_g[i, 1][None, :], ln_b[i, 1][None, :]
        tail = (w1_all, w2_all, lng1, lnb1)
        if i % 2 == 0:
            w_in = ev_w_in[j]
            c0, c1, c2 = 4 * GDN_W, 4 * GDN_W + 4 * GDN_HEADS, 4 * GDN_W + 4 * GDN_HEADS + 2 * LRU_W
            w_main = jnp.concatenate([w_in[:, :c0], w_in[:, c1:c2]], axis=1).astype(BF16)
            w_ab = _gate_columns_per_head(w_in[:, c0:c1]).astype(BF16)
            qkv_l, z_l, xr_l, gate_l, ab_l = _inproj_even_call(h_lat, mod, i, n_lat, w_main, w_ab)
            qkv_c, z_c, xr_c, gate_c, ab_c = _inproj_even_call(h_ctx, mod, i, None, w_main, w_ab)
            gparams = jnp.stack([ev_a_log[j, 0], ev_a_log[j, 1], ev_dt_bias[j, 0], ev_dt_bias[j, 1]],
                                axis=-1).reshape(GDN_HEADS, 1, 4)
            o_l, o_c = _gdn_call(qkv_l, qkv_c, ab_l, ab_c, ev_qkv_conv[j], gparams, bsz, n_lat, n_ctx)
            gw, gb, lm = _lru_gate_weights(ev_lru_gate_w[j], ev_lru_gate_b[j], ev_lru_lambda[j])
            hr_l, hr_c = _lru_call(xr_l, xr_c, ev_lru_conv_w[j], ev_lru_conv_b[j][None, :], gw, gb, lm,
                                   bsz, n_lat, n_ctx)
            gn = ev_gdn_norm[j][None, :]
            mixer = (gn, w_out_all, lng0, lnb0)
            h_lat = _post_call(_post_even_kernel, (o_l, z_l, hr_l, gate_l), h_lat, mod, i, n_lat, mixer, tail,
                               alpha)
            if not last:
                h_ctx = _post_call(_post_even_kernel, (o_c, z_c, hr_c, gate_c), h_ctx, mod, i, None, mixer,
                                   tail, alpha)
        else:
            lam_init = 0.8 - 0.6 * math.exp(-0.3 * i)
            wq = od_w_qkv[j][:, :D_MODEL] * (DIFF_D ** -0.5 * math.log2(math.e))
            wk = od_w_qkv[j][:, D_MODEL:2 * D_MODEL]
            wvt = od_w_qkv[j][:, 2 * D_MODEL:].T.astype(BF16)
            qkv_w = (wq.astype(BF16), wk.astype(BF16), wvt)
            q_l, k_l, vt_l = _qkv_call(h_lat, mod, i, n_lat, qkv_w, (cos2, sin2))
            q_c, k_c, vt_c = _qkv_call(h_ctx, mod, i, None, qkv_w, None)
            subln_col = od_subln[j][:, None]
            y_l = _attn_call(q_l, [k_c, k_l], [vt_c, vt_l], od_lambda[j], subln_col, bsz, n_lat,
                             [n_ctx, n_lat], lam_init)
            mixer = (w_out_all, lng0, lnb0)
            h_lat = _post_call(_post_odd_kernel, (y_l,), h_lat, mod, i, n_lat, mixer, tail, alpha)
            if not last:
                y_c = _attn_call(q_c, [k_c], [vt_c], od_lambda[j], subln_col, bsz, n_ctx, [n_ctx], lam_init)
                h_ctx = _post_call(_post_odd_kernel, (y_c,), h_ctx, mod, i, None, mixer, tail, alpha)
    return h_lat.reshape(bsz, n_lat, d)
```

```python
import functools
import math

import jax
import jax.numpy as jnp
from jax import lax
from jax.experimental import pallas as pl
from jax.experimental.pallas import tpu as pltpu

F32 = jnp.float32
BF16 = jnp.bfloat16

D_MODEL = 1024
D_FF = 4 * D_MODEL
CONV_K = 4
CONV_LEFT = 2
GDN_HEADS = 4
GDN_DK = 128
GDN_W = GDN_HEADS * GDN_DK
CHUNK = 64
LRU_W = D_MODEL - GDN_W
LRU_BLOCKS = 8
LRU_BW = LRU_W // LRU_BLOCKS
LRU_C = 8.0
DIFF_HEADS = 8
DIFF_D = 64
DIFF_DV = 128
ROPE_Q = DIFF_D // 4
GRID_W = 64
ROPE_THETA = 10000.0
NORM_EPS = 1e-6
LANES = 128
SUBLANES = 8
MOD_ROWS = 16
TOKEN_TILE = 512
GDN_GROUP = 8
LRU_COEFF_ROWS = 256
CAST_BLOCK_BYTES = 4 * 1024 * 1024
VMEM_LIMIT = 56 * 1024 * 1024


def _cparams(sem):
    return pltpu.CompilerParams(dimension_semantics=sem, vmem_limit_bytes=VMEM_LIMIT)


def _sigmoid(x):
    return 0.5 * jnp.tanh(0.5 * x) + 0.5


def _silu(x):
    return x * _sigmoid(x)


def _softplus(x):
    return jnp.maximum(x, 0.0) + jnp.log1p(jnp.exp(-jnp.abs(x)))


def _gelu_tanh(x):
    return 0.5 * x * (1.0 + jnp.tanh(math.sqrt(2.0 / math.pi) * (x + 0.044715 * (x * x * x))))


def _modulate(h, mod_ref, shift_row, scale_row):
    return h * (1.0 + mod_ref[scale_row:scale_row + 1, :]) + mod_ref[shift_row:shift_row + 1, :]


def _layernorm(x, g, b):
    mu = jnp.mean(x, axis=-1, keepdims=True)
    xc = x - mu
    var = jnp.mean(xc * xc, axis=-1, keepdims=True)
    return xc * lax.rsqrt(var + NORM_EPS) * g + b


def _dot(a, b):
    return jnp.dot(a, b, preferred_element_type=F32)


def _dot_nt(a, b):
    return lax.dot_general(a, b, (((1,), (1,)), ((), ())), preferred_element_type=F32)


def _dot_tn(a, b):
    return lax.dot_general(a, b, (((0,), (0,)), ((), ())), preferred_element_type=F32)


def _ada_kernel(c_ref, w_ref, b_ref, o_ref):
    s = _silu(c_ref[...])
    o_ref[...] = jnp.dot(s, w_ref[...], preferred_element_type=F32,
                         precision=lax.Precision.HIGHEST) + b_ref[...]


def _ada_call(cc, ada_w, ada_b):
    depth, d, n = ada_w.shape
    tn = D_MODEL
    return pl.pallas_call(
        _ada_kernel,
        grid=(depth, n // tn),
        in_specs=[pl.BlockSpec((MOD_ROWS, d), lambda i, j: (0, 0)),
                  pl.BlockSpec((None, d, tn), lambda i, j: (i, 0, j)),
                  pl.BlockSpec((None, 1, tn), lambda i, j: (i, 0, j))],
        out_specs=pl.BlockSpec((None, MOD_ROWS, tn), lambda i, j: (i, 0, j)),
        out_shape=jax.ShapeDtypeStruct((depth, MOD_ROWS, n), F32),
        compiler_params=_cparams(("parallel", "parallel")),
    )(cc, ada_w, ada_b.reshape(depth, 1, n))


def _mod_spec(layer, rows_per_mod):
    if rows_per_mod is None:
        return pl.BlockSpec((None, None, 6, D_MODEL), lambda t: (layer, 0, 0, 0))
    return pl.BlockSpec((None, None, 6, D_MODEL), lambda t: (layer, 1 + t // rows_per_mod, 0, 0))


def _const_spec(shape):
    nd = len(shape)
    return pl.BlockSpec(shape, lambda t: (0,) * nd, pipeline_mode=pl.Buffered(1))


def _token_tile(n_rows, per_batch):
    tm = min(TOKEN_TILE, n_rows if per_batch is None else per_batch)
    assert n_rows % tm == 0 and (per_batch is None or per_batch % tm == 0)
    return tm


def _inproj_even_kernel(h_ref, mod_ref, w_ref, wab_ref, qkv_ref, z_ref, xr_ref, gate_ref, ab_ref):
    u = _modulate(h_ref[...], mod_ref, 0, 1).astype(BF16)
    qkv_ref[...] = _dot(u, w_ref[:, 0:3 * GDN_W])
    z_ref[...] = _dot(u, w_ref[:, 3 * GDN_W:4 * GDN_W]).astype(BF16)
    xr_ref[...] = _dot(u, w_ref[:, 4 * GDN_W:4 * GDN_W + LRU_W])
    gate_ref[...] = _dot(u, w_ref[:, 4 * GDN_W + LRU_W:]).astype(BF16)
    ab = _dot(u, wab_ref[...])
    for hd in range(GDN_HEADS):
        ab_ref[hd] = ab[:, 4 * hd:4 * hd + 4]


def _inproj_even_call(h, mod, layer, per_batch, w_main, w_ab):
    n = h.shape[0]
    tm = _token_tile(n, per_batch)
    rpm = None if per_batch is None else per_batch // tm
    tok = lambda w: pl.BlockSpec((tm, w), lambda t: (t, 0))
    return pl.pallas_call(
        _inproj_even_kernel,
        grid=(n // tm,),
        in_specs=[tok(D_MODEL), _mod_spec(layer, rpm), _const_spec(w_main.shape), _const_spec(w_ab.shape)],
        out_specs=[tok(3 * GDN_W), tok(GDN_W), tok(LRU_W), tok(LRU_W),
                   pl.BlockSpec((GDN_HEADS, tm, 4), lambda t: (0, t, 0))],
        out_shape=[jax.ShapeDtypeStruct((n, 3 * GDN_W), F32), jax.ShapeDtypeStruct((n, GDN_W), BF16),
                   jax.ShapeDtypeStruct((n, LRU_W), F32), jax.ShapeDtypeStruct((n, LRU_W), BF16),
                   jax.ShapeDtypeStruct((GDN_HEADS, n, 4), F32)],
        compiler_params=_cparams(("parallel",)),
    )(h, mod, w_main, w_ab)


def _dwconv(x_ref, w_ref, pad_ref, n):
    zeros = jnp.zeros((SUBLANES, LANES), F32)
    pad_ref[0:SUBLANES, :] = zeros
    pad_ref[SUBLANES + n:2 * SUBLANES + n, :] = zeros
    pad_ref[SUBLANES:SUBLANES + n, :] = x_ref[...]
    y = None
    for j in range(CONV_K):
        off = SUBLANES + j - CONV_LEFT
        term = pad_ref[off:off + n, :] * w_ref[j:j + 1, :]
        y = term if y is None else y + term
    return y


def _hp_parts(x):
    hi = x.astype(BF16).astype(F32)
    return hi, x - hi


def _hp_operands(x):
    hi, lo = _hp_parts(x)
    x2 = hi + pltpu.roll(lo, CHUNK, 1)
    lhs = jnp.concatenate([x2, x2], axis=1).astype(BF16)
    rhs = jnp.concatenate([hi, hi, lo, lo], axis=0).astype(BF16)
    return lhs, rhs


def _gdn_kernel(ql_ref, kl_ref, vl_ref, qc_ref, kc_ref, vc_ref, wq_ref, wk_ref, wv_ref,
                abl_ref, abc_ref, gp_ref, ol_ref, oc_ref,
                pad_s, q_s, k_s, v_s, ab_s, mt_s, nt_s, qe_s, gl_s, st_s, *, n_lat, n_ctx):
    n_tot = n_ctx + n_lat

    def prep(x_ref, w_ref, n, dst, base, kind):
        y = _silu(_dwconv(x_ref, w_ref, pad_s, n))
        if kind != "v":
            y = y * lax.rsqrt(jnp.sum(y * y, axis=-1, keepdims=True) + NORM_EPS)
        if kind == "q":
            y = y * (GDN_DK ** -0.5)
        dst[base:base + n, :] = y

    prep(qc_ref, wq_ref, n_ctx, q_s, 0, "q")
    prep(kc_ref, wk_ref, n_ctx, k_s, 0, "k")
    prep(vc_ref, wv_ref, n_ctx, v_s, 0, "v")
    prep(ql_ref, wq_ref, n_lat, q_s, n_ctx, "q")
    prep(kl_ref, wk_ref, n_lat, k_s, n_ctx, "k")
    prep(vl_ref, wv_ref, n_lat, v_s, n_ctx, "v")
    ab_s[0:n_ctx, :] = abc_ref[...]
    ab_s[n_ctx:n_tot, :] = abl_ref[...]

    row = lax.broadcasted_iota(jnp.int32, (CHUNK, LANES), 0)
    col = lax.broadcasted_iota(jnp.int32, (CHUNK, LANES), 1)
    left = col < CHUNK
    eye = jnp.where(row == col, 1.0, 0.0).astype(F32)
    incl = [jnp.logical_and(row >= col, left), jnp.logical_and(row <= col, left)]
    strict = [jnp.logical_and(row > col, left), jnp.logical_and(row < col, left)]
    ones_left = jnp.where(left, 1.0, 0.0).astype(F32)
    gp = gp_ref[...]

    def chunk_group(chunk0, out_ref, group, gi):
        chains = []
        for g in range(group):
            lc = gi * group + g
            c = chunk0 + lc
            rows = pl.ds(pl.multiple_of(c * CHUNK, CHUNK), CHUNK)
            orow = pl.ds(pl.multiple_of(lc * CHUNK, CHUNK), CHUNK)
            q_c = q_s[rows, :]
            k_c = k_s[rows, :]
            v_c = v_s[rows, :]
            ab_c = ab_s[rows, :]
            kb = k_c.astype(BF16)
            k_pad = jnp.concatenate([kb, jnp.zeros_like(kb)], axis=0)
            qk_kk = _dot_nt(jnp.concatenate([q_c.astype(BF16), kb], axis=0), k_pad)
            for d in range(2):
                chains.append(dict(c=c, rows=rows, orow=orow, d=d, q=q_c, k=k_c, v=v_c, ab=ab_c,
                                   qk_raw=qk_kk[0:CHUNK], kk=qk_kk[CHUNK:2 * CHUNK]))
        for ch in chains:
            d = ch["d"]
            a_col = ch["ab"][:, d:d + 1]
            b_col = ch["ab"][:, 2 + d:3 + d]
            g_col = -jnp.exp(gp[:, d:d + 1]) * _softplus(a_col + gp[:, 2 + d:3 + d])
            ch["beta"] = _sigmoid(b_col)
            g_b = jnp.broadcast_to(g_col, (CHUNK, LANES))
            m_t = incl[1 - d]
            gh, gl = _hp_parts(g_b)
            cum = jnp.where(incl[d], 1.0, 0.0).astype(F32)
            lhs1 = cum + pltpu.roll(ones_left, CHUNK, 1)
            lhs = jnp.concatenate([lhs1, lhs1], axis=1).astype(BF16)
            rhs = jnp.concatenate([gh, jnp.where(m_t, -gh, 0.0), gl, jnp.where(m_t, -gl, 0.0)],
                                  axis=0).astype(BF16)
            ch["diff"] = _dot(lhs, rhs)
            ch["tot"] = jnp.sum(g_b, axis=0, keepdims=True)
        for ch in chains:
            d = ch["d"]
            ch["decay"] = jnp.exp(jnp.where(incl[d], ch["diff"], -1e30))
            a_mat = jnp.where(strict[d], ch["beta"] * ch["kk"] * ch["decay"], 0.0)
            ch["p"] = eye - a_mat
            ch["a"] = a_mat
        for ch in chains:
            a_lhs, a_rhs = _hp_operands(ch["a"])
            ch["q2"] = _dot(a_lhs, a_rhs)
        sq = 2
        while sq * 2 < CHUNK:
            for ch in chains:
                q_lhs, q_rhs = _hp_operands(ch["q2"])
                p_lhs, _ = _hp_operands(ch["p"])
                r = _dot(jnp.concatenate([p_lhs, q_lhs], axis=0), q_rhs)
                ch["p"] = ch["p"] + r[0:CHUNK]
                ch["q2"] = r[CHUNK:2 * CHUNK]
            sq *= 2
        for ch in chains:
            ch["t"] = ch["p"] + _dot(_hp_operands(ch["p"])[0], _hp_operands(ch["q2"])[1])
        for ch in chains:
            gam = jnp.where(left, pltpu.roll(ch["diff"], CHUNK, 1), ch["diff"])
            egam = jnp.exp(gam)
            beta = ch["beta"]
            rhs_uw = jnp.concatenate([ch["k"] * (beta * egam), ch["v"] * beta], axis=1).astype(BF16)
            ch["wu"] = _dot(ch["t"][:, 0:CHUNK].astype(BF16), rhs_uw).astype(BF16)
            ch["qg"] = ch["q"] * egam
            ch["kd"] = (ch["k"] * jnp.exp(ch["tot"] - gam)).astype(BF16)
        for ch in chains:
            d, c = ch["d"], ch["c"]
            qk_m = (ch["qk_raw"] * ch["decay"])[:, 0:CHUNK].astype(BF16)
            x = _dot(qk_m, ch["wu"])
            mn = _dot_tn(ch["wu"], ch["kd"])
            qe_s[d, ch["rows"], :] = (ch["qg"] - x[:, 0:LANES]).astype(BF16)
            out_ref[ch["orow"], :] = out_ref[ch["orow"], :] + x[:, LANES:2 * LANES]
            mt_s[d, c] = mn[0:LANES].astype(BF16)
            nt_s[d, c] = mn[LANES:2 * LANES]
            gl_s[d, c] = jnp.broadcast_to(jnp.exp(ch["tot"]), (SUBLANES, LANES))

    def prep_segment(chunk0, n_chunks, out_ref):
        group = max(g for g in (GDN_GROUP, 4, 2, 1) if n_chunks % g == 0)

        def prep(gi, carry):
            chunk_group(chunk0, out_ref, group, gi)
            return carry
        lax.fori_loop(0, n_chunks // group, prep, 0)

    def rec_segment(chunk0, n_chunks, out_ref):
        def step(s, carry):
            for d in range(2):
                lc = s if d == 0 else n_chunks - 1 - s
                c = chunk0 + lc
                rows = pl.ds(pl.multiple_of(c * CHUNK, CHUNK), CHUNK)
                orow = pl.ds(pl.multiple_of(lc * CHUNK, CHUNK), CHUNK)
                st = st_s[d]
                stb = st.astype(BF16)
                r = _dot(stb, mt_s[d, c])
                out_ref[orow, :] = out_ref[orow, :] + _dot_nt(qe_s[d, rows, :], stb)
                st_s[d] = st * gl_s[d, c][0:1, :] + nt_s[d, c] - r
            return carry
        lax.fori_loop(0, n_chunks, step, 0, unroll=max(g for g in (8, 4, 2, 1) if n_chunks % g == 0))

    st_s[...] = jnp.zeros_like(st_s)
    ol_ref[...] = jnp.zeros_like(ol_ref)
    oc_ref[...] = jnp.zeros_like(oc_ref)
    prep_segment(0, n_ctx // CHUNK, oc_ref)
    prep_segment(n_ctx // CHUNK, n_lat // CHUNK, ol_ref)
    rec_segment(0, n_ctx // CHUNK, oc_ref)
    rec_segment(n_ctx // CHUNK, n_lat // CHUNK, ol_ref)


def _gdn_call(qkv_l, qkv_c, ab_l, ab_c, conv_w, gparams, bsz, n_lat, n_ctx):
    n_tot = n_lat + n_ctx
    nh = GDN_HEADS
    lat = lambda off: pl.BlockSpec((n_lat, LANES), lambda b, h: (b, off + h))
    ctx = lambda off: pl.BlockSpec((n_ctx, LANES), lambda b, h: (b, off + h))
    cw = lambda off: pl.BlockSpec((CONV_K, LANES), lambda b, h: (0, off + h))
    kern = functools.partial(_gdn_kernel, n_lat=n_lat, n_ctx=n_ctx)
    return pl.pallas_call(
        kern,
        grid=(bsz, nh),
        in_specs=[lat(0), lat(nh), lat(2 * nh), ctx(0), ctx(nh), ctx(2 * nh), cw(0), cw(nh), cw(2 * nh),
                  pl.BlockSpec((None, n_lat, 4), lambda b, h: (h, b, 0)),
                  pl.BlockSpec((None, n_ctx, 4), lambda b, h: (h, b, 0)),
                  pl.BlockSpec((None, 1, 4), lambda b, h: (h, 0, 0))],
        out_specs=[pl.BlockSpec((n_lat, LANES), lambda b, h: (b, h)),
                   pl.BlockSpec((n_ctx, LANES), lambda b, h: (b, h))],
        out_shape=[jax.ShapeDtypeStruct((bsz * n_lat, GDN_W), F32),
                   jax.ShapeDtypeStruct((bsz * n_ctx, GDN_W), F32)],
        scratch_shapes=[
            pltpu.VMEM((n_lat + 2 * SUBLANES, LANES), F32),
            pltpu.VMEM((n_tot, LANES), F32),
            pltpu.VMEM((n_tot, LANES), F32),
            pltpu.VMEM((n_tot, LANES), F32),
            pltpu.VMEM((n_tot, 4), F32),
            pltpu.VMEM((2, n_tot // CHUNK, GDN_DK, LANES), BF16),
            pltpu.VMEM((2, n_tot // CHUNK, LANES, GDN_DK), F32),
            pltpu.VMEM((2, n_tot, LANES), BF16),
            pltpu.VMEM((2, n_tot // CHUNK, SUBLANES, LANES), F32),
            pltpu.VMEM((2, GDN_DK, LANES), F32),
        ],
        compiler_params=_cparams(("parallel", "arbitrary")),
    )(qkv_l, qkv_l, qkv_l, qkv_c, qkv_c, qkv_c, conv_w, conv_w, conv_w, ab_l, ab_c, gparams)


def _lru_kernel(xl_ref, xc_ref, cw_ref, cb_ref, gw_ref, gb_ref, lam_ref, hl_ref, hc_ref,
                pad_s, x_s, a_s, u_s, *, n_lat, n_ctx):
    n_tot = n_ctx + n_lat
    x_s[0:n_ctx, :] = _dwconv(xc_ref, cw_ref, pad_s, n_ctx) + cb_ref[...]
    x_s[n_ctx:n_tot, :] = _dwconv(xl_ref, cw_ref, pad_s, n_lat) + cb_ref[...]

    sp = [_softplus(-lam_ref[:, d * LANES:(d + 1) * LANES]) for d in range(2)]
    tile = min(LRU_COEFF_ROWS, n_ctx)

    def coeffs(t, carry):
        rows = pl.ds(pl.multiple_of(t * tile, tile), tile)
        x = x_s[rows, :]
        gates = _dot(x.astype(BF16), gw_ref[...]) + gb_ref[...]
        for d in range(2):
            r = _sigmoid(gates[:, (2 * d) * LANES:(2 * d + 1) * LANES])
            i = _sigmoid(gates[:, (2 * d + 1) * LANES:(2 * d + 2) * LANES])
            a = jnp.exp(-LRU_C * r * sp[d])
            a_s[d, rows, :] = a
            u_s[d, rows, :] = jnp.sqrt(1.0 - a * a) * (i * x)
        return carry

    lax.fori_loop(0, n_tot // tile, coeffs, 0)

    sub = lax.broadcasted_iota(jnp.int32, (SUBLANES, LANES), 0)

    def group_scan(d, g0):
        rows = pl.ds(pl.multiple_of(g0, SUBLANES), SUBLANES)
        a = a_s[d, rows, :]
        u = u_s[d, rows, :]
        sh = 1
        while sh < SUBLANES:
            if d == 0:
                a_sh, u_sh, ok = pltpu.roll(a, sh, 0), pltpu.roll(u, sh, 0), sub >= sh
            else:
                a_sh, u_sh = pltpu.roll(a, SUBLANES - sh, 0), pltpu.roll(u, SUBLANES - sh, 0)
                ok = sub < SUBLANES - sh
            u = u + a * jnp.where(ok, u_sh, 0.0)
            a = a * jnp.where(ok, a_sh, 1.0)
            sh *= 2
        return rows, u, a

    def fwd(base, n, h0):
        def body(g, h):
            rows, u, a = group_scan(0, base + g * SUBLANES)
            hh = u + a * h
            u_s[0, rows, :] = hh
            return jnp.broadcast_to(hh[SUBLANES - 1:SUBLANES, :], (SUBLANES, LANES))
        return lax.fori_loop(0, n // SUBLANES, body, h0, unroll=8)

    def bwd(base, n, h0, out_ref):
        def body(g, h):
            lg = n // SUBLANES - 1 - g
            rows, u, a = group_scan(1, base + lg * SUBLANES)
            hh = u + a * h
            orow = pl.ds(pl.multiple_of(lg * SUBLANES, SUBLANES), SUBLANES)
            out_ref[orow, :] = u_s[0, rows, :] + hh
            return jnp.broadcast_to(hh[0:1, :], (SUBLANES, LANES))
        return lax.fori_loop(0, n // SUBLANES, body, h0, unroll=8)

    zero = jnp.zeros((SUBLANES, LANES), F32)
    h = fwd(0, n_ctx, zero)
    fwd(n_ctx, n_lat, h)
    h = bwd(0, n_ctx, zero, hc_ref)
    bwd(n_ctx, n_lat, h, hl_ref)


def _lru_call(xr_l, xr_c, conv_w, conv_b, gate_w, gate_b, lam, bsz, n_lat, n_ctx):
    n_tot = n_lat + n_ctx
    nt = LRU_W // LANES
    kern = functools.partial(_lru_kernel, n_lat=n_lat, n_ctx=n_ctx)
    return pl.pallas_call(
        kern,
        grid=(bsz, nt),
        in_specs=[pl.BlockSpec((n_lat, LANES), lambda b, j: (b, j)),
                  pl.BlockSpec((n_ctx, LANES), lambda b, j: (b, j)),
                  pl.BlockSpec((CONV_K, LANES), lambda b, j: (0, j)),
                  pl.BlockSpec((1, LANES), lambda b, j: (0, j)),
                  pl.BlockSpec((None, LANES, 4 * LANES), lambda b, j: (j, 0, 0)),
                  pl.BlockSpec((None, 1, 4 * LANES), lambda b, j: (j, 0, 0)),
                  pl.BlockSpec((None, 1, 2 * LANES), lambda b, j: (j, 0, 0))],
        out_specs=[pl.BlockSpec((n_lat, LANES), lambda b, j: (b, j)),
                   pl.BlockSpec((n_ctx, LANES), lambda b, j: (b, j))],
        out_shape=[jax.ShapeDtypeStruct((bsz * n_lat, LRU_W), F32),
                   jax.ShapeDtypeStruct((bsz * n_ctx, LRU_W), F32)],
        scratch_shapes=[pltpu.VMEM((n_lat + 2 * SUBLANES, LANES), F32),
                        pltpu.VMEM((n_tot, LANES), F32),
                        pltpu.VMEM((2, n_tot, LANES), F32),
                        pltpu.VMEM((2, n_tot, LANES), F32)],
        compiler_params=_cparams(("parallel", "arbitrary")),
    )(xr_l, xr_c, conv_w, conv_b, gate_w, gate_b, lam)


def _residual_ln(h, y, mod_ref, gate_row, lng_ref, lnb_ref, alpha):
    return _layernorm(alpha * h + mod_ref[gate_row:gate_row + 1, :] * y, lng_ref[...], lnb_ref[...])


FF_TILE = 1024


def _mlp_residual(h, mod_ref, w1_ref, w2_ref, lng_ref, lnb_ref, alpha):
    u = _modulate(h, mod_ref, 3, 4).astype(BF16)
    acc = None
    for j in range(D_FF // FF_TILE):
        sl = slice(j * FF_TILE, (j + 1) * FF_TILE)
        a = jnp.maximum(_dot(u, w1_ref[:, sl]), 0.0)
        part = _dot((a * a).astype(BF16), w2_ref[sl, :])
        acc = part if acc is None else acc + part
    return _residual_ln(h, acc, mod_ref, 5, lng_ref, lnb_ref, alpha)


def _post_even_kernel(o_ref, z_ref, hr_ref, gate_ref, h_ref, mod_ref, gn_ref, w_ref, lng0_ref, lnb0_ref,
                      w1_ref, w2_ref, lng1_ref, lnb1_ref, out_ref, *, alpha):
    parts = []
    for hd in range(GDN_HEADS):
        sl = slice(hd * LANES, (hd + 1) * LANES)
        o = o_ref[:, sl]
        on = o * lax.rsqrt(jnp.mean(o * o, axis=-1, keepdims=True) + NORM_EPS) * gn_ref[...]
        parts.append(on * _silu(z_ref[:, sl].astype(F32)))
    y_gdn = jnp.concatenate(parts, axis=1).astype(BF16)
    y_lru = (hr_ref[...] * _gelu_tanh(gate_ref[...].astype(F32))).astype(BF16)
    y = _dot(y_gdn, w_ref[0:GDN_W, :]) + _dot(y_lru, w_ref[GDN_W:, :])
    h1 = _residual_ln(h_ref[...], y, mod_ref, 2, lng0_ref, lnb0_ref, alpha)
    out_ref[...] = _mlp_residual(h1, mod_ref, w1_ref, w2_ref, lng1_ref, lnb1_ref, alpha)


def _post_odd_kernel(y_ref, h_ref, mod_ref, w_ref, lng0_ref, lnb0_ref, w1_ref, w2_ref, lng1_ref, lnb1_ref,
                     out_ref, *, alpha):
    y = _dot(y_ref[...], w_ref[...])
    h1 = _residual_ln(h_ref[...], y, mod_ref, 2, lng0_ref, lnb0_ref, alpha)
    out_ref[...] = _mlp_residual(h1, mod_ref, w1_ref, w2_ref, lng1_ref, lnb1_ref, alpha)


def _post_call(kern, token_inputs, h, mod, layer, per_batch, mixer_consts, tail_consts, alpha):
    n = h.shape[0]
    tm = _token_tile(n, per_batch)
    rpm = None if per_batch is None else per_batch // tm
    tok = lambda w: pl.BlockSpec((tm, w), lambda t: (t, 0))
    consts = list(mixer_consts) + list(tail_consts)
    const_specs = [_const_spec(a.shape) if a.ndim == 2 else
                   pl.BlockSpec((None,) + a.shape[1:], lambda t: (layer, 0, 0), pipeline_mode=pl.Buffered(1))
                   for a in consts]
    return pl.pallas_call(
        functools.partial(kern, alpha=alpha),
        grid=(n // tm,),
        in_specs=[tok(a.shape[1]) for a in token_inputs] + [tok(D_MODEL), _mod_spec(layer, rpm)]
                 + const_specs,
        out_specs=tok(D_MODEL),
        out_shape=jax.ShapeDtypeStruct((n, D_MODEL), F32),
        compiler_params=_cparams(("parallel",)),
    )(*token_inputs, h, mod, *consts)


def _qkv_rope_kernel(h_ref, mod_ref, wq_ref, wk_ref, wvt_ref, cos_ref, sin_ref, q_ref, k_ref, vt_ref):
    u = _modulate(h_ref[...], mod_ref, 0, 1).astype(BF16)
    cos = cos_ref[...]
    sin_signed = sin_ref[...]
    lane = lax.broadcasted_iota(jnp.int32, cos.shape, 1)
    partner_above = (lane % (2 * ROPE_Q)) < ROPE_Q

    def rope(y):
        parts = []
        for hd in range(DIFF_HEADS):
            t = y[:, hd * LANES:(hd + 1) * LANES]
            partner = jnp.where(partner_above, pltpu.roll(t, LANES - ROPE_Q, 1), pltpu.roll(t, ROPE_Q, 1))
            parts.append(t * cos + partner * sin_signed)
        return jnp.concatenate(parts, axis=1).astype(BF16)

    q_ref[...] = rope(_dot(u, wq_ref[...]))
    k_ref[...] = rope(_dot(u, wk_ref[...]))
    vt_ref[...] = _dot_nt(wvt_ref[...], u).astype(BF16)


def _qkv_plain_kernel(h_ref, mod_ref, wq_ref, wk_ref, wvt_ref, q_ref, k_ref, vt_ref):
    u = _modulate(h_ref[...], mod_ref, 0, 1).astype(BF16)
    q_ref[...] = _dot(u, wq_ref[...]).astype(BF16)
    k_ref[...] = _dot(u, wk_ref[...]).astype(BF16)
    vt_ref[...] = _dot_nt(wvt_ref[...], u).astype(BF16)


def _qkv_call(h, mod, layer, per_batch, weights, rope):
    n = h.shape[0]
    tm = _token_tile(n, per_batch)
    tok = pl.BlockSpec((tm, D_MODEL), lambda t: (t, 0))
    out_specs = [tok, tok, pl.BlockSpec((D_MODEL, tm), lambda t: (0, t))]
    out_shape = [jax.ShapeDtypeStruct((n, D_MODEL), BF16), jax.ShapeDtypeStruct((n, D_MODEL), BF16),
                 jax.ShapeDtypeStruct((D_MODEL, n), BF16)]
    wspec = _const_spec((D_MODEL, D_MODEL))
    if rope is not None:
        cos2, sin2 = rope
        rpm = per_batch // tm
        tab = pl.BlockSpec((tm, LANES), lambda t: (t % rpm, 0))
        wq, wk, wvt = weights
        return pl.pallas_call(
            _qkv_rope_kernel, grid=(n // tm,),
            in_specs=[tok, _mod_spec(layer, rpm), wspec, wspec, wspec, tab, tab],
            out_specs=out_specs, out_shape=out_shape,
            compiler_params=_cparams(("parallel",)),
        )(h, mod, wq, wk, wvt, cos2, sin2)
    wq, wk, wvt = weights
    return pl.pallas_call(
        _qkv_plain_kernel, grid=(n // tm,),
        in_specs=[tok, _mod_spec(layer, None), wspec, wspec, wspec],
        out_specs=out_specs, out_shape=out_shape,
        compiler_params=_cparams(("parallel",)),
    )(h, mod, wq, wk, wvt)


KEY_TILE = 512
SCORE_LOOKAHEAD = 1
Q_TILE = 1024
ONES_ROWS = 2 * SUBLANES


def _attn_kernel(*refs, key_lens, lam_init):
    nseg = len(key_lens)
    q_ref = refs[0]
    k_refs = refs[1:1 + nseg]
    vt_refs = refs[1 + nseg:1 + 2 * nseg]
    lam_ref, subln_ref, y_ref, acc_s = refs[1 + 2 * nseg:]
    tq = q_ref.shape[0]

    q = q_ref[...]
    lane = lax.broadcasted_iota(jnp.int32, q.shape, 1)
    zero = jnp.zeros_like(q)
    qm = [jnp.where(lane < DIFF_D, q, zero), jnp.where(lane >= DIFF_D, q, zero)]
    acc_s[...] = jnp.zeros_like(acc_s)

    tiles = []
    for k_ref, vt_ref, n_keys in zip(k_refs, vt_refs, key_lens):
        tk = min(KEY_TILE, n_keys)
        tiles += [(k_ref, vt_ref, j * tk, tk) for j in range(n_keys // tk)]

    def tile_scores(t):
        k_ref, _, k0, tk = tiles[t]
        k_t = k_ref[k0:k0 + tk, :]
        return [_dot_nt(k_t, qm[0]), _dot_nt(k_t, qm[1])]

    def absorb(s_pair, t, m_run):
        _, vt_ref, k0, tk = tiles[t]
        vt_ext = jnp.concatenate([vt_ref[:, k0:k0 + tk], jnp.ones((ONES_ROWS, tk), BF16)], axis=0)
        new = []
        for m in range(2):
            m_new = jnp.maximum(m_run[m], jnp.max(s_pair[m], axis=0, keepdims=True))
            alpha = jnp.exp2(m_run[m] - m_new)
            p = jnp.exp2(s_pair[m] - m_new).astype(BF16)
            acc_s[m] = alpha * acc_s[m] + _dot(vt_ext, p)
            new.append(m_new)
        return new

    stats = [jnp.full((1, tq), -1e30, F32)] * 2
    ahead = [tile_scores(t) for t in range(min(SCORE_LOOKAHEAD, len(tiles)))]
    for t in range(len(tiles)):
        if t + SCORE_LOOKAHEAD < len(tiles):
            ahead.append(tile_scores(t + SCORE_LOOKAHEAD))
        stats = absorb(ahead.pop(0), t, stats)

    lv = lam_ref[...]
    lam = (jnp.exp(jnp.sum(lv[0:1] * lv[1:2], axis=1, keepdims=True))
           - jnp.exp(jnp.sum(lv[2:3] * lv[3:4], axis=1, keepdims=True)) + lam_init)
    num = [acc_s[m, 0:DIFF_DV, :] for m in range(2)]
    den = [acc_s[m, DIFF_DV:DIFF_DV + 1, :] for m in range(2)]
    o_t = num[0] / den[0] - lam * (num[1] / den[1])
    ms = jnp.mean(o_t * o_t, axis=0, keepdims=True)
    y_t = o_t * lax.rsqrt(ms + NORM_EPS) * subln_ref[...] * (1.0 - lam_init)
    y_ref[...] = y_t.T.astype(BF16)


def _attn_call(q, ks, vts, lam_vec, subln_col, bsz, n_q, key_lens, lam_init):
    tq = min(Q_TILE, n_q)
    nq = n_q // tq
    nseg = len(key_lens)
    in_specs = [pl.BlockSpec((tq, LANES), lambda b, h, i: (b * nq + i, h))]
    in_specs += [pl.BlockSpec((n, LANES), lambda b, h, i: (b, h)) for n in key_lens]
    in_specs += [pl.BlockSpec((LANES, n), lambda b, h, i: (h, b)) for n in key_lens]
    in_specs += [pl.BlockSpec((4, DIFF_D), lambda b, h, i: (0, 0)),
                 pl.BlockSpec((DIFF_DV, 1), lambda b, h, i: (0, 0))]
    kern = functools.partial(_attn_kernel, key_lens=tuple(key_lens), lam_init=lam_init)
    return pl.pallas_call(
        kern,
        grid=(bsz, DIFF_HEADS, nq),
        in_specs=in_specs,
        out_specs=pl.BlockSpec((tq, LANES), lambda b, h, i: (b * nq + i, h)),
        out_shape=jax.ShapeDtypeStruct((bsz * n_q, D_MODEL), BF16),
        scratch_shapes=[pltpu.VMEM((2, DIFF_DV + ONES_ROWS, tq), F32)],
        compiler_params=_cparams(("parallel", "parallel", "arbitrary")),
    )(q, *ks, *vts, lam_vec, subln_col)


def _rope_tables(n_lat):
    rows = n_lat // GRID_W
    row = jnp.repeat(jnp.arange(rows), GRID_W).astype(F32)
    col = jnp.tile(jnp.arange(GRID_W), rows).astype(F32)
    half = DIFF_D // 2
    inv = ROPE_THETA ** (-(jnp.arange(0, half, 2, dtype=F32) / half))
    ang_r = row[:, None] * inv
    ang_c = col[:, None] * inv
    ang = jnp.concatenate([ang_r, ang_r, ang_c, ang_c], axis=-1)
    cos, sin = jnp.cos(ang), jnp.sin(ang)
    sign = jnp.where((jnp.arange(DIFF_D) % (2 * ROPE_Q)) < ROPE_Q, -1.0, 1.0).astype(F32)
    sin_signed = sin * sign
    return jnp.concatenate([cos, cos], axis=-1), jnp.concatenate([sin_signed, sin_signed], axis=-1)


def _lru_gate_weights(gate_w, gate_b, lam):
    nt = LRU_W // LANES
    per = LANES // LRU_BW
    blocks = gate_w.reshape(2, 2, nt, per, LRU_BW, LRU_BW)
    eye = jnp.eye(per, dtype=gate_w.dtype)
    dense = jnp.einsum('dgtpab,pq->dgtpaqb', blocks, eye).reshape(2, 2, nt, LANES, LANES)
    w = jnp.transpose(dense, (2, 3, 0, 1, 4)).reshape(nt, LANES, 4 * LANES)
    b = jnp.transpose(gate_b.reshape(2, 2, nt, LANES), (2, 0, 1, 3)).reshape(nt, 1, 4 * LANES)
    lm = jnp.transpose(lam.reshape(2, nt, LANES), (1, 0, 2)).reshape(nt, 1, 2 * LANES)
    return w.astype(BF16), b, lm


def _gate_columns_per_head(w_ab):
    d = w_ab.shape[0]
    per_head = jnp.transpose(w_ab.reshape(d, 4, GDN_HEADS), (0, 2, 1)).reshape(d, 4 * GDN_HEADS)
    return jnp.pad(per_head, ((0, 0), (0, LANES - 4 * GDN_HEADS)))


def _cast_kernel(x_ref, o_ref):
    o_ref[...] = x_ref[...].astype(BF16)


def _cast_bf16(w):
    n, r, c = w.shape
    tr = max(t for t in range(2 * SUBLANES, r + 1, 2 * SUBLANES)
             if r % t == 0 and t * c * w.dtype.itemsize <= CAST_BLOCK_BYTES)
    return pl.pallas_call(
        _cast_kernel,
        grid=(n, r // tr),
        in_specs=[pl.BlockSpec((None, tr, c), lambda i, j: (i, j, 0))],
        out_specs=pl.BlockSpec((None, tr, c), lambda i, j: (i, j, 0)),
        out_shape=jax.ShapeDtypeStruct(w.shape, BF16),
        compiler_params=_cparams(("parallel", "parallel")),
    )(w)


def kernel(x, c, ctx, c_ctx, ada_w, ada_b, ln_g, ln_b, mlp_w1, mlp_w2, mix_w_out, ev_w_in, ev_qkv_conv,
           ev_a_log, ev_dt_bias, ev_gdn_norm, ev_lru_conv_w, ev_lru_conv_b, ev_lru_gate_w, ev_lru_gate_b,
           ev_lru_lambda, od_w_qkv, od_lambda, od_subln):
    bsz, n_lat, d = x.shape
    n_ctx = ctx.shape[1]
    depth = ada_w.shape[0]
    assert d == D_MODEL and bsz + 1 <= MOD_ROWS
    assert n_lat % CHUNK == 0 and n_ctx % CHUNK == 0 and n_lat % GRID_W == 0
    alpha = (2 * depth) ** 0.25

    h_lat = x.reshape(bsz * n_lat, d)
    h_ctx = ctx.reshape(bsz * n_ctx, d)
    cc = jnp.concatenate([c_ctx[None, :], c, jnp.zeros((MOD_ROWS - 1 - bsz, d), F32)], axis=0)
    mod = _ada_call(cc, ada_w, ada_b).reshape(depth, MOD_ROWS, 6, d)
    cos2, sin2 = _rope_tables(n_lat)
    w1_all, w2_all, w_out_all = _cast_bf16(mlp_w1), _cast_bf16(mlp_w2), _cast_bf16(mix_w_out)

    for i in range(depth):
        last = i == depth - 1
        j = i // 2
        lng0, lnb0 = ln_g[i, 0][None, :], ln_b[i, 0][None, :]
        lng1, lnb1 = ln_g[i, 1][None, :], ln_b[i, 1][None, :]
        tail = (w1_all, w2_all, lng1, lnb1)
        if i % 2 == 0:
            w_in = ev_w_in[j]
            c0, c1, c2 = 4 * GDN_W, 4 * GDN_W + 4 * GDN_HEADS, 4 * GDN_W + 4 * GDN_HEADS + 2 * LRU_W
            w_main = jnp.concatenate([w_in[:, :c0], w_in[:, c1:c2]], axis=1).astype(BF16)
            w_ab = _gate_columns_per_head(w_in[:, c0:c1]).astype(BF16)
            qkv_l, z_l, xr_l, gate_l, ab_l = _inproj_even_call(h_lat, mod, i, n_lat, w_main, w_ab)
            qkv_c, z_c, xr_c, gate_c, ab_c = _inproj_even_call(h_ctx, mod, i, None, w_main, w_ab)
            gparams = jnp.stack([ev_a_log[j, 0], ev_a_log[j, 1], ev_dt_bias[j, 0], ev_dt_bias[j, 1]],
                                axis=-1).reshape(GDN_HEADS, 1, 4)
            o_l, o_c = _gdn_call(qkv_l, qkv_c, ab_l, ab_c, ev_qkv_conv[j], gparams, bsz, n_lat, n_ctx)
            gw, gb, lm = _lru_gate_weights(ev_lru_gate_w[j], ev_lru_gate_b[j], ev_lru_lambda[j])
            hr_l, hr_c = _lru_call(xr_l, xr_c, ev_lru_conv_w[j], ev_lru_conv_b[j][None, :], gw, gb, lm,
                                   bsz, n_lat, n_ctx)
            gn = ev_gdn_norm[j][None, :]
            mixer = (gn, w_out_all, lng0, lnb0)
            h_lat = _post_call(_post_even_kernel, (o_l, z_l, hr_l, gate_l), h_lat, mod, i, n_lat, mixer, tail,
                               alpha)
            if not last:
                h_ctx = _post_call(_post_even_kernel, (o_c, z_c, hr_c, gate_c), h_ctx, mod, i, None, mixer,
                                   tail, alpha)
        else:
            lam_init = 0.8 - 0.6 * math.exp(-0.3 * i)
            wq = od_w_qkv[j][:, :D_MODEL] * (DIFF_D ** -0.5 * math.log2(math.e))
            wk = od_w_qkv[j][:, D_MODEL:2 * D_MODEL]
            wvt = od_w_qkv[j][:, 2 * D_MODEL:].T.astype(BF16)
            qkv_w = (wq.astype(BF16), wk.astype(BF16), wvt)
            q_l, k_l, vt_l = _qkv_call(h_lat, mod, i, n_lat, qkv_w, (cos2, sin2))
            q_c, k_c, vt_c = _qkv_call(h_ctx, mod, i, None, qkv_w, None)
            subln_col = od_subln[j][:, None]
            y_l = _attn_call(q_l, [k_c, k_l], [vt_c, vt_l], od_lambda[j], subln_col, bsz, n_lat,
                             [n_ctx, n_lat], lam_init)
            mixer = (w_out_all, lng0, lnb0)
            h_lat = _post_call(_post_odd_kernel, (y_l,), h_lat, mod, i, n_lat, mixer, tail, alpha)
            if not last:
                y_c = _attn_call(q_c, [k_c], [vt_c], od_lambda[j], subln_col, bsz, n_ctx, [n_ctx], lam_init)
                h_ctx = _post_call(_post_odd_kernel, (y_c,), h_ctx, mod, i, None, mixer, tail, alpha)
    return h_lat.reshape(bsz, n_lat, d)
```

```python
import functools
import math

import jax
import jax.numpy as jnp
from jax import lax
from jax.experimental import pallas as pl
from jax.experimental.pallas import tpu as pltpu

F32 = jnp.float32
BF16 = jnp.bfloat16

D_MODEL = 1024
D_FF = 4 * D_MODEL
CONV_K = 4
CONV_LEFT = 2
GDN_HEADS = 4
GDN_DK = 128
GDN_W = GDN_HEADS * GDN_DK
CHUNK = 64
LRU_W = D_MODEL - GDN_W
LRU_BLOCKS = 8
LRU_BW = LRU_W // LRU_BLOCKS
LRU_C = 8.0
DIFF_HEADS = 8
DIFF_D = 64
DIFF_DV = 128
ROPE_Q = DIFF_D // 4
GRID_W = 64
ROPE_THETA = 10000.0
NORM_EPS = 1e-6
LANES = 128
SUBLANES = 8
MOD_ROWS = 16
TOKEN_TILE = 512
GDN_GROUP = 8
LRU_COEFF_ROWS = 256
CAST_BLOCK_BYTES = 4 * 1024 * 1024
VMEM_LIMIT = 56 * 1024 * 1024


def _cparams(sem):
    return pltpu.CompilerParams(dimension_semantics=sem, vmem_limit_bytes=VMEM_LIMIT)


def _sigmoid(x):
    return 0.5 * jnp.tanh(0.5 * x) + 0.5


def _silu(x):
    return x * _sigmoid(x)


def _softplus(x):
    return jnp.maximum(x, 0.0) + jnp.log1p(jnp.exp(-jnp.abs(x)))


def _gelu_tanh(x):
    return 0.5 * x * (1.0 + jnp.tanh(math.sqrt(2.0 / math.pi) * (x + 0.044715 * (x * x * x))))


def _modulate(h, mod_ref, shift_row, scale_row):
    return h * (1.0 + mod_ref[scale_row:scale_row + 1, :]) + mod_ref[shift_row:shift_row + 1, :]


def _layernorm(x, g, b):
    mu = jnp.mean(x, axis=-1, keepdims=True)
    xc = x - mu
    var = jnp.mean(xc * xc, axis=-1, keepdims=True)
    return xc * lax.rsqrt(var + NORM_EPS) * g + b


def _dot(a, b):
    return jnp.dot(a, b, preferred_element_type=F32)


def _dot_nt(a, b):
    return lax.dot_general(a, b, (((1,), (1,)), ((), ())), preferred_element_type=F32)


def _dot_tn(a, b):
    return lax.dot_general(a, b, (((0,), (0,)), ((), ())), preferred_element_type=F32)


def _ada_kernel(c_ref, w_ref, b_ref, o_ref):
    s = _silu(c_ref[...])
    o_ref[...] = jnp.dot(s, w_ref[...], preferred_element_type=F32,
                         precision=lax.Precision.HIGHEST) + b_ref[...]


def _ada_call(cc, ada_w, ada_b):
    depth, d, n = ada_w.shape
    tn = D_MODEL
    return pl.pallas_call(
        _ada_kernel,
        grid=(depth, n // tn),
        in_specs=[pl.BlockSpec((MOD_ROWS, d), lambda i, j: (0, 0)),
                  pl.BlockSpec((None, d, tn), lambda i, j: (i, 0, j)),
                  pl.BlockSpec((None, 1, tn), lambda i, j: (i, 0, j))],
        out_specs=pl.BlockSpec((None, MOD_ROWS, tn), lambda i, j: (i, 0, j)),
        out_shape=jax.ShapeDtypeStruct((depth, MOD_ROWS, n), F32),
        compiler_params=_cparams(("parallel", "parallel")),
    )(cc, ada_w, ada_b.reshape(depth, 1, n))


def _mod_spec(layer, rows_per_mod):
    if rows_per_mod is None:
        return pl.BlockSpec((None, None, 6, D_MODEL), lambda t: (layer, 0, 0, 0))
    return pl.BlockSpec((None, None, 6, D_MODEL), lambda t: (layer, 1 + t // rows_per_mod, 0, 0))


def _const_spec(shape):
    nd = len(shape)
    return pl.BlockSpec(shape, lambda t: (0,) * nd, pipeline_mode=pl.Buffered(1))


def _token_tile(n_rows, per_batch):
    tm = min(TOKEN_TILE, n_rows if per_batch is None else per_batch)
    assert n_rows % tm == 0 and (per_batch is None or per_batch % tm == 0)
    return tm


def _inproj_even_kernel(h_ref, mod_ref, w_ref, wab_ref, qkv_ref, z_ref, xr_ref, gate_ref, ab_ref):
    u = _modulate(h_ref[...], mod_ref, 0, 1).astype(BF16)
    qkv_ref[...] = _dot(u, w_ref[:, 0:3 * GDN_W])
    z_ref[...] = _dot(u, w_ref[:, 3 * GDN_W:4 * GDN_W]).astype(BF16)
    xr_ref[...] = _dot(u, w_ref[:, 4 * GDN_W:4 * GDN_W + LRU_W])
    gate_ref[...] = _dot(u, w_ref[:, 4 * GDN_W + LRU_W:]).astype(BF16)
    ab = _dot(u, wab_ref[...])
    for hd in range(GDN_HEADS):
        ab_ref[hd] = ab[:, 4 * hd:4 * hd + 4]


def _inproj_even_call(h, mod, layer, per_batch, w_main, w_ab):
    n = h.shape[0]
    tm = _token_tile(n, per_batch)
    rpm = None if per_batch is None else per_batch // tm
    tok = lambda w: pl.BlockSpec((tm, w), lambda t: (t, 0))
    return pl.pallas_call(
        _inproj_even_kernel,
        grid=(n // tm,),
        in_specs=[tok(D_MODEL), _mod_spec(layer, rpm), _const_spec(w_main.shape), _const_spec(w_ab.shape)],
        out_specs=[tok(3 * GDN_W), tok(GDN_W), tok(LRU_W), tok(LRU_W),
                   pl.BlockSpec((GDN_HEADS, tm, 4), lambda t: (0, t, 0))],
        out_shape=[jax.ShapeDtypeStruct((n, 3 * GDN_W), F32), jax.ShapeDtypeStruct((n, GDN_W), BF16),
                   jax.ShapeDtypeStruct((n, LRU_W), F32), jax.ShapeDtypeStruct((n, LRU_W), BF16),
                   jax.ShapeDtypeStruct((GDN_HEADS, n, 4), F32)],
        compiler_params=_cparams(("parallel",)),
    )(h, mod, w_main, w_ab)


def _dwconv(x_ref, w_ref, pad_ref, n):
    zeros = jnp.zeros((SUBLANES, LANES), F32)
    pad_ref[0:SUBLANES, :] = zeros
    pad_ref[SUBLANES + n:2 * SUBLANES + n, :] = zeros
    pad_ref[SUBLANES:SUBLANES + n, :] = x_ref[...]
    y = None
    for j in range(CONV_K):
        off = SUBLANES + j - CONV_LEFT
        term = pad_ref[off:off + n, :] * w_ref[j:j + 1, :]
        y = term if y is None else y + term
    return y


def _hp_parts(x):
    hi = x.astype(BF16).astype(F32)
    return hi, x - hi


def _hp_operands(x):
    hi, lo = _hp_parts(x)
    x2 = hi + pltpu.roll(lo, CHUNK, 1)
    lhs = jnp.concatenate([x2, x2], axis=1).astype(BF16)
    rhs = jnp.concatenate([hi, hi, lo, lo], axis=0).astype(BF16)
    return lhs, rhs


def _gdn_kernel(ql_ref, kl_ref, vl_ref, qc_ref, kc_ref, vc_ref, wq_ref, wk_ref, wv_ref,
                abl_ref, abc_ref, gp_ref, ol_ref, oc_ref,
                pad_s, q_s, k_s, v_s, ab_s, mt_s, nt_s, qe_s, gl_s, st_s, *, n_lat, n_ctx):
    n_tot = n_ctx + n_lat

    def prep(x_ref, w_ref, n, dst, base, kind):
        y = _silu(_dwconv(x_ref, w_ref, pad_s, n))
        if kind != "v":
            y = y * lax.rsqrt(jnp.sum(y * y, axis=-1, keepdims=True) + NORM_EPS)
        if kind == "q":
            y = y * (GDN_DK ** -0.5)
        dst[base:base + n, :] = y

    prep(qc_ref, wq_ref, n_ctx, q_s, 0, "q")
    prep(kc_ref, wk_ref, n_ctx, k_s, 0, "k")
    prep(vc_ref, wv_ref, n_ctx, v_s, 0, "v")
    prep(ql_ref, wq_ref, n_lat, q_s, n_ctx, "q")
    prep(kl_ref, wk_ref, n_lat, k_s, n_ctx, "k")
    prep(vl_ref, wv_ref, n_lat, v_s, n_ctx, "v")
    ab_s[0:n_ctx, :] = abc_ref[...]
    ab_s[n_ctx:n_tot, :] = abl_ref[...]

    row = lax.broadcasted_iota(jnp.int32, (CHUNK, LANES), 0)
    col = lax.broadcasted_iota(jnp.int32, (CHUNK, LANES), 1)
    left = col < CHUNK
    eye = jnp.where(row == col, 1.0, 0.0).astype(F32)
    incl = [jnp.logical_and(row >= col, left), jnp.logical_and(row <= col, left)]
    strict = [jnp.logical_and(row > col, left), jnp.logical_and(row < col, left)]
    ones_left = jnp.where(left, 1.0, 0.0).astype(F32)
    gp = gp_ref[...]

    def chunk_group(chunk0, out_ref, group, gi):
        chains = []
        for g in range(group):
            lc = gi * group + g
            c = chunk0 + lc
            rows = pl.ds(pl.multiple_of(c * CHUNK, CHUNK), CHUNK)
            orow = pl.ds(pl.multiple_of(lc * CHUNK, CHUNK), CHUNK)
            q_c = q_s[rows, :]
            k_c = k_s[rows, :]
            v_c = v_s[rows, :]
            ab_c = ab_s[rows, :]
            kb = k_c.astype(BF16)
            k_pad = jnp.concatenate([kb, jnp.zeros_like(kb)], axis=0)
            qk_kk = _dot_nt(jnp.concatenate([q_c.astype(BF16), kb], axis=0), k_pad)
            for d in range(2):
                chains.append(dict(c=c, rows=rows, orow=orow, d=d, q=q_c, k=k_c, v=v_c, ab=ab_c,
                                   qk_raw=qk_kk[0:CHUNK], kk=qk_kk[CHUNK:2 * CHUNK]))
        for ch in chains:
            d = ch["d"]
            a_col = ch["ab"][:, d:d + 1]
            b_col = ch["ab"][:, 2 + d:3 + d]
            g_col = -jnp.exp(gp[:, d:d + 1]) * _softplus(a_col + gp[:, 2 + d:3 + d])
            ch["beta"] = _sigmoid(b_col)
            g_b = jnp.broadcast_to(g_col, (CHUNK, LANES))
            m_t = incl[1 - d]
            gh, gl = _hp_parts(g_b)
            cum = jnp.where(incl[d], 1.0, 0.0).astype(F32)
            lhs1 = cum + pltpu.roll(ones_left, CHUNK, 1)
            lhs = jnp.concatenate([lhs1, lhs1], axis=1).astype(BF16)
            rhs = jnp.concatenate([gh, jnp.where(m_t, -gh, 0.0), gl, jnp.where(m_t, -gl, 0.0)],
                                  axis=0).astype(BF16)
            ch["diff"] = _dot(lhs, rhs)
            ch["tot"] = jnp.sum(g_b, axis=0, keepdims=True)
        for ch in chains:
            d = ch["d"]
            ch["decay"] = jnp.exp(jnp.where(incl[d], ch["diff"], -1e30))
            a_mat = jnp.where(strict[d], ch["beta"] * ch["kk"] * ch["decay"], 0.0)
            ch["p"] = eye - a_mat
            ch["a"] = a_mat
        for ch in chains:
            a_lhs, a_rhs = _hp_operands(ch["a"])
            ch["q2"] = _dot(a_lhs, a_rhs)
        sq = 2
        while sq * 2 < CHUNK:
            for ch in chains:
                q_lhs, q_rhs = _hp_operands(ch["q2"])
                p_lhs, _ = _hp_operands(ch["p"])
                r = _dot(jnp.concatenate([p_lhs, q_lhs], axis=0), q_rhs)
                ch["p"] = ch["p"] + r[0:CHUNK]
                ch["q2"] = r[CHUNK:2 * CHUNK]
            sq *= 2
        for ch in chains:
            ch["t"] = ch["p"] + _dot(_hp_operands(ch["p"])[0], _hp_operands(ch["q2"])[1])
        for ch in chains:
            gam = jnp.where(left, pltpu.roll(ch["diff"], CHUNK, 1), ch["diff"])
            egam = jnp.exp(gam)
            beta = ch["beta"]
            rhs_uw = jnp.concatenate([ch["k"] * (beta * egam), ch["v"] * beta], axis=1).astype(BF16)
            ch["wu"] = _dot(ch["t"][:, 0:CHUNK].astype(BF16), rhs_uw).astype(BF16)
            ch["qg"] = ch["q"] * egam
            ch["kd"] = (ch["k"] * jnp.exp(ch["tot"] - gam)).astype(BF16)
        for ch in chains:
            d, c = ch["d"], ch["c"]
            qk_m = (ch["qk_raw"] * ch["decay"])[:, 0:CHUNK].astype(BF16)
            x = _dot(qk_m, ch["wu"])
            mn = _dot_tn(ch["wu"], ch["kd"])
            qe_s[d, ch["rows"], :] = (ch["qg"] - x[:, 0:LANES]).astype(BF16)
            out_ref[ch["orow"], :] = out_ref[ch["orow"], :] + x[:, LANES:2 * LANES]
            mt_s[d, c] = mn[0:LANES].astype(BF16)
            nt_s[d, c] = mn[LANES:2 * LANES]
            gl_s[d, c] = jnp.broadcast_to(jnp.exp(ch["tot"]), (SUBLANES, LANES))

    def prep_segment(chunk0, n_chunks, out_ref):
        group = max(g for g in (GDN_GROUP, 4, 2, 1) if n_chunks % g == 0)

        def prep(gi, carry):
            chunk_group(chunk0, out_ref, group, gi)
            return carry
        lax.fori_loop(0, n_chunks // group, prep, 0)

    def rec_segment(chunk0, n_chunks, out_ref):
        def step(s, carry):
            for d in range(2):
                lc = s if d == 0 else n_chunks - 1 - s
                c = chunk0 + lc
                rows = pl.ds(pl.multiple_of(c * CHUNK, CHUNK), CHUNK)
                orow = pl.ds(pl.multiple_of(lc * CHUNK, CHUNK), CHUNK)
                st = st_s[d]
                stb = st.astype(BF16)
                r = _dot(stb, mt_s[d, c])
                out_ref[orow, :] = out_ref[orow, :] + _dot_nt(qe_s[d, rows, :], stb)
                st_s[d] = st * gl_s[d, c][0:1, :] + nt_s[d, c] - r
            return carry
        lax.fori_loop(0, n_chunks, step, 0, unroll=max(g for g in (16, 8, 4, 2, 1) if n_chunks % g == 0))

    st_s[...] = jnp.zeros_like(st_s)
    ol_ref[...] = jnp.zeros_like(ol_ref)
    oc_ref[...] = jnp.zeros_like(oc_ref)
    prep_segment(0, n_ctx // CHUNK, oc_ref)
    prep_segment(n_ctx // CHUNK, n_lat // CHUNK, ol_ref)
    rec_segment(0, n_ctx // CHUNK, oc_ref)
    rec_segment(n_ctx // CHUNK, n_lat // CHUNK, ol_ref)


def _gdn_call(qkv_l, qkv_c, ab_l, ab_c, conv_w, gparams, bsz, n_lat, n_ctx):
    n_tot = n_lat + n_ctx
    nh = GDN_HEADS
    lat = lambda off: pl.BlockSpec((n_lat, LANES), lambda b, h: (b, off + h))
    ctx = lambda off: pl.BlockSpec((n_ctx, LANES), lambda b, h: (b, off + h))
    cw = lambda off: pl.BlockSpec((CONV_K, LANES), lambda b, h: (0, off + h))
    kern = functools.partial(_gdn_kernel, n_lat=n_lat, n_ctx=n_ctx)
    return pl.pallas_call(
        kern,
        grid=(bsz, nh),
        in_specs=[lat(0), lat(nh), lat(2 * nh), ctx(0), ctx(nh), ctx(2 * nh), cw(0), cw(nh), cw(2 * nh),
                  pl.BlockSpec((None, n_lat, 4), lambda b, h: (h, b, 0)),
                  pl.BlockSpec((None, n_ctx, 4), lambda b, h: (h, b, 0)),
                  pl.BlockSpec((None, 1, 4), lambda b, h: (h, 0, 0))],
        out_specs=[pl.BlockSpec((n_lat, LANES), lambda b, h: (b, h)),
                   pl.BlockSpec((n_ctx, LANES), lambda b, h: (b, h))],
        out_shape=[jax.ShapeDtypeStruct((bsz * n_lat, GDN_W), F32),
                   jax.ShapeDtypeStruct((bsz * n_ctx, GDN_W), F32)],
        scratch_shapes=[
            pltpu.VMEM((n_lat + 2 * SUBLANES, LANES), F32),
            pltpu.VMEM((n_tot, LANES), F32),
            pltpu.VMEM((n_tot, LANES), F32),
            pltpu.VMEM((n_tot, LANES), F32),
            pltpu.VMEM((n_tot, 4), F32),
            pltpu.VMEM((2, n_tot // CHUNK, GDN_DK, LANES), BF16),
            pltpu.VMEM((2, n_tot // CHUNK, LANES, GDN_DK), F32),
            pltpu.VMEM((2, n_tot, LANES), BF16),
            pltpu.VMEM((2, n_tot // CHUNK, SUBLANES, LANES), F32),
            pltpu.VMEM((2, GDN_DK, LANES), F32),
        ],
        compiler_params=_cparams(("parallel", "arbitrary")),
    )(qkv_l, qkv_l, qkv_l, qkv_c, qkv_c, qkv_c, conv_w, conv_w, conv_w, ab_l, ab_c, gparams)


def _lru_kernel(xl_ref, xc_ref, cw_ref, cb_ref, gw_ref, gb_ref, lam_ref, hl_ref, hc_ref,
                pad_s, x_s, a_s, u_s, *, n_lat, n_ctx):
    n_tot = n_ctx + n_lat
    x_s[0:n_ctx, :] = _dwconv(xc_ref, cw_ref, pad_s, n_ctx) + cb_ref[...]
    x_s[n_ctx:n_tot, :] = _dwconv(xl_ref, cw_ref, pad_s, n_lat) + cb_ref[...]

    sp = [_softplus(-lam_ref[:, d * LANES:(d + 1) * LANES]) for d in range(2)]
    tile = min(LRU_COEFF_ROWS, n_ctx)

    def coeffs(t, carry):
        rows = pl.ds(pl.multiple_of(t * tile, tile), tile)
        x = x_s[rows, :]
        gates = _dot(x.astype(BF16), gw_ref[...]) + gb_ref[...]
        for d in range(2):
            r = _sigmoid(gates[:, (2 * d) * LANES:(2 * d + 1) * LANES])
            i = _sigmoid(gates[:, (2 * d + 1) * LANES:(2 * d + 2) * LANES])
            a = jnp.exp(-LRU_C * r * sp[d])
            a_s[d, rows, :] = a
            u_s[d, rows, :] = jnp.sqrt(1.0 - a * a) * (i * x)
        return carry

    lax.fori_loop(0, n_tot // tile, coeffs, 0)

    sub = lax.broadcasted_iota(jnp.int32, (SUBLANES, LANES), 0)

    def group_scan(d, g0):
        rows = pl.ds(pl.multiple_of(g0, SUBLANES), SUBLANES)
        a = a_s[d, rows, :]
        u = u_s[d, rows, :]
        sh = 1
        while sh < SUBLANES:
            if d == 0:
                a_sh, u_sh, ok = pltpu.roll(a, sh, 0), pltpu.roll(u, sh, 0), sub >= sh
            else:
                a_sh, u_sh = pltpu.roll(a, SUBLANES - sh, 0), pltpu.roll(u, SUBLANES - sh, 0)
                ok = sub < SUBLANES - sh
            u = u + a * jnp.where(ok, u_sh, 0.0)
            a = a * jnp.where(ok, a_sh, 1.0)
            sh *= 2
        return rows, u, a

    def fwd(base, n, h0):
        def body(g, h):
            rows, u, a = group_scan(0, base + g * SUBLANES)
            hh = u + a * h
            u_s[0, rows, :] = hh
            return jnp.broadcast_to(hh[SUBLANES - 1:SUBLANES, :], (SUBLANES, LANES))
        return lax.fori_loop(0, n // SUBLANES, body, h0, unroll=16)

    def bwd(base, n, h0, out_ref):
        def body(g, h):
            lg = n // SUBLANES - 1 - g
            rows, u, a = group_scan(1, base + lg * SUBLANES)
            hh = u + a * h
            orow = pl.ds(pl.multiple_of(lg * SUBLANES, SUBLANES), SUBLANES)
            out_ref[orow, :] = u_s[0, rows, :] + hh
            return jnp.broadcast_to(hh[0:1, :], (SUBLANES, LANES))
        return lax.fori_loop(0, n // SUBLANES, body, h0, unroll=16)

    zero = jnp.zeros((SUBLANES, LANES), F32)
    h = fwd(0, n_ctx, zero)
    fwd(n_ctx, n_lat, h)
    h = bwd(0, n_ctx, zero, hc_ref)
    bwd(n_ctx, n_lat, h, hl_ref)


def _lru_call(xr_l, xr_c, conv_w, conv_b, gate_w, gate_b, lam, bsz, n_lat, n_ctx):
    n_tot = n_lat + n_ctx
    nt = LRU_W // LANES
    kern = functools.partial(_lru_kernel, n_lat=n_lat, n_ctx=n_ctx)
    return pl.pallas_call(
        kern,
        grid=(bsz, nt),
        in_specs=[pl.BlockSpec((n_lat, LANES), lambda b, j: (b, j)),
                  pl.BlockSpec((n_ctx, LANES), lambda b, j: (b, j)),
                  pl.BlockSpec((CONV_K, LANES), lambda b, j: (0, j)),
                  pl.BlockSpec((1, LANES), lambda b, j: (0, j)),
                  pl.BlockSpec((None, LANES, 4 * LANES), lambda b, j: (j, 0, 0)),
                  pl.BlockSpec((None, 1, 4 * LANES), lambda b, j: (j, 0, 0)),
                  pl.BlockSpec((None, 1, 2 * LANES), lambda b, j: (j, 0, 0))],
        out_specs=[pl.BlockSpec((n_lat, LANES), lambda b, j: (b, j)),
                   pl.BlockSpec((n_ctx, LANES), lambda b, j: (b, j))],
        out_shape=[jax.ShapeDtypeStruct((bsz * n_lat, LRU_W), F32),
                   jax.ShapeDtypeStruct((bsz * n_ctx, LRU_W), F32)],
        scratch_shapes=[pltpu.VMEM((n_lat + 2 * SUBLANES, LANES), F32),
                        pltpu.VMEM((n_tot, LANES), F32),
                        pltpu.VMEM((2, n_tot, LANES), F32),
                        pltpu.VMEM((2, n_tot, LANES), F32)],
        compiler_params=_cparams(("parallel", "arbitrary")),
    )(xr_l, xr_c, conv_w, conv_b, gate_w, gate_b, lam)


def _residual_ln(h, y, mod_ref, gate_row, lng_ref, lnb_ref, alpha):
    return _layernorm(alpha * h + mod_ref[gate_row:gate_row + 1, :] * y, lng_ref[...], lnb_ref[...])


FF_TILE = 1024


def _mlp_residual(h, mod_ref, w1_ref, w2_ref, lng_ref, lnb_ref, alpha):
    u = _modulate(h, mod_ref, 3, 4).astype(BF16)
    acc = None
    for j in range(D_FF // FF_TILE):
        sl = slice(j * FF_TILE, (j + 1) * FF_TILE)
        a = jnp.maximum(_dot(u, w1_ref[:, sl]), 0.0)
        part = _dot((a * a).astype(BF16), w2_ref[sl, :])
        acc = part if acc is None else acc + part
    return _residual_ln(h, acc, mod_ref, 5, lng_ref, lnb_ref, alpha)


def _post_even_kernel(o_ref, z_ref, hr_ref, gate_ref, h_ref, mod_ref, gn_ref, w_ref, lng0_ref, lnb0_ref,
                      w1_ref, w2_ref, lng1_ref, lnb1_ref, out_ref, *, alpha):
    parts = []
    for hd in range(GDN_HEADS):
        sl = slice(hd * LANES, (hd + 1) * LANES)
        o = o_ref[:, sl]
        on = o * lax.rsqrt(jnp.mean(o * o, axis=-1, keepdims=True) + NORM_EPS) * gn_ref[...]
        parts.append(on * _silu(z_ref[:, sl].astype(F32)))
    y_gdn = jnp.concatenate(parts, axis=1).astype(BF16)
    y_lru = (hr_ref[...] * _gelu_tanh(gate_ref[...].astype(F32))).astype(BF16)
    y = _dot(y_gdn, w_ref[0:GDN_W, :]) + _dot(y_lru, w_ref[GDN_W:, :])
    h1 = _residual_ln(h_ref[...], y, mod_ref, 2, lng0_ref, lnb0_ref, alpha)
    out_ref[...] = _mlp_residual(h1, mod_ref, w1_ref, w2_ref, lng1_ref, lnb1_ref, alpha)


def _post_odd_kernel(y_ref, h_ref, mod_ref, w_ref, lng0_ref, lnb0_ref, w1_ref, w2_ref, lng1_ref, lnb1_ref,
                     out_ref, *, alpha):
    y = _dot(y_ref[...], w_ref[...])
    h1 = _residual_ln(h_ref[...], y, mod_ref, 2, lng0_ref, lnb0_ref, alpha)
    out_ref[...] = _mlp_residual(h1, mod_ref, w1_ref, w2_ref, lng1_ref, lnb1_ref, alpha)


def _post_call(kern, token_inputs, h, mod, layer, per_batch, mixer_consts, tail_consts, alpha):
    n = h.shape[0]
    tm = _token_tile(n, per_batch)
    rpm = None if per_batch is None else per_batch // tm
    tok = lambda w: pl.BlockSpec((tm, w), lambda t: (t, 0))
    consts = list(mixer_consts) + list(tail_consts)
    const_specs = [_const_spec(a.shape) if a.ndim == 2 else
                   pl.BlockSpec((None,) + a.shape[1:], lambda t: (layer, 0, 0), pipeline_mode=pl.Buffered(1))
                   for a in consts]
    return pl.pallas_call(
        functools.partial(kern, alpha=alpha),
        grid=(n // tm,),
        in_specs=[tok(a.shape[1]) for a in token_inputs] + [tok(D_MODEL), _mod_spec(layer, rpm)]
                 + const_specs,
        out_specs=tok(D_MODEL),
        out_shape=jax.ShapeDtypeStruct((n, D_MODEL), F32),
        compiler_params=_cparams(("parallel",)),
    )(*token_inputs, h, mod, *consts)


def _qkv_rope_kernel(h_ref, mod_ref, wq_ref, wk_ref, wvt_ref, cos_ref, sin_ref, q_ref, k_ref, vt_ref):
    u = _modulate(h_ref[...], mod_ref, 0, 1).astype(BF16)
    cos = cos_ref[...]
    sin_signed = sin_ref[...]
    lane = lax.broadcasted_iota(jnp.int32, cos.shape, 1)
    partner_above = (lane % (2 * ROPE_Q)) < ROPE_Q

    def rope(y):
        parts = []
        for hd in range(DIFF_HEADS):
            t = y[:, hd * LANES:(hd + 1) * LANES]
            partner = jnp.where(partner_above, pltpu.roll(t, LANES - ROPE_Q, 1), pltpu.roll(t, ROPE_Q, 1))
            parts.append(t * cos + partner * sin_signed)
        return jnp.concatenate(parts, axis=1).astype(BF16)

    q_ref[...] = rope(_dot(u, wq_ref[...]))
    k_ref[...] = rope(_dot(u, wk_ref[...]))
    vt_ref[...] = _dot_nt(wvt_ref[...], u).astype(BF16)


def _qkv_plain_kernel(h_ref, mod_ref, wq_ref, wk_ref, wvt_ref, q_ref, k_ref, vt_ref):
    u = _modulate(h_ref[...], mod_ref, 0, 1).astype(BF16)
    q_ref[...] = _dot(u, wq_ref[...]).astype(BF16)
    k_ref[...] = _dot(u, wk_ref[...]).astype(BF16)
    vt_ref[...] = _dot_nt(wvt_ref[...], u).astype(BF16)


def _qkv_call(h, mod, layer, per_batch, weights, rope):
    n = h.shape[0]
    tm = _token_tile(n, per_batch)
    tok = pl.BlockSpec((tm, D_MODEL), lambda t: (t, 0))
    out_specs = [tok, tok, pl.BlockSpec((D_MODEL, tm), lambda t: (0, t))]
    out_shape = [jax.ShapeDtypeStruct((n, D_MODEL), BF16), jax.ShapeDtypeStruct((n, D_MODEL), BF16),
                 jax.ShapeDtypeStruct((D_MODEL, n), BF16)]
    wspec = _const_spec((D_MODEL, D_MODEL))
    if rope is not None:
        cos2, sin2 = rope
        rpm = per_batch // tm
        tab = pl.BlockSpec((tm, LANES), lambda t: (t % rpm, 0))
        wq, wk, wvt = weights
        return pl.pallas_call(
            _qkv_rope_kernel, grid=(n // tm,),
            in_specs=[tok, _mod_spec(layer, rpm), wspec, wspec, wspec, tab, tab],
            out_specs=out_specs, out_shape=out_shape,
            compiler_params=_cparams(("parallel",)),
        )(h, mod, wq, wk, wvt, cos2, sin2)
    wq, wk, wvt = weights
    return pl.pallas_call(
        _qkv_plain_kernel, grid=(n // tm,),
        in_specs=[tok, _mod_spec(layer, None), wspec, wspec, wspec],
        out_specs=out_specs, out_shape=out_shape,
        compiler_params=_cparams(("parallel",)),
    )(h, mod, wq, wk, wvt)


KEY_TILE = 512
SCORE_LOOKAHEAD = 1
Q_TILE = 1024
ONES_ROWS = 2 * SUBLANES


def _attn_kernel(*refs, key_lens, lam_init):
    nseg = len(key_lens)
    q_ref = refs[0]
    k_refs = refs[1:1 + nseg]
    vt_refs = refs[1 + nseg:1 + 2 * nseg]
    lam_ref, subln_ref, y_ref, acc_s = refs[1 + 2 * nseg:]
    tq = q_ref.shape[0]

    q = q_ref[...]
    lane = lax.broadcasted_iota(jnp.int32, q.shape, 1)
    zero = jnp.zeros_like(q)
    qm = [jnp.where(lane < DIFF_D, q, zero), jnp.where(lane >= DIFF_D, q, zero)]
    acc_s[...] = jnp.zeros_like(acc_s)

    tiles = []
    for k_ref, vt_ref, n_keys in zip(k_refs, vt_refs, key_lens):
        tk = min(KEY_TILE, n_keys)
        tiles += [(k_ref, vt_ref, j * tk, tk) for j in range(n_keys // tk)]

    def tile_scores(t):
        k_ref, _, k0, tk = tiles[t]
        k_t = k_ref[k0:k0 + tk, :]
        return [_dot_nt(k_t, qm[0]), _dot_nt(k_t, qm[1])]

    def absorb(s_pair, t, m_run):
        _, vt_ref, k0, tk = tiles[t]
        vt_ext = jnp.concatenate([vt_ref[:, k0:k0 + tk], jnp.ones((ONES_ROWS, tk), BF16)], axis=0)
        new = []
        for m in range(2):
            m_new = jnp.maximum(m_run[m], jnp.max(s_pair[m], axis=0, keepdims=True))
            alpha = jnp.exp2(m_run[m] - m_new)
            p = jnp.exp2(s_pair[m] - m_new).astype(BF16)
            acc_s[m] = alpha * acc_s[m] + _dot(vt_ext, p)
            new.append(m_new)
        return new

    stats = [jnp.full((1, tq), -1e30, F32)] * 2
    ahead = [tile_scores(t) for t in range(min(SCORE_LOOKAHEAD, len(tiles)))]
    for t in range(len(tiles)):
        if t + SCORE_LOOKAHEAD < len(tiles):
            ahead.append(tile_scores(t + SCORE_LOOKAHEAD))
        stats = absorb(ahead.pop(0), t, stats)

    lv = lam_ref[...]
    lam = (jnp.exp(jnp.sum(lv[0:1] * lv[1:2], axis=1, keepdims=True))
           - jnp.exp(jnp.sum(lv[2:3] * lv[3:4], axis=1, keepdims=True)) + lam_init)
    num = [acc_s[m, 0:DIFF_DV, :] for m in range(2)]
    den = [acc_s[m, DIFF_DV:DIFF_DV + 1, :] for m in range(2)]
    o_t = num[0] / den[0] - lam * (num[1] / den[1])
    ms = jnp.mean(o_t * o_t, axis=0, keepdims=True)
    y_t = o_t * lax.rsqrt(ms + NORM_EPS) * subln_ref[...] * (1.0 - lam_init)
    y_ref[...] = y_t.T.astype(BF16)


def _attn_call(q, ks, vts, lam_vec, subln_col, bsz, n_q, key_lens, lam_init):
    tq = min(Q_TILE, n_q)
    nq = n_q // tq
    nseg = len(key_lens)
    in_specs = [pl.BlockSpec((tq, LANES), lambda b, h, i: (b * nq + i, h))]
    in_specs += [pl.BlockSpec((n, LANES), lambda b, h, i: (b, h)) for n in key_lens]
    in_specs += [pl.BlockSpec((LANES, n), lambda b, h, i: (h, b)) for n in key_lens]
    in_specs += [pl.BlockSpec((4, DIFF_D), lambda b, h, i: (0, 0)),
                 pl.BlockSpec((DIFF_DV, 1), lambda b, h, i: (0, 0))]
    kern = functools.partial(_attn_kernel, key_lens=tuple(key_lens), lam_init=lam_init)
    return pl.pallas_call(
        kern,
        grid=(bsz, DIFF_HEADS, nq),
        in_specs=in_specs,
        out_specs=pl.BlockSpec((tq, LANES), lambda b, h, i: (b * nq + i, h)),
        out_shape=jax.ShapeDtypeStruct((bsz * n_q, D_MODEL), BF16),
        scratch_shapes=[pltpu.VMEM((2, DIFF_DV + ONES_ROWS, tq), F32)],
        compiler_params=_cparams(("parallel", "parallel", "arbitrary")),
    )(q, *ks, *vts, lam_vec, subln_col)


def _rope_tables(n_lat):
    rows = n_lat // GRID_W
    row = jnp.repeat(jnp.arange(rows), GRID_W).astype(F32)
    col = jnp.tile(jnp.arange(GRID_W), rows).astype(F32)
    half = DIFF_D // 2
    inv = ROPE_THETA ** (-(jnp.arange(0, half, 2, dtype=F32) / half))
    ang_r = row[:, None] * inv
    ang_c = col[:, None] * inv
    ang = jnp.concatenate([ang_r, ang_r, ang_c, ang_c], axis=-1)
    cos, sin = jnp.cos(ang), jnp.sin(ang)
    sign = jnp.where((jnp.arange(DIFF_D) % (2 * ROPE_Q)) < ROPE_Q, -1.0, 1.0).astype(F32)
    sin_signed = sin * sign
    return jnp.concatenate([cos, cos], axis=-1), jnp.concatenate([sin_signed, sin_signed], axis=-1)


def _lru_gate_weights(gate_w, gate_b, lam):
    nt = LRU_W // LANES
    per = LANES // LRU_BW
    blocks = gate_w.reshape(2, 2, nt, per, LRU_BW, LRU_BW)
    eye = jnp.eye(per, dtype=gate_w.dtype)
    dense = jnp.einsum('dgtpab,pq->dgtpaqb', blocks, eye).reshape(2, 2, nt, LANES, LANES)
    w = jnp.transpose(dense, (2, 3, 0, 1, 4)).reshape(nt, LANES, 4 * LANES)
    b = jnp.transpose(gate_b.reshape(2, 2, nt, LANES), (2, 0, 1, 3)).reshape(nt, 1, 4 * LANES)
    lm = jnp.transpose(lam.reshape(2, nt, LANES), (1, 0, 2)).reshape(nt, 1, 2 * LANES)
    return w.astype(BF16), b, lm


def _gate_columns_per_head(w_ab):
    d = w_ab.shape[0]
    per_head = jnp.transpose(w_ab.reshape(d, 4, GDN_HEADS), (0, 2, 1)).reshape(d, 4 * GDN_HEADS)
    return jnp.pad(per_head, ((0, 0), (0, LANES - 4 * GDN_HEADS)))


def _cast_kernel(x_ref, o_ref):
    o_ref[...] = x_ref[...].astype(BF16)


def _cast_bf16(w):
    n, r, c = w.shape
    tr = max(t for t in range(2 * SUBLANES, r + 1, 2 * SUBLANES)
             if r % t == 0 and t * c * w.dtype.itemsize <= CAST_BLOCK_BYTES)
    return pl.pallas_call(
        _cast_kernel,
        grid=(n, r // tr),
        in_specs=[pl.BlockSpec((None, tr, c), lambda i, j: (i, j, 0))],
        out_specs=pl.BlockSpec((None, tr, c), lambda i, j: (i, j, 0)),
        out_shape=jax.ShapeDtypeStruct(w.shape, BF16),
        compiler_params=_cparams(("parallel", "parallel")),
    )(w)


def kernel(x, c, ctx, c_ctx, ada_w, ada_b, ln_g, ln_b, mlp_w1, mlp_w2, mix_w_out, ev_w_in, ev_qkv_conv,
           ev_a_log, ev_dt_bias, ev_gdn_norm, ev_lru_conv_w, ev_lru_conv_b, ev_lru_gate_w, ev_lru_gate_b,
           ev_lru_lambda, od_w_qkv, od_lambda, od_subln):
    bsz, n_lat, d = x.shape
    n_ctx = ctx.shape[1]
    depth = ada_w.shape[0]
    assert d == D_MODEL and bsz + 1 <= MOD_ROWS
    assert n_lat % CHUNK == 0 and n_ctx % CHUNK == 0 and n_lat % GRID_W == 0
    alpha = (2 * depth) ** 0.25

    h_lat = x.reshape(bsz * n_lat, d)
    h_ctx = ctx.reshape(bsz * n_ctx, d)
    cc = jnp.concatenate([c_ctx[None, :], c, jnp.zeros((MOD_ROWS - 1 - bsz, d), F32)], axis=0)
    mod = _ada_call(cc, ada_w, ada_b).reshape(depth, MOD_ROWS, 6, d)
    cos2, sin2 = _rope_tables(n_lat)
    w1_all, w2_all, w_out_all = _cast_bf16(mlp_w1), _cast_bf16(mlp_w2), _cast_bf16(mix_w_out)

    for i in range(depth):
        last = i == depth - 1
        j = i // 2
        lng0, lnb0 = ln_g[i, 0][None, :], ln_b[i, 0][None, :]
        lng1, lnb1 = ln_g[i, 1][None, :], ln_b[i, 1][None, :]
        tail = (w1_all, w2_all, lng1, lnb1)
        if i % 2 == 0:
            w_in = ev_w_in[j]
            c0, c1, c2 = 4 * GDN_W, 4 * GDN_W + 4 * GDN_HEADS, 4 * GDN_W + 4 * GDN_HEADS + 2 * LRU_W
            w_main = jnp.concatenate([w_in[:, :c0], w_in[:, c1:c2]], axis=1).astype(BF16)
            w_ab = _gate_columns_per_head(w_in[:, c0:c1]).astype(BF16)
            qkv_l, z_l, xr_l, gate_l, ab_l = _inproj_even_call(h_lat, mod, i, n_lat, w_main, w_ab)
            qkv_c, z_c, xr_c, gate_c, ab_c = _inproj_even_call(h_ctx, mod, i, None, w_main, w_ab)
            gparams = jnp.stack([ev_a_log[j, 0], ev_a_log[j, 1], ev_dt_bias[j, 0], ev_dt_bias[j, 1]],
                                axis=-1).reshape(GDN_HEADS, 1, 4)
            o_l, o_c = _gdn_call(qkv_l, qkv_c, ab_l, ab_c, ev_qkv_conv[j], gparams, bsz, n_lat, n_ctx)
            gw, gb, lm = _lru_gate_weights(ev_lru_gate_w[j], ev_lru_gate_b[j], ev_lru_lambda[j])
            hr_l, hr_c = _lru_call(xr_l, xr_c, ev_lru_conv_w[j], ev_lru_conv_b[j][None, :], gw, gb, lm,
                                   bsz, n_lat, n_ctx)
            gn = ev_gdn_norm[j][None, :]
            mixer = (gn, w_out_all, lng0, lnb0)
            h_lat = _post_call(_post_even_kernel, (o_l, z_l, hr_l, gate_l), h_lat, mod, i, n_lat, mixer, tail,
                               alpha)
            if not last:
                h_ctx = _post_call(_post_even_kernel, (o_c, z_c, hr_c, gate_c), h_ctx, mod, i, None, mixer,
                                   tail, alpha)
        else:
            lam_init = 0.8 - 0.6 * math.exp(-0.3 * i)
            wq = od_w_qkv[j][:, :D_MODEL] * (DIFF_D ** -0.5 * math.log2(math.e))
            wk = od_w_qkv[j][:, D_MODEL:2 * D_MODEL]
            wvt = od_w_qkv[j][:, 2 * D_MODEL:].T.astype(BF16)
            qkv_w = (wq.astype(BF16), wk.astype(BF16), wvt)
            q_l, k_l, vt_l = _qkv_call(h_lat, mod, i, n_lat, qkv_w, (cos2, sin2))
            q_c, k_c, vt_c = _qkv_call(h_ctx, mod, i, None, qkv_w, None)
            subln_col = od_subln[j][:, None]
            y_l = _attn_call(q_l, [k_c, k_l], [vt_c, vt_l], od_lambda[j], subln_col, bsz, n_lat,
                             [n_ctx, n_lat], lam_init)
            mixer = (w_out_all, lng0, lnb0)
            h_lat = _post_call(_post_odd_kernel, (y_l,), h_lat, mod, i, n_lat, mixer, tail, alpha)
            if not last:
                y_c = _attn_call(q_c, [k_c], [vt_c], od_lambda[j], subln_col, bsz, n_ctx, [n_ctx], lam_init)
                h_ctx = _post_call(_post_odd_kernel, (y_c,), h_ctx, mod, i, None, mixer, tail, alpha)
    return h_lat.reshape(bsz, n_lat, d)
```
